```python
import jax
import jax.numpy as jnp
from jax import lax
import numpy as np

D_MODEL = 1024
BATCH = 16
SEQ = 2048
DEPTH = 1

N_MOD = 9
D_FF = 2816
EPS = 1e-6
ATT_GROUPS = ((128, 1), (512, 4), (2048, 16))
ATT_HEADS_PER_GROUP = 4
ATT_HEADS = ATT_HEADS_PER_GROUP * len(ATT_GROUPS)
ATT_HEAD_DIM = 64
ATT_WIDTH = ATT_HEADS * ATT_HEAD_DIM
ATT_OUT_WIDTH = ATT_HEADS_PER_GROUP * ATT_HEAD_DIM
ALIBI_MAX = 8.0
GLA_HEADS = 4
GLA_KEY_DIM = D_MODEL // 2
GLA_VAL_DIM = D_MODEL
GLA_DK = GLA_KEY_DIM // GLA_HEADS
GLA_DV = GLA_VAL_DIM // GLA_HEADS
GLA_GATE_RANK = 16
GLA_TAU = 16.0
GLA_CHUNK = 64
IN_SPLITS = (ATT_WIDTH, ATT_WIDTH, ATT_WIDTH, GLA_KEY_DIM, GLA_KEY_DIM, GLA_VAL_DIM, GLA_VAL_DIM, GLA_GATE_RANK, D_MODEL, D_MODEL)
IN_WIDTH = sum(IN_SPLITS)

kernel_name = 'hybrid_dilated_attn_gla_macaron_adaln'


def rms_norm(x, g):
    x32 = x.astype(jnp.float32)
    y = x32 * lax.rsqrt(jnp.mean(x32 * x32, axis=-1, keepdims=True) + EPS)
    return (y * g.astype(jnp.float32)).astype(x.dtype)


def modulate(h, shift, scale):
    return h * (1 + scale[:, None, :]) + shift[:, None, :]


def swiglu(u, w1, w3, w2):
    return (jax.nn.silu(u @ w1) * (u @ w3)) @ w2


def alibi_slopes(n):
    return 2.0 ** (-ALIBI_MAX * jnp.arange(1, n + 1, dtype=jnp.float32) / n)


def dilated_window_attention(q, k, v, slopes, window, dilation):
    B, S, H, Dh = q.shape
    blk = window // dilation
    L = S // dilation
    nb = -(-L // blk)
    pad = nb * blk - L

    def to_blocks(t):
        t = t.reshape(B, L, dilation, H, Dh)
        t = jnp.pad(t, ((0, 0), (0, pad), (0, 0), (0, 0), (0, 0)))
        return t.reshape(B, nb, blk, dilation, H, Dh)

    def band(t):
        prev = jnp.pad(t[:, :-1], ((0, 0), (1, 0), (0, 0), (0, 0), (0, 0), (0, 0)))
        return jnp.concatenate([prev, t], axis=2)

    qb = to_blocks(q)
    kk = band(to_blocks(k))
    vv = band(to_blocks(v))
    s = jnp.einsum('bnqrhd,bnkrhd->bnrhqk', qb, kk).astype(jnp.float32)
    qi = jnp.arange(blk)[:, None]
    kj = jnp.arange(2 * blk)[None, :]
    steps = qi + blk - kj
    valid = (steps >= 0) & (steps <= blk)
    valid = valid[None] & ~((jnp.arange(nb) == 0)[:, None, None] & (kj < blk)[None])
    bias = -(slopes * dilation)[:, None, None] * steps.astype(jnp.float32)
    s = jnp.where(valid[None, :, None, None], s + bias, -jnp.inf)
    m = jnp.max(s, axis=-1, keepdims=True)
    e = jnp.exp(s - m)
    den = jnp.sum(e, axis=-1, keepdims=True)
    p = (e / den).astype(v.dtype)
    lse = (m + jnp.log(den))[..., 0]
    o = jnp.einsum('bnrhqk,bnkrhd->bnqrhd', p, vv)
    o = o.reshape(B, nb * blk, dilation, H, Dh)[:, :L].reshape(B, S, H, Dh)
    lse = lse.transpose(0, 1, 4, 2, 3).reshape(B, nb * blk, dilation, H)[:, :L].reshape(B, S, H)
    return o, lse


def gla_chunked(q, k, v, log_a):
    B, S, H, Dk = q.shape
    Dv = v.shape[-1]
    C = GLA_CHUNK
    N = S // C

    def chunks(t):
        return t.astype(jnp.float32).reshape(B, N, C, H, t.shape[-1]).transpose(1, 0, 3, 2, 4)

    qc, kc, vc = chunks(q), chunks(k), chunks(v)
    bc = jnp.cumsum(chunks(log_a), axis=3)
    causal = jnp.tril(jnp.ones((C, C), dtype=bool))

    def step(state, inp):
        q_, k_, v_, b_ = inp
        b_last = b_[:, :, -1:, :]
        q_dec = q_ * jnp.exp(b_)
        attn = jnp.einsum('bhid,bhjd->bhij', q_dec, k_ * jnp.exp(-b_))
        attn = jnp.where(causal, attn, 0.0)
        o = jnp.einsum('bhij,bhjv->bhiv', attn, v_) + jnp.einsum('bhid,bhdv->bhiv', q_dec, state)
        state = (jnp.exp(b_last)[:, :, 0, :, None] * state
                 + jnp.einsum('bhjd,bhjv->bhdv', k_ * jnp.exp(b_last - b_), v_))
        return state, o

    state0 = jnp.zeros((B, H, Dk, Dv), jnp.float32)
    _, o = lax.scan(step, state0, (qc, kc, vc, bc))
    return o.transpose(1, 0, 3, 2, 4).reshape(B, S, H, Dv)


def token_mixing(u, w_in, q_norm_g, k_norm_g, gla_gate_up, gla_gate_bias, gla_out_norm_g,
                 w_branch_att, w_branch_gla, w_out):
    B, S, _ = u.shape
    split_points = [int(i) for i in np.cumsum(IN_SPLITS)[:-1]]
    (aq, ak, av, gq, gk, gv, gr, g_down, gate_att, gate_gla) = jnp.split(u @ w_in, split_points, axis=-1)

    aq = rms_norm(aq.reshape(B, S, ATT_HEADS, ATT_HEAD_DIM), q_norm_g) * (ATT_HEAD_DIM ** -0.5)
    ak = rms_norm(ak.reshape(B, S, ATT_HEADS, ATT_HEAD_DIM), k_norm_g)
    av = av.reshape(B, S, ATT_HEADS, ATT_HEAD_DIM)
    slopes = alibi_slopes(ATT_HEADS)
    outs, lses = [], []
    for gi, (window, dilation) in enumerate(ATT_GROUPS):
        hs = slice(gi * ATT_HEADS_PER_GROUP, (gi + 1) * ATT_HEADS_PER_GROUP)
        o_g, lse_g = dilated_window_attention(aq[:, :, hs], ak[:, :, hs], av[:, :, hs], slopes[hs], window, dilation)
        outs.append(o_g)
        lses.append(lse_g)
    mix_w = jax.nn.softmax(jnp.stack(lses), axis=0).astype(u.dtype)
    o_att = jnp.sum(mix_w[..., None] * jnp.stack(outs), axis=0).reshape(B, S, ATT_OUT_WIDTH)

    gq = gq.reshape(B, S, GLA_HEADS, GLA_DK) * (GLA_DK ** -0.5)
    gk = gk.reshape(B, S, GLA_HEADS, GLA_DK)
    gv = gv.reshape(B, S, GLA_HEADS, GLA_DV)
    log_a = jax.nn.log_sigmoid((g_down @ gla_gate_up + gla_gate_bias).astype(jnp.float32)) / GLA_TAU
    o_gla = gla_chunked(gq, gk, gv, log_a.reshape(B, S, GLA_HEADS, GLA_DK))
    o_gla = rms_norm(o_gla, gla_out_norm_g).astype(u.dtype) * jax.nn.silu(gr).reshape(B, S, GLA_HEADS, GLA_DV)
    o_gla = o_gla.reshape(B, S, GLA_VAL_DIM)

    merged = (jax.nn.sigmoid(gate_att) * (o_att @ w_branch_att)
              + jax.nn.sigmoid(gate_gla) * (o_gla @ w_branch_gla))
    return merged @ w_out


def setup_inputs(seed: int = 0) -> dict:
    key = jax.random.key(seed)
    ks = jax.random.split(key, 22)
    f32 = jnp.float32

    def nrm(k, shape, fan_in, scale=1.0):
        return jax.random.normal(k, shape, f32) * (scale * fan_in ** -0.5)

    def gain(k, shape):
        return 1.0 + 0.05 * jax.random.normal(k, shape, f32)

    return {
        'x': jax.random.normal(ks[0], (BATCH, SEQ, D_MODEL), f32),
        'c': jax.random.normal(ks[1], (BATCH, D_MODEL), f32),
        'w_mod': nrm(ks[2], (DEPTH, D_MODEL, N_MOD * D_MODEL), D_MODEL, 0.2),
        'b_mod': 0.01 * jax.random.normal(ks[3], (DEPTH, N_MOD * D_MODEL), f32),
        'g_ffn1': gain(ks[4], (DEPTH, D_MODEL)),
        'ffn1_w1': nrm(ks[5], (DEPTH, D_MODEL, D_FF), D_MODEL),
        'ffn1_w3': nrm(ks[6], (DEPTH, D_MODEL, D_FF), D_MODEL),
        'ffn1_w2': nrm(ks[7], (DEPTH, D_FF, D_MODEL), D_FF),
        'g_mix': gain(ks[8], (DEPTH, D_MODEL)),
        'w_in': nrm(ks[9], (DEPTH, D_MODEL, IN_WIDTH), D_MODEL),
        'q_norm_g': gain(ks[10], (DEPTH, ATT_HEAD_DIM)),
        'k_norm_g': gain(ks[11], (DEPTH, ATT_HEAD_DIM)),
        'gla_gate_up': nrm(ks[12], (DEPTH, GLA_GATE_RANK, GLA_KEY_DIM), GLA_GATE_RANK),
        'gla_gate_bias': 0.1 * jax.random.normal(ks[13], (DEPTH, GLA_KEY_DIM), f32),
        'gla_out_norm_g': gain(ks[14], (DEPTH, GLA_DV)),
        'w_branch_att': nrm(ks[15], (DEPTH, ATT_OUT_WIDTH, D_MODEL), ATT_OUT_WIDTH),
        'w_branch_gla': nrm(ks[16], (DEPTH, GLA_VAL_DIM, D_MODEL), GLA_VAL_DIM),
        'w_out': nrm(ks[17], (DEPTH, D_MODEL, D_MODEL), D_MODEL),
        'g_ffn2': gain(ks[18], (DEPTH, D_MODEL)),
        'ffn2_w1': nrm(ks[19], (DEPTH, D_MODEL, D_FF), D_MODEL),
        'ffn2_w3': nrm(ks[20], (DEPTH, D_MODEL, D_FF), D_MODEL),
        'ffn2_w2': nrm(ks[21], (DEPTH, D_FF, D_MODEL), D_FF),
    }


def reference(x, c, w_mod, b_mod, g_ffn1, ffn1_w1, ffn1_w3, ffn1_w2, g_mix, w_in, q_norm_g, k_norm_g,
              gla_gate_up, gla_gate_bias, gla_out_norm_g, w_branch_att, w_branch_gla, w_out,
              g_ffn2, ffn2_w1, ffn2_w3, ffn2_w2):
    h = x
    c_act = jax.nn.silu(c)
    for l in range(DEPTH):
        mod = c_act @ w_mod[l] + b_mod[l]
        (sh1, sc1, gt1, sh2, sc2, gt2, sh3, sc3, gt3) = jnp.split(mod, N_MOD, axis=-1)
        u = modulate(rms_norm(h, g_ffn1[l]), sh1, sc1)
        h = h + 0.5 * (1 + gt1)[:, None, :] * swiglu(u, ffn1_w1[l], ffn1_w3[l], ffn1_w2[l])
        u = modulate(rms_norm(h, g_mix[l]), sh2, sc2)
        h = h + (1 + gt2)[:, None, :] * token_mixing(u, w_in[l], q_norm_g[l], k_norm_g[l], gla_gate_up[l],
                                                     gla_gate_bias[l], gla_out_norm_g[l], w_branch_att[l],
                                                     w_branch_gla[l], w_out[l])
        u = modulate(rms_norm(h, g_ffn2[l]), sh3, sc3)
        h = h + 0.5 * (1 + gt3)[:, None, :] * swiglu(u, ffn2_w1[l], ffn2_w3[l], ffn2_w2[l])
    return h
```

```python
import functools

import numpy as np
import jax
import jax.numpy as jnp
from jax import lax
from jax.experimental import pallas as pl
from jax.experimental.pallas import tpu as pltpu

D_MODEL = 1024
BATCH = 16
SEQ = 2048
N_MOD = 9
D_FF = 2816
EPS = 1e-6
ATT_GROUPS = ((128, 1), (512, 4), (2048, 16))
ATT_HEADS_PER_GROUP = 4
ATT_HEADS = 12
ATT_HEAD_DIM = 64
ATT_WIDTH = 768
ATT_OUT_WIDTH = 256
ALIBI_MAX = 8.0
GLA_HEADS = 4
GLA_KEY_DIM = 512
GLA_VAL_DIM = 1024
GLA_DK = 128
GLA_DV = 256
GLA_GATE_RANK = 16
GLA_TAU = 16.0
GLA_CHUNK = 64
ATT_BLK = 128
NEG_BIG = -1e30

BF = jnp.bfloat16
F32 = jnp.float32

VMEM_LIMIT_BYTES = 56 * 1024 * 1024

ROW_TILE = 512
FF_CHUNKS = 2
GLA_BLOCK = 512


def _dot(a, b):
    return jnp.dot(a, b, preferred_element_type=F32)


def _dot_nt(a, b):
    return lax.dot_general(a, b, (((1,), (1,)), ((), ())), preferred_element_type=F32)


def _dot_tn(a, b):
    return lax.dot_general(a, b, (((0,), (0,)), ((), ())), preferred_element_type=F32)


def _split_bf16(x):
    hi = x.astype(BF)
    lo = (x - hi.astype(F32)).astype(BF)
    return hi, lo


def _silu(x):
    return x * jax.nn.sigmoid(x)


def _norm_modulate(x, g, shift, scale):
    ms = jnp.mean(x * x, axis=-1, keepdims=True)
    y = x * lax.rsqrt(ms + EPS) * g
    return y * (1.0 + scale) + shift


def _const_spec(shape):
    nd = len(shape)
    return pl.BlockSpec(shape, lambda *_: (0,) * nd, pipeline_mode=pl.Buffered(1))


def _params(*sem):
    return pltpu.CompilerParams(dimension_semantics=sem, vmem_limit_bytes=VMEM_LIMIT_BYTES)


def _mod_body(c_ref, w_ref, b_ref, o_ref):
    c = c_ref[...]
    o_ref[...] = _dot(_silu(c).astype(BF), w_ref[...].astype(BF)) + b_ref[...]


def _mod_call(c, w_mod, b_mod):
    n = w_mod.shape[1]
    bn = D_MODEL
    return pl.pallas_call(
        _mod_body,
        grid=(n // bn,),
        in_specs=[pl.BlockSpec((BATCH, D_MODEL), lambda j: (0, 0)),
                  pl.BlockSpec((D_MODEL, bn), lambda j: (0, j)),
                  pl.BlockSpec((1, bn), lambda j: (0, j))],
        out_specs=pl.BlockSpec((BATCH, bn), lambda j: (0, j)),
        out_shape=jax.ShapeDtypeStruct((BATCH, n), F32),
        compiler_params=_params("arbitrary"),
        name="mod",
    )(c, w_mod, b_mod.reshape(1, n))


def _ffn_body(x_ref, mod_ref, g_ref, w1_ref, w3_ref, w2_ref, o_ref, *, mod_row):
    x = x_ref[0]
    m = mod_ref[0]
    shift, scale, gate = (m[mod_row + i:mod_row + i + 1] for i in range(3))
    u = _norm_modulate(x, g_ref[...], shift, scale).astype(BF)
    fc = D_FF // FF_CHUNKS
    acc = None
    for c in range(FF_CHUNKS):
        h1 = _dot(u, w1_ref[:, c * fc:(c + 1) * fc])
        h3 = _dot(u, w3_ref[:, c * fc:(c + 1) * fc])
        p = _dot((_silu(h1) * h3).astype(BF), w2_ref[c * fc:(c + 1) * fc, :])
        acc = p if acc is None else acc + p
    o_ref[0] = x + (0.5 * (1.0 + gate)) * acc


def _ffn_call(x, mod, g, w1, w3, w2, mod_row):
    row = pl.BlockSpec((1, ROW_TILE, D_MODEL), lambda b, i: (b, i, 0))
    return pl.pallas_call(
        functools.partial(_ffn_body, mod_row=mod_row),
        grid=(BATCH, SEQ // ROW_TILE),
        in_specs=[row,
                  pl.BlockSpec((1, N_MOD, D_MODEL), lambda b, i: (b, 0, 0)),
                  _const_spec((1, D_MODEL)),
                  _const_spec((D_MODEL, D_FF)),
                  _const_spec((D_MODEL, D_FF)),
                  _const_spec((D_FF, D_MODEL))],
        out_specs=row,
        out_shape=jax.ShapeDtypeStruct((BATCH, SEQ, D_MODEL), F32),
        compiler_params=_params("parallel", "parallel"),
        name="ffn%d" % mod_row,
    )(x, mod, g, w1, w3, w2)


QK_W = 2 * ATT_OUT_WIDTH
GRP_W = 3 * ATT_OUT_WIDTH
GD_PAD = 128


def _proj_body(h_ref, mod_ref, g_ref, watt_ref, wgqk_ref, wgv_ref, wgr_ref, wgd_ref, wgate_ref,
               hsum_ref, hexp_ref, qkg_ref,
               a0_ref, a1_ref, a2_ref, gqk_ref, gv_ref, gr_ref, gd_ref, gate_ref):
    m = mod_ref[0]
    u = _norm_modulate(h_ref[0], g_ref[...], m[3:4], m[4:5]).astype(BF)
    for gi, a_ref in enumerate((a0_ref, a1_ref, a2_ref)):
        y = _dot(u, watt_ref[:, gi * GRP_W:(gi + 1) * GRP_W])
        qk = y[:, :QK_W]
        ss = _dot((qk * qk).astype(BF), hsum_ref[...])
        r_hi, r_lo = _split_bf16(lax.rsqrt(ss * (1.0 / ATT_HEAD_DIM) + EPS))
        rx = _dot(r_hi, hexp_ref[...]) + _dot(r_lo, hexp_ref[...])
        a_ref[0, :, :QK_W] = (qk * rx * qkg_ref[...]).astype(BF)
        a_ref[0, :, QK_W:] = y[:, QK_W:].astype(BF)
    gqk_ref[0] = _dot(u, wgqk_ref[...]).astype(BF)
    gv_ref[0] = _dot(u, wgv_ref[...]).astype(BF)
    gr_ref[0] = _dot(u, wgr_ref[...]).astype(BF)
    gd_ref[0] = _dot(u, wgd_ref[...]).astype(BF)
    gate_ref[0] = _dot(u, wgate_ref[...]).astype(BF)


def _proj_call(h, mod, g, watt, wgqk, wgv, wgr, wgd, wgate, hsum, hexp, qkg):
    def row(w):
        return pl.BlockSpec((1, ROW_TILE, w), lambda b, i: (b, i, 0))

    def out(w):
        return jax.ShapeDtypeStruct((BATCH, SEQ, w), BF)

    widths = (GRP_W, GRP_W, GRP_W, 2 * GLA_KEY_DIM, GLA_VAL_DIM, GLA_VAL_DIM, GD_PAD, 2 * D_MODEL)
    consts = (g, watt, wgqk, wgv, wgr, wgd, wgate, hsum, hexp, qkg)
    return pl.pallas_call(
        _proj_body,
        grid=(BATCH, SEQ // ROW_TILE),
        in_specs=[row(D_MODEL), pl.BlockSpec((1, N_MOD, D_MODEL), lambda b, i: (b, 0, 0))]
                 + [_const_spec(a.shape) for a in consts],
        out_specs=[row(w) for w in widths],
        out_shape=[out(w) for w in widths],
        compiler_params=_params("parallel", "parallel"),
        name="proj",
    )(h, mod, *consts)


def _attn_block(a_ref, bias, o_ref, l_ref, col, q0, k0, nk):
    q = a_ref[0, pl.ds(q0, ATT_BLK), col:col + ATT_OUT_WIDTH]
    k = a_ref[0, pl.ds(k0, nk), col + ATT_OUT_WIDTH:col + 2 * ATT_OUT_WIDTH]
    v = a_ref[0, pl.ds(k0, nk), col + 2 * ATT_OUT_WIDTH:col + 3 * ATT_OUT_WIDTH]
    head_of_lane = lax.broadcasted_iota(jnp.int32, (ATT_BLK, ATT_OUT_WIDTH), 1) // ATT_HEAD_DIM
    zero = jnp.zeros_like(q)
    q4 = jnp.concatenate([jnp.where(head_of_lane == j, q, zero)
                          for j in range(ATT_HEADS_PER_GROUP)], axis=0)
    s = _dot_nt(q4, k) + bias
    mx = jnp.max(s, axis=-1, keepdims=True)
    e = jnp.exp(s - mx)
    den = jnp.sum(e, axis=-1, keepdims=True)
    pv = _dot(e.astype(BF), v) / den
    lse = mx + jnp.log(den)
    lane = lax.broadcasted_iota(jnp.int32, (ATT_BLK, 128), 1)
    o = jnp.zeros((ATT_BLK, ATT_OUT_WIDTH), F32)
    l = jnp.zeros((ATT_BLK, 128), F32)
    for j in range(ATT_HEADS_PER_GROUP):
        rows = slice(j * ATT_BLK, (j + 1) * ATT_BLK)
        o = jnp.where(head_of_lane == j, pv[rows], o)
        l = jnp.where(lane == j, lse[rows], l)
    ocol = (col // GRP_W) * ATT_OUT_WIDTH
    lcol = (col // GRP_W) * 128
    o_ref[0, pl.ds(q0, ATT_BLK), ocol:ocol + ATT_OUT_WIDTH] = o.astype(o_ref.dtype)
    l_ref[0, pl.ds(q0, ATT_BLK), lcol:lcol + 128] = l


def _attn_body(a_ref, bias_ref, o_ref, l_ref, *, subseqs, n_blocks):
    for r in range(subseqs):
        col = r * GRP_W
        _attn_block(a_ref, bias_ref[:, ATT_BLK:], o_ref, l_ref, col, 0, 0, ATT_BLK)
        if n_blocks > 1:
            def step(n, carry, col=col):
                q0 = pl.multiple_of(n * ATT_BLK, ATT_BLK)
                k0 = pl.multiple_of((n - 1) * ATT_BLK, ATT_BLK)
                _attn_block(a_ref, bias_ref[...], o_ref, l_ref, col, q0, k0, 2 * ATT_BLK)
                return carry
            lax.fori_loop(1, n_blocks, step, 0)


def _attn_bias(group):
    _, dilation = ATT_GROUPS[group]
    heads = np.arange(group * ATT_HEADS_PER_GROUP, (group + 1) * ATT_HEADS_PER_GROUP)
    slopes = (2.0 ** (-ALIBI_MAX * (heads + 1).astype(np.float32) / ATT_HEADS)).astype(np.float32)
    qi = np.arange(ATT_BLK)[:, None]
    kj = np.arange(2 * ATT_BLK)[None, :]
    steps = qi + ATT_BLK - kj
    valid = (steps >= 0) & (steps <= ATT_BLK)
    bias = -(slopes * np.float32(dilation))[:, None, None] * steps.astype(np.float32)[None]
    bias = np.where(valid[None], bias, np.float32(NEG_BIG)).astype(np.float32)
    return bias.reshape(ATT_HEADS_PER_GROUP * ATT_BLK, 2 * ATT_BLK)


def _attn_call(a, group):
    _, d = ATT_GROUPS[group]
    L = SEQ // d
    n_blocks = L // ATT_BLK
    subseqs = min(d, 4)
    steps = d // subseqs
    a = a.reshape(BATCH, L, d * GRP_W)
    o, l = pl.pallas_call(
        functools.partial(_attn_body, subseqs=subseqs, n_blocks=n_blocks),
        grid=(BATCH, steps),
        in_specs=[pl.BlockSpec((1, L, subseqs * GRP_W), lambda b, r: (b, 0, r)),
                  _const_spec((ATT_HEADS_PER_GROUP * ATT_BLK, 2 * ATT_BLK))],
        out_specs=[pl.BlockSpec((1, L, subseqs * ATT_OUT_WIDTH), lambda b, r: (b, 0, r)),
                   pl.BlockSpec((1, L, subseqs * 128), lambda b, r: (b, 0, r))],
        out_shape=[jax.ShapeDtypeStruct((BATCH, L, d * ATT_OUT_WIDTH), BF),
                   jax.ShapeDtypeStruct((BATCH, L, d * 128), F32)],
        compiler_params=_params("parallel", "parallel"),
        name="attn%d" % d,
    )(a, jnp.asarray(_attn_bias(group)))
    return o.reshape(BATCH, SEQ, ATT_OUT_WIDTH), l.reshape(BATCH, SEQ, 128)


def _gla_body(gqk_ref, gv_ref, gr_ref, gd_ref, up_ref, gb_ref, tril_ref, ones_ref, gain_ref,
              o_ref, state_ref, oacc_ref):
    @pl.when(pl.program_id(1) == 0)
    def _():
        state_ref[...] = jnp.zeros_like(state_ref)

    C = GLA_CHUNK
    x = _dot(gd_ref[0], up_ref[...]) + gb_ref[...]
    log_a = (jnp.minimum(x, 0.0) - jnp.log1p(jnp.exp(-jnp.abs(x)))) * (1.0 / GLA_TAU)
    la_hi, la_lo = _split_bf16(log_a)
    b = _dot(tril_ref[...], la_hi) + _dot(tril_ref[...], la_lo)
    b_last = _dot(ones_ref[...], la_hi) + _dot(ones_ref[...], la_lo)
    q = gqk_ref[0, :, :GLA_KEY_DIM].astype(F32) * (GLA_DK ** -0.5)
    k = gqk_ref[0, :, GLA_KEY_DIM:].astype(F32)
    q_dec = (q * jnp.exp(b)).astype(BF)
    k_dec = (k * jnp.exp(-b)).astype(BF)
    k_rem = (k * jnp.exp(b_last - b)).astype(BF)
    causal = (lax.broadcasted_iota(jnp.int32, (C, C), 0)
              >= lax.broadcasted_iota(jnp.int32, (C, C), 1))
    ones_cv = jnp.ones((C, GLA_DV), BF)
    for c in range(GLA_BLOCK // C):
        rows = slice(c * C, (c + 1) * C)
        for h in range(GLA_HEADS):
            kc = slice(h * GLA_DK, (h + 1) * GLA_DK)
            vc = slice(h * GLA_DV, (h + 1) * GLA_DV)
            qd, kd, kr = q_dec[rows, kc], k_dec[rows, kc], k_rem[rows, kc]
            v = gv_ref[0, rows, vc]
            st = state_ref[h]
            attn = jnp.where(causal, _dot_nt(qd, kd), 0.0).astype(BF)
            oacc_ref[rows, vc] = _dot(attn, v) + _dot(qd, st.astype(BF))
            decay = jnp.exp(_dot_tn(la_hi[rows, kc], ones_cv) + _dot_tn(la_lo[rows, kc], ones_cv))
            state_ref[h] = decay * st + _dot_tn(kr, v)
    for h in range(GLA_HEADS):
        vc = slice(h * GLA_DV, (h + 1) * GLA_DV)
        o = oacc_ref[:, vc]
        ms = jnp.mean(o * o, axis=-1, keepdims=True)
        y = o * lax.rsqrt(ms + EPS) * gain_ref[:, vc]
        o_ref[0, :, vc] = (y * _silu(gr_ref[0, :, vc].astype(F32))).astype(o_ref.dtype)


def _block_diag(n, c, lower):
    i = np.arange(n)
    same = (i[:, None] // c) == (i[None, :] // c)
    if lower:
        same &= i[:, None] >= i[None, :]
    return same.astype(np.float32)


def _gla_call(gqk, gv, gr, gd, up, gbias, gain):
    def row(w):
        return pl.BlockSpec((1, GLA_BLOCK, w), lambda b, i: (b, i, 0))

    tril = jnp.asarray(_block_diag(GLA_BLOCK, GLA_CHUNK, True), BF)
    ones = jnp.asarray(_block_diag(GLA_BLOCK, GLA_CHUNK, False), BF)
    consts = (up, gbias, tril, ones, gain)
    return pl.pallas_call(
        _gla_body,
        grid=(BATCH, SEQ // GLA_BLOCK),
        in_specs=[row(2 * GLA_KEY_DIM), row(GLA_VAL_DIM), row(GLA_VAL_DIM), row(GD_PAD)]
                 + [_const_spec(a.shape) for a in consts],
        out_specs=row(GLA_VAL_DIM),
        out_shape=jax.ShapeDtypeStruct((BATCH, SEQ, GLA_VAL_DIM), BF),
        scratch_shapes=[pltpu.VMEM((GLA_HEADS, GLA_DK, GLA_DV), F32),
                        pltpu.VMEM((GLA_BLOCK, GLA_VAL_DIM), F32)],
        compiler_params=_params("parallel", "arbitrary"),
        name="gla",
    )(gqk, gv, gr, gd, *consts)


def _merge_body(h_ref, mod_ref, o0_ref, o1_ref, o2_ref, l0_ref, l1_ref, l2_ref,
                ogla_ref, gate_ref, hexp_ref, wba_ref, wbg_ref, wout_ref, out_ref):
    lses = [r[0] for r in (l0_ref, l1_ref, l2_ref)]
    mx = jnp.maximum(jnp.maximum(lses[0], lses[1]), lses[2])
    es = [jnp.exp(l - mx) for l in lses]
    den = es[0] + es[1] + es[2]
    o_att = None
    for e, o_ref in zip(es, (o0_ref, o1_ref, o2_ref)):
        w_hi, w_lo = _split_bf16(e / den)
        w = _dot(w_hi, hexp_ref[...]) + _dot(w_lo, hexp_ref[...])
        t = w * o_ref[0].astype(F32)
        o_att = t if o_att is None else o_att + t
    att = _dot(o_att.astype(BF), wba_ref[...])
    gla = _dot(ogla_ref[0], wbg_ref[...])
    g_att = jax.nn.sigmoid(gate_ref[0, :, :D_MODEL].astype(F32))
    g_gla = jax.nn.sigmoid(gate_ref[0, :, D_MODEL:].astype(F32))
    merged = (g_att * att + g_gla * gla).astype(BF)
    gate = mod_ref[0][5:6]
    out_ref[0] = h_ref[0] + (1.0 + gate) * _dot(merged, wout_ref[...])


def _merge_call(h, mod, os_, ls_, ogla, gates, hexp, wba, wbg, wout):
    def row(w):
        return pl.BlockSpec((1, ROW_TILE, w), lambda b, i: (b, i, 0))

    consts = (hexp, wba, wbg, wout)
    return pl.pallas_call(
        _merge_body,
        grid=(BATCH, SEQ // ROW_TILE),
        in_specs=[row(D_MODEL), pl.BlockSpec((1, N_MOD, D_MODEL), lambda b, i: (b, 0, 0))]
                 + [row(ATT_OUT_WIDTH)] * 3 + [row(128)] * 3
                 + [row(GLA_VAL_DIM), row(2 * D_MODEL)]
                 + [_const_spec(a.shape) for a in consts],
        out_specs=row(D_MODEL),
        out_shape=jax.ShapeDtypeStruct((BATCH, SEQ, D_MODEL), F32),
        compiler_params=_params("parallel", "parallel"),
        name="merge",
    )(h, mod, *os_, *ls_, ogla, gates, *consts)


def _head_matrix(n_lanes_in, width):
    j = np.arange(n_lanes_in)[:, None]
    c = np.arange(width)[None, :]
    return (c // ATT_HEAD_DIM == j).astype(np.float32)


def _layer(h, mod, g_ffn1, f1w1, f1w3, f1w2, g_mix, w_in, q_norm_g, k_norm_g, gate_up, gate_bias,
           out_norm_g, w_branch_att, w_branch_gla, w_out, g_ffn2, f2w1, f2w3, f2w2):
    bf = lambda w: w.astype(BF)
    row = lambda v: v.reshape(1, -1)

    h = _ffn_call(h, mod, row(g_ffn1), bf(f1w1), bf(f1w3), bf(f1w2), 0)

    splits = np.cumsum((ATT_WIDTH, ATT_WIDTH, ATT_WIDTH, GLA_KEY_DIM, GLA_KEY_DIM, GLA_VAL_DIM,
                        GLA_VAL_DIM, GLA_GATE_RANK))
    aq, ak, av, gq, gk, gv, gr, gdn, gates = jnp.split(bf(w_in), [int(s) for s in splits], axis=1)
    W = ATT_OUT_WIDTH
    watt = jnp.concatenate([t[:, gi * W:(gi + 1) * W] for gi in range(len(ATT_GROUPS))
                            for t in (aq, ak, av)], axis=1)
    wgqk = jnp.concatenate([gq, gk], axis=1)
    wgd = jnp.pad(gdn, ((0, 0), (0, GD_PAD - GLA_GATE_RANK)))
    hexp_qk = jnp.asarray(_head_matrix(128, QK_W), BF)
    hsum_qk = hexp_qk.T
    qkg = jnp.concatenate([jnp.tile(q_norm_g * (ATT_HEAD_DIM ** -0.5), ATT_HEADS_PER_GROUP),
                           jnp.tile(k_norm_g, ATT_HEADS_PER_GROUP)])
    a0, a1, a2, pgqk, pgv, pgr, pgd, pgate = _proj_call(
        h, mod, row(g_mix), watt, wgqk, gv, gr, wgd, gates, hsum_qk, hexp_qk, row(qkg))

    outs = [_attn_call(a, gi) for gi, a in enumerate((a0, a1, a2))]
    up = jnp.pad(bf(gate_up), ((0, GD_PAD - GLA_GATE_RANK), (0, 0)))
    ogla = _gla_call(pgqk, pgv, pgr, pgd, up, row(gate_bias), row(jnp.tile(out_norm_g, GLA_HEADS)))

    hexp_o = jnp.asarray(_head_matrix(128, ATT_OUT_WIDTH), BF)
    h = _merge_call(h, mod, [o for o, _ in outs], [l for _, l in outs], ogla, pgate,
                    hexp_o, bf(w_branch_att), bf(w_branch_gla), bf(w_out))

    return _ffn_call(h, mod, row(g_ffn2), bf(f2w1), bf(f2w3), bf(f2w2), 6)


def kernel(x, c, w_mod, b_mod, g_ffn1, ffn1_w1, ffn1_w3, ffn1_w2, g_mix, w_in, q_norm_g, k_norm_g,
           gla_gate_up, gla_gate_bias, gla_out_norm_g, w_branch_att, w_branch_gla, w_out,
           g_ffn2, ffn2_w1, ffn2_w3, ffn2_w2):
    h = x
    for l in range(w_mod.shape[0]):
        mod = _mod_call(c, w_mod[l], b_mod[l]).reshape(BATCH, N_MOD, D_MODEL)
        h = _layer(h, mod, g_ffn1[l], ffn1_w1[l], ffn1_w3[l], ffn1_w2[l], g_mix[l], w_in[l],
                   q_norm_g[l], k_norm_g[l], gla_gate_up[l], gla_gate_bias[l], gla_out_norm_g[l],
                   w_branch_att[l], w_branch_gla[l], w_out[l], g_ffn2[l], ffn2_w1[l], ffn2_w3[l],
                   ffn2_w2[l])
    return h
```

```python
import functools

import numpy as np
import jax
import jax.numpy as jnp
from jax import lax
from jax.experimental import pallas as pl
from jax.experimental.pallas import tpu as pltpu

D_MODEL = 1024
BATCH = 16
SEQ = 2048
N_MOD = 9
D_FF = 2816
EPS = 1e-6
ATT_GROUPS = ((128, 1), (512, 4), (2048, 16))
ATT_HEADS_PER_GROUP = 4
ATT_HEADS = 12
ATT_HEAD_DIM = 64
ATT_WIDTH = 768
ATT_OUT_WIDTH = 256
ALIBI_MAX = 8.0
GLA_HEADS = 4
GLA_KEY_DIM = 512
GLA_VAL_DIM = 1024
GLA_DK = 128
GLA_DV = 256
GLA_GATE_RANK = 16
GLA_TAU = 16.0
GLA_CHUNK = 64
ATT_BLK = 128
NEG_BIG = -1e30

BF = jnp.bfloat16
F32 = jnp.float32

VMEM_LIMIT_BYTES = 56 * 1024 * 1024

ROW_TILE = 512
FFN_TILE = 1024
FF_CHUNKS = 4
GLA_BLOCK = 512


def _dot(a, b):
    return jnp.dot(a, b, preferred_element_type=F32)


def _dot_nt(a, b):
    return lax.dot_general(a, b, (((1,), (1,)), ((), ())), preferred_element_type=F32)


def _dot_tn(a, b):
    return lax.dot_general(a, b, (((0,), (0,)), ((), ())), preferred_element_type=F32)


def _split_bf16(x):
    hi = x.astype(BF)
    lo = (x - hi.astype(F32)).astype(BF)
    return hi, lo


def _silu(x):
    return x * jax.nn.sigmoid(x)


def _norm_modulate(x, g, shift, scale):
    ms = jnp.mean(x * x, axis=-1, keepdims=True)
    y = x * lax.rsqrt(ms + EPS) * g
    return y * (1.0 + scale) + shift


def _const_spec(shape):
    nd = len(shape)
    return pl.BlockSpec(shape, lambda *_: (0,) * nd, pipeline_mode=pl.Buffered(1))


def _dilated_spec(d, width):
    return pl.BlockSpec((1, d, ROW_TILE // d, width), lambda b, i: (b, 0, i, 0))


def _params(*sem):
    return pltpu.CompilerParams(dimension_semantics=sem, vmem_limit_bytes=VMEM_LIMIT_BYTES)


def _mod_body(c_ref, w_ref, b_ref, o_ref):
    c = c_ref[...]
    o_ref[...] = _dot(_silu(c).astype(BF), w_ref[...].astype(BF)) + b_ref[...]


def _mod_call(c, w_mod, b_mod):
    n = w_mod.shape[1]
    bn = D_MODEL
    return pl.pallas_call(
        _mod_body,
        grid=(n // bn,),
        in_specs=[pl.BlockSpec((BATCH, D_MODEL), lambda j: (0, 0)),
                  pl.BlockSpec((D_MODEL, bn), lambda j: (0, j)),
                  pl.BlockSpec((1, bn), lambda j: (0, j))],
        out_specs=pl.BlockSpec((BATCH, bn), lambda j: (0, j)),
        out_shape=jax.ShapeDtypeStruct((BATCH, n), F32),
        compiler_params=_params("arbitrary"),
        name="mod",
    )(c, w_mod, b_mod.reshape(1, n))


def _ffn_body(x_ref, mod_ref, g_ref, w1_ref, w3_ref, w2_ref, o_ref, *, mod_row):
    x = x_ref[0]
    m = mod_ref[0]
    shift, scale, gate = (m[mod_row + i:mod_row + i + 1] for i in range(3))
    u = _norm_modulate(x, g_ref[...], shift, scale).astype(BF)
    fc = D_FF // FF_CHUNKS
    acc = None
    for c in range(FF_CHUNKS):
        h1 = _dot(u, w1_ref[:, c * fc:(c + 1) * fc])
        h3 = _dot(u, w3_ref[:, c * fc:(c + 1) * fc])
        p = _dot((_silu(h1) * h3).astype(BF), w2_ref[c * fc:(c + 1) * fc, :])
        acc = p if acc is None else acc + p
    o_ref[0] = x + (0.5 * (1.0 + gate)) * acc


def _ffn_call(x, mod, g, w1, w3, w2, mod_row):
    row = pl.BlockSpec((1, FFN_TILE, D_MODEL), lambda b, i: (b, i, 0))
    return pl.pallas_call(
        functools.partial(_ffn_body, mod_row=mod_row),
        grid=(BATCH, SEQ // FFN_TILE),
        in_specs=[row,
                  pl.BlockSpec((1, N_MOD, D_MODEL), lambda b, i: (b, 0, 0)),
                  _const_spec((1, D_MODEL)),
                  _const_spec((D_MODEL, D_FF)),
                  _const_spec((D_MODEL, D_FF)),
                  _const_spec((D_FF, D_MODEL))],
        out_specs=row,
        out_shape=jax.ShapeDtypeStruct((BATCH, SEQ, D_MODEL), F32),
        compiler_params=_params("parallel", "parallel"),
        name="ffn%d" % mod_row,
    )(x, mod, g, w1, w3, w2)


QK_W = 2 * ATT_OUT_WIDTH
GRP_W = 3 * ATT_OUT_WIDTH
GD_PAD = 128


def _proj_body(h_ref, mod_ref, g_ref, watt_ref, wgqk_ref, wgv_ref, wgr_ref, wgd_ref, wgate_ref,
               hsum_ref, hexp_ref, qkg_ref,
               a0_ref, a1_ref, a2_ref, gqk_ref, gv_ref, gr_ref, gd_ref, gate_ref, perm_ref):
    m = mod_ref[0]
    u = _norm_modulate(h_ref[0], g_ref[...], m[3:4], m[4:5]).astype(BF)
    for gi, a_ref in enumerate((a0_ref, a1_ref, a2_ref)):
        d = ATT_GROUPS[gi][1]
        y = _dot(u, watt_ref[:, gi * GRP_W:(gi + 1) * GRP_W])
        qk = y[:, :QK_W]
        ss = _dot((qk * qk).astype(BF), hsum_ref[...])
        r_hi, r_lo = _split_bf16(lax.rsqrt(ss * (1.0 / ATT_HEAD_DIM) + EPS))
        rx = _dot(r_hi, hexp_ref[...]) + _dot(r_lo, hexp_ref[...])
        qkn = qk * rx * qkg_ref[...]
        if d == 1:
            a_ref[0, 0, :, :QK_W] = qkn.astype(BF)
            a_ref[0, 0, :, QK_W:] = y[:, QK_W:].astype(BF)
        else:
            for j in range(GRP_W // 128):
                cols = slice(j * 128, (j + 1) * 128)
                perm_ref[j] = qkn[:, cols] if j < QK_W // 128 else y[:, cols]
            for r in range(d):
                for j in range(GRP_W // 128):
                    a_ref[0, r, :, j * 128:(j + 1) * 128] = (
                        perm_ref[j, pl.ds(r, ROW_TILE // d, stride=d), :].astype(BF))
    gqk_ref[0] = _dot(u, wgqk_ref[...]).astype(BF)
    gv_ref[0] = _dot(u, wgv_ref[...]).astype(BF)
    gr_ref[0] = _dot(u, wgr_ref[...]).astype(BF)
    gd_ref[0] = _dot(u, wgd_ref[...]).astype(BF)
    gate_ref[0] = _dot(u, wgate_ref[...]).astype(BF)


def _proj_call(h, mod, g, watt, wgqk, wgv, wgr, wgd, wgate, hsum, hexp, qkg):
    def row(w):
        return pl.BlockSpec((1, ROW_TILE, w), lambda b, i: (b, i, 0))

    def out(w):
        return jax.ShapeDtypeStruct((BATCH, SEQ, w), BF)

    widths = (2 * GLA_KEY_DIM, GLA_VAL_DIM, GLA_VAL_DIM, GD_PAD, 2 * D_MODEL)
    consts = (g, watt, wgqk, wgv, wgr, wgd, wgate, hsum, hexp, qkg)
    dils = [d for _, d in ATT_GROUPS]
    return pl.pallas_call(
        _proj_body,
        grid=(BATCH, SEQ // ROW_TILE),
        in_specs=[row(D_MODEL), pl.BlockSpec((1, N_MOD, D_MODEL), lambda b, i: (b, 0, 0))]
                 + [_const_spec(a.shape) for a in consts],
        out_specs=[_dilated_spec(d, GRP_W) for d in dils] + [row(w) for w in widths],
        out_shape=[jax.ShapeDtypeStruct((BATCH, d, SEQ // d, GRP_W), BF) for d in dils]
                  + [out(w) for w in widths],
        scratch_shapes=[pltpu.VMEM((GRP_W // 128, ROW_TILE, 128), F32)],
        compiler_params=_params("parallel", "parallel"),
        name="proj",
    )(h, mod, *consts)


def _attn_block(a_ref, bias, o_ref, l_ref, r, q0, k0, nk):
    W = ATT_OUT_WIDTH
    q = a_ref[0, r, pl.ds(q0, ATT_BLK), :W]
    k = a_ref[0, r, pl.ds(k0, nk), W:2 * W]
    v = a_ref[0, r, pl.ds(k0, nk), 2 * W:]
    head_of_lane = lax.broadcasted_iota(jnp.int32, (ATT_BLK, ATT_OUT_WIDTH), 1) // ATT_HEAD_DIM
    zero = jnp.zeros_like(q)
    q4 = jnp.concatenate([jnp.where(head_of_lane == j, q, zero)
                          for j in range(ATT_HEADS_PER_GROUP)], axis=0)
    s = _dot_nt(q4, k) + bias
    mx = jnp.max(s, axis=-1, keepdims=True)
    e = jnp.exp(s - mx)
    den = jnp.sum(e, axis=-1, keepdims=True)
    pv = _dot(e.astype(BF), v) / den
    lse = mx + jnp.log(den)
    lane = lax.broadcasted_iota(jnp.int32, (ATT_BLK, 128), 1)
    o = jnp.zeros((ATT_BLK, ATT_OUT_WIDTH), F32)
    l = jnp.zeros((ATT_BLK, 128), F32)
    for j in range(ATT_HEADS_PER_GROUP):
        rows = slice(j * ATT_BLK, (j + 1) * ATT_BLK)
        o = jnp.where(head_of_lane == j, pv[rows], o)
        l = jnp.where(lane == j, lse[rows], l)
    o_ref[0, r, pl.ds(q0, ATT_BLK), :] = o.astype(o_ref.dtype)
    l_ref[0, r, pl.ds(q0, ATT_BLK), :] = l


def _attn_body(a_ref, bias_ref, o_ref, l_ref, *, subseqs, n_blocks):
    for r in range(subseqs):
        _attn_block(a_ref, bias_ref[:, ATT_BLK:], o_ref, l_ref, r, 0, 0, ATT_BLK)
        if n_blocks > 1:
            def step(n, carry, r=r):
                q0 = pl.multiple_of(n * ATT_BLK, ATT_BLK)
                k0 = pl.multiple_of((n - 1) * ATT_BLK, ATT_BLK)
                _attn_block(a_ref, bias_ref[...], o_ref, l_ref, r, q0, k0, 2 * ATT_BLK)
                return carry
            lax.fori_loop(1, n_blocks, step, 0)


def _attn_bias(group):
    _, dilation = ATT_GROUPS[group]
    heads = np.arange(group * ATT_HEADS_PER_GROUP, (group + 1) * ATT_HEADS_PER_GROUP)
    slopes = (2.0 ** (-ALIBI_MAX * (heads + 1).astype(np.float32) / ATT_HEADS)).astype(np.float32)
    qi = np.arange(ATT_BLK)[:, None]
    kj = np.arange(2 * ATT_BLK)[None, :]
    steps = qi + ATT_BLK - kj
    valid = (steps >= 0) & (steps <= ATT_BLK)
    bias = -(slopes * np.float32(dilation))[:, None, None] * steps.astype(np.float32)[None]
    bias = np.where(valid[None], bias, np.float32(NEG_BIG)).astype(np.float32)
    return bias.reshape(ATT_HEADS_PER_GROUP * ATT_BLK, 2 * ATT_BLK)


def _attn_call(a, group):
    _, d = ATT_GROUPS[group]
    L = SEQ // d
    n_blocks = L // ATT_BLK
    subseqs = min(d, 4)

    def spec(w):
        return pl.BlockSpec((1, subseqs, L, w), lambda b, r: (b, r, 0, 0))

    return pl.pallas_call(
        functools.partial(_attn_body, subseqs=subseqs, n_blocks=n_blocks),
        grid=(BATCH, d // subseqs),
        in_specs=[spec(GRP_W), _const_spec((ATT_HEADS_PER_GROUP * ATT_BLK, 2 * ATT_BLK))],
        out_specs=[spec(ATT_OUT_WIDTH), spec(128)],
        out_shape=[jax.ShapeDtypeStruct((BATCH, d, L, ATT_OUT_WIDTH), BF),
                   jax.ShapeDtypeStruct((BATCH, d, L, 128), F32)],
        compiler_params=_params("parallel", "parallel"),
        name="attn%d" % d,
    )(a, jnp.asarray(_attn_bias(group)))


def _gla_body(gqk_ref, gv_ref, gr_ref, gd_ref, up_ref, gb_ref, tril_ref, ones_ref, gain_ref,
              o_ref, state_ref, oacc_ref):
    @pl.when(pl.program_id(1) == 0)
    def _():
        state_ref[...] = jnp.zeros_like(state_ref)

    C = GLA_CHUNK
    x = _dot(gd_ref[0], up_ref[...]) + gb_ref[...]
    log_a = (jnp.minimum(x, 0.0) - jnp.log1p(jnp.exp(-jnp.abs(x)))) * (1.0 / GLA_TAU)
    la_hi, la_lo = _split_bf16(log_a)
    b = _dot(tril_ref[...], la_hi) + _dot(tril_ref[...], la_lo)
    b_last = _dot(ones_ref[...], la_hi) + _dot(ones_ref[...], la_lo)
    q = gqk_ref[0, :, :GLA_KEY_DIM].astype(F32) * (GLA_DK ** -0.5)
    k = gqk_ref[0, :, GLA_KEY_DIM:].astype(F32)
    q_dec = (q * jnp.exp(b)).astype(BF)
    k_dec = (k * jnp.exp(-b)).astype(BF)
    k_rem = (k * jnp.exp(b_last - b)).astype(BF)
    causal = (lax.broadcasted_iota(jnp.int32, (C, C), 0)
              >= lax.broadcasted_iota(jnp.int32, (C, C), 1))
    ones_cv = jnp.ones((C, GLA_DV), BF)
    for c in range(GLA_BLOCK // C):
        rows = slice(c * C, (c + 1) * C)
        for h in range(GLA_HEADS):
            kc = slice(h * GLA_DK, (h + 1) * GLA_DK)
            vc = slice(h * GLA_DV, (h + 1) * GLA_DV)
            qd, kd, kr = q_dec[rows, kc], k_dec[rows, kc], k_rem[rows, kc]
            v = gv_ref[0, rows, vc]
            st = state_ref[h]
            attn = jnp.where(causal, _dot_nt(qd, kd), 0.0).astype(BF)
            oacc_ref[rows, vc] = _dot(attn, v) + _dot(qd, st.astype(BF))
            decay = jnp.exp(_dot_tn(la_hi[rows, kc], ones_cv) + _dot_tn(la_lo[rows, kc], ones_cv))
            state_ref[h] = decay * st + _dot_tn(kr, v)
    for h in range(GLA_HEADS):
        vc = slice(h * GLA_DV, (h + 1) * GLA_DV)
        o = oacc_ref[:, vc]
        ms = jnp.mean(o * o, axis=-1, keepdims=True)
        y = o * lax.rsqrt(ms + EPS) * gain_ref[:, vc]
        o_ref[0, :, vc] = (y * _silu(gr_ref[0, :, vc].astype(F32))).astype(o_ref.dtype)


def _block_diag(n, c, lower):
    i = np.arange(n)
    same = (i[:, None] // c) == (i[None, :] // c)
    if lower:
        same &= i[:, None] >= i[None, :]
    return same.astype(np.float32)


def _gla_call(gqk, gv, gr, gd, up, gbias, gain):
    def row(w):
        return pl.BlockSpec((1, GLA_BLOCK, w), lambda b, i: (b, i, 0))

    tril = jnp.asarray(_block_diag(GLA_BLOCK, GLA_CHUNK, True), BF)
    ones = jnp.asarray(_block_diag(GLA_BLOCK, GLA_CHUNK, False), BF)
    consts = (up, gbias, tril, ones, gain)
    return pl.pallas_call(
        _gla_body,
        grid=(BATCH, SEQ // GLA_BLOCK),
        in_specs=[row(2 * GLA_KEY_DIM), row(GLA_VAL_DIM), row(GLA_VAL_DIM), row(GD_PAD)]
                 + [_const_spec(a.shape) for a in consts],
        out_specs=row(GLA_VAL_DIM),
        out_shape=jax.ShapeDtypeStruct((BATCH, SEQ, GLA_VAL_DIM), BF),
        scratch_shapes=[pltpu.VMEM((GLA_HEADS, GLA_DK, GLA_DV), F32),
                        pltpu.VMEM((GLA_BLOCK, GLA_VAL_DIM), F32)],
        compiler_params=_params("parallel", "arbitrary"),
        name="gla",
    )(gqk, gv, gr, gd, *consts)


def _undilate(src_ref, scr_ref, d):
    if d == 1:
        return src_ref[0, 0].astype(F32)
    planes = scr_ref.shape[0]
    for r in range(d):
        x = src_ref[0, r].astype(F32)
        for j in range(planes):
            scr_ref[j, pl.ds(r, ROW_TILE // d, stride=d), :] = x[:, j * 128:(j + 1) * 128]
    return jnp.concatenate([scr_ref[j] for j in range(planes)], axis=-1)


def _merge_body(h_ref, mod_ref, o0_ref, o1_ref, o2_ref, l0_ref, l1_ref, l2_ref,
                ogla_ref, gate_ref, hexp_ref, wba_ref, wbg_ref, wout_ref, out_ref,
                os1_ref, os2_ref, ls1_ref, ls2_ref):
    dils = [d for _, d in ATT_GROUPS]
    lses = [_undilate(r, s, d) for r, s, d in
            zip((l0_ref, l1_ref, l2_ref), (None, ls1_ref, ls2_ref), dils)]
    outs = [_undilate(r, s, d) for r, s, d in
            zip((o0_ref, o1_ref, o2_ref), (None, os1_ref, os2_ref), dils)]
    mx = jnp.maximum(jnp.maximum(lses[0], lses[1]), lses[2])
    es = [jnp.exp(l - mx) for l in lses]
    den = es[0] + es[1] + es[2]
    o_att = None
    for e, o in zip(es, outs):
        w_hi, w_lo = _split_bf16(e / den)
        t = (_dot(w_hi, hexp_ref[...]) + _dot(w_lo, hexp_ref[...])) * o
        o_att = t if o_att is None else o_att + t
    att = _dot(o_att.astype(BF), wba_ref[...])
    gla = _dot(ogla_ref[0], wbg_ref[...])
    g_att = jax.nn.sigmoid(gate_ref[0, :, :D_MODEL].astype(F32))
    g_gla = jax.nn.sigmoid(gate_ref[0, :, D_MODEL:].astype(F32))
    merged = (g_att * att + g_gla * gla).astype(BF)
    gate = mod_ref[0][5:6]
    out_ref[0] = h_ref[0] + (1.0 + gate) * _dot(merged, wout_ref[...])


def _merge_call(h, mod, os_, ls_, ogla, gates, hexp, wba, wbg, wout):
    def row(w):
        return pl.BlockSpec((1, ROW_TILE, w), lambda b, i: (b, i, 0))

    consts = (hexp, wba, wbg, wout)
    dils = [d for _, d in ATT_GROUPS]
    return pl.pallas_call(
        _merge_body,
        grid=(BATCH, SEQ // ROW_TILE),
        in_specs=[row(D_MODEL), pl.BlockSpec((1, N_MOD, D_MODEL), lambda b, i: (b, 0, 0))]
                 + [_dilated_spec(d, ATT_OUT_WIDTH) for d in dils]
                 + [_dilated_spec(d, 128) for d in dils]
                 + [row(GLA_VAL_DIM), row(2 * D_MODEL)]
                 + [_const_spec(a.shape) for a in consts],
        out_specs=row(D_MODEL),
        out_shape=jax.ShapeDtypeStruct((BATCH, SEQ, D_MODEL), F32),
        scratch_shapes=[pltpu.VMEM((ATT_OUT_WIDTH // 128, ROW_TILE, 128), F32)] * 2
                       + [pltpu.VMEM((1, ROW_TILE, 128), F32)] * 2,
        compiler_params=_params("parallel", "parallel"),
        name="merge",
    )(h, mod, *os_, *ls_, ogla, gates, *consts)


def _head_matrix(n_lanes_in, width):
    j = np.arange(n_lanes_in)[:, None]
    c = np.arange(width)[None, :]
    return (c // ATT_HEAD_DIM == j).astype(np.float32)


def _layer(h, mod, g_ffn1, f1w1, f1w3, f1w2, g_mix, w_in, q_norm_g, k_norm_g, gate_up, gate_bias,
           out_norm_g, w_branch_att, w_branch_gla, w_out, g_ffn2, f2w1, f2w3, f2w2):
    bf = lambda w: w.astype(BF)
    row = lambda v: v.reshape(1, -1)

    h = _ffn_call(h, mod, row(g_ffn1), bf(f1w1), bf(f1w3), bf(f1w2), 0)

    splits = np.cumsum((ATT_WIDTH, ATT_WIDTH, ATT_WIDTH, GLA_KEY_DIM, GLA_KEY_DIM, GLA_VAL_DIM,
                        GLA_VAL_DIM, GLA_GATE_RANK))
    aq, ak, av, gq, gk, gv, gr, gdn, gates = jnp.split(bf(w_in), [int(s) for s in splits], axis=1)
    W = ATT_OUT_WIDTH
    watt = jnp.concatenate([t[:, gi * W:(gi + 1) * W] for gi in range(len(ATT_GROUPS))
                            for t in (aq, ak, av)], axis=1)
    wgqk = jnp.concatenate([gq, gk], axis=1)
    wgd = jnp.pad(gdn, ((0, 0), (0, GD_PAD - GLA_GATE_RANK)))
    hexp_qk = jnp.asarray(_head_matrix(128, QK_W), BF)
    hsum_qk = hexp_qk.T
    qkg = jnp.concatenate([jnp.tile(q_norm_g * (ATT_HEAD_DIM ** -0.5), ATT_HEADS_PER_GROUP),
                           jnp.tile(k_norm_g, ATT_HEADS_PER_GROUP)])
    a0, a1, a2, pgqk, pgv, pgr, pgd, pgate = _proj_call(
        h, mod, row(g_mix), watt, wgqk, gv, gr, wgd, gates, hsum_qk, hexp_qk, row(qkg))

    outs = [_attn_call(a, gi) for gi, a in enumerate((a0, a1, a2))]
    up = jnp.pad(bf(gate_up), ((0, GD_PAD - GLA_GATE_RANK), (0, 0)))
    ogla = _gla_call(pgqk, pgv, pgr, pgd, up, row(gate_bias), row(jnp.tile(out_norm_g, GLA_HEADS)))

    hexp_o = jnp.asarray(_head_matrix(128, ATT_OUT_WIDTH), BF)
    h = _merge_call(h, mod, [o for o, _ in outs], [l for _, l in outs], ogla, pgate,
                    hexp_o, bf(w_branch_att), bf(w_branch_gla), bf(w_out))

    return _ffn_call(h, mod, row(g_ffn2), bf(f2w1), bf(f2w3), bf(f2w2), 6)


def kernel(x, c, w_mod, b_mod, g_ffn1, ffn1_w1, ffn1_w3, ffn1_w2, g_mix, w_in, q_norm_g, k_norm_g,
           gla_gate_up, gla_gate_bias, gla_out_norm_g, w_branch_att, w_branch_gla, w_out,
           g_ffn2, ffn2_w1, ffn2_w3, ffn2_w2):
    h = x
    for l in range(w_mod.shape[0]):
        mod = _mod_call(c, w_mod[l], b_mod[l]).reshape(BATCH, N_MOD, D_MODEL)
        h = _layer(h, mod, g_ffn1[l], ffn1_w1[l], ffn1_w3[l], ffn1_w2[l], g_mix[l], w_in[l],
                   q_norm_g[l], k_norm_g[l], gla_gate_up[l], gla_gate_bias[l], gla_out_norm_g[l],
                   w_branch_att[l], w_branch_gla[l], w_out[l], g_ffn2[l], ffn2_w1[l], ffn2_w3[l],
                   ffn2_w2[l])
    return h
```

```python
import functools

import numpy as np
import jax
import jax.numpy as jnp
from jax import lax
from jax.experimental import pallas as pl
from jax.experimental.pallas import tpu as pltpu

D_MODEL = 1024
BATCH = 16
SEQ = 2048
N_MOD = 9
D_FF = 2816
EPS = 1e-6
ATT_GROUPS = ((128, 1), (512, 4), (2048, 16))
ATT_HEADS_PER_GROUP = 4
ATT_HEADS = 12
ATT_HEAD_DIM = 64
ATT_WIDTH = 768
ATT_OUT_WIDTH = 256
ALIBI_MAX = 8.0
GLA_HEADS = 4
GLA_KEY_DIM = 512
GLA_VAL_DIM = 1024
GLA_DK = 128
GLA_DV = 256
GLA_GATE_RANK = 16
GLA_TAU = 16.0
GLA_CHUNK = 64
ATT_BLK = 128
NEG_BIG = -1e30

BF = jnp.bfloat16
F32 = jnp.float32

VMEM_LIMIT_BYTES = 56 * 1024 * 1024

ROW_TILE = 512
FFN_TILE = 1024
FF_CHUNKS = 4
GLA_BLOCK = 512


def _dot(a, b):
    return jnp.dot(a, b, preferred_element_type=F32)


def _dot_nt(a, b):
    return lax.dot_general(a, b, (((1,), (1,)), ((), ())), preferred_element_type=F32)


def _dot_tn(a, b):
    return lax.dot_general(a, b, (((0,), (0,)), ((), ())), preferred_element_type=F32)


def _split_bf16(x):
    hi = x.astype(BF)
    lo = (x - hi.astype(F32)).astype(BF)
    return hi, lo


def _silu(x):
    return x * jax.nn.sigmoid(x)


def _norm_modulate(x, g, shift, scale):
    ms = jnp.mean(x * x, axis=-1, keepdims=True)
    y = x * lax.rsqrt(ms + EPS) * g
    return y * (1.0 + scale) + shift


def _const_spec(shape):
    nd = len(shape)
    return pl.BlockSpec(shape, lambda *_: (0,) * nd, pipeline_mode=pl.Buffered(1))


def _dilated_spec(d, width):
    return pl.BlockSpec((1, d, ROW_TILE // d, width), lambda b, i: (b, 0, i, 0))


def _params(*sem):
    return pltpu.CompilerParams(dimension_semantics=sem, vmem_limit_bytes=VMEM_LIMIT_BYTES)


def _mod_body(c_ref, w_ref, b_ref, o_ref):
    c = c_ref[...]
    o_ref[...] = _dot(_silu(c).astype(BF), w_ref[...].astype(BF)) + b_ref[...]


def _mod_call(c, w_mod, b_mod):
    n = w_mod.shape[1]
    bn = D_MODEL
    return pl.pallas_call(
        _mod_body,
        grid=(n // bn,),
        in_specs=[pl.BlockSpec((BATCH, D_MODEL), lambda j: (0, 0)),
                  pl.BlockSpec((D_MODEL, bn), lambda j: (0, j)),
                  pl.BlockSpec((1, bn), lambda j: (0, j))],
        out_specs=pl.BlockSpec((BATCH, bn), lambda j: (0, j)),
        out_shape=jax.ShapeDtypeStruct((BATCH, n), F32),
        compiler_params=_params("arbitrary"),
        name="mod",
    )(c, w_mod, b_mod.reshape(1, n))


def _ffn_body(x_ref, mod_ref, g_ref, w1_ref, w3_ref, w2_ref, o_ref, *, mod_row):
    x = x_ref[0]
    m = mod_ref[0]
    shift, scale, gate = (m[mod_row + i:mod_row + i + 1] for i in range(3))
    u = _norm_modulate(x, g_ref[...], shift, scale).astype(BF)
    fc = D_FF // FF_CHUNKS
    acc = None
    for c in range(FF_CHUNKS):
        h1 = _dot(u, w1_ref[:, c * fc:(c + 1) * fc])
        h3 = _dot(u, w3_ref[:, c * fc:(c + 1) * fc])
        p = _dot((_silu(h1) * h3).astype(BF), w2_ref[c * fc:(c + 1) * fc, :])
        acc = p if acc is None else acc + p
    o_ref[0] = x + (0.5 * (1.0 + gate)) * acc


def _ffn_call(x, mod, g, w1, w3, w2, mod_row):
    row = pl.BlockSpec((1, FFN_TILE, D_MODEL), lambda b, i: (b, i, 0))
    return pl.pallas_call(
        functools.partial(_ffn_body, mod_row=mod_row),
        grid=(BATCH, SEQ // FFN_TILE),
        in_specs=[row,
                  pl.BlockSpec((1, N_MOD, D_MODEL), lambda b, i: (b, 0, 0)),
                  _const_spec((1, D_MODEL)),
                  _const_spec((D_MODEL, D_FF)),
                  _const_spec((D_MODEL, D_FF)),
                  _const_spec((D_FF, D_MODEL))],
        out_specs=row,
        out_shape=jax.ShapeDtypeStruct((BATCH, SEQ, D_MODEL), F32),
        compiler_params=_params("parallel", "parallel"),
        name="ffn%d" % mod_row,
    )(x, mod, g, w1, w3, w2)


QK_W = 2 * ATT_OUT_WIDTH
GRP_W = 3 * ATT_OUT_WIDTH
GD_PAD = 128


def _proj_body(h_ref, mod_ref, g_ref, watt_ref, wgq_ref, wgkt_ref, wgv_ref, wgr_ref, wgd_ref,
               wgate_ref, hsum_ref, hexp_ref, qkg_ref,
               a0_ref, a1_ref, a2_ref, gq_ref, gkt_ref, gv_ref, gr_ref, gd_ref, gate_ref, perm_ref):
    m = mod_ref[0]
    u = _norm_modulate(h_ref[0], g_ref[...], m[3:4], m[4:5]).astype(BF)
    for gi, a_ref in enumerate((a0_ref, a1_ref, a2_ref)):
        d = ATT_GROUPS[gi][1]
        y = _dot(u, watt_ref[:, gi * GRP_W:(gi + 1) * GRP_W])
        qk = y[:, :QK_W]
        ss = _dot((qk * qk).astype(BF), hsum_ref[...])
        r_hi, r_lo = _split_bf16(lax.rsqrt(ss * (1.0 / ATT_HEAD_DIM) + EPS))
        rx = _dot(r_hi, hexp_ref[...]) + _dot(r_lo, hexp_ref[...])
        qkn = qk * rx * qkg_ref[...]
        if d == 1:
            a_ref[0, 0, :, :QK_W] = qkn.astype(BF)
            a_ref[0, 0, :, QK_W:] = y[:, QK_W:].astype(BF)
        else:
            for j in range(GRP_W // 128):
                cols = slice(j * 128, (j + 1) * 128)
                perm_ref[j] = qkn[:, cols] if j < QK_W // 128 else y[:, cols]
            for r in range(d):
                for j in range(GRP_W // 128):
                    a_ref[0, r, :, j * 128:(j + 1) * 128] = (
                        perm_ref[j, pl.ds(r, ROW_TILE // d, stride=d), :].astype(BF))
    gq_ref[0] = _dot(u, wgq_ref[...]).astype(BF)
    gkt_ref[0] = _dot_nt(wgkt_ref[...], u).astype(BF)
    gv_ref[0] = _dot(u, wgv_ref[...]).astype(BF)
    gr_ref[0] = _dot(u, wgr_ref[...]).astype(BF)
    gd_ref[0] = _dot(u, wgd_ref[...]).astype(BF)
    gate_ref[0] = _dot(u, wgate_ref[...]).astype(BF)


def _proj_call(h, mod, g, watt, wgq, wgkt, wgv, wgr, wgd, wgate, hsum, hexp, qkg):
    def row(w):
        return pl.BlockSpec((1, ROW_TILE, w), lambda b, i: (b, i, 0))

    def out(w):
        return jax.ShapeDtypeStruct((BATCH, SEQ, w), BF)

    kt_spec = pl.BlockSpec((1, GLA_KEY_DIM, ROW_TILE), lambda b, i: (b, 0, i))
    kt_shape = jax.ShapeDtypeStruct((BATCH, GLA_KEY_DIM, SEQ), BF)
    widths = (GLA_VAL_DIM, GLA_VAL_DIM, GD_PAD, 2 * D_MODEL)
    consts = (g, watt, wgq, wgkt, wgv, wgr, wgd, wgate, hsum, hexp, qkg)
    dils = [d for _, d in ATT_GROUPS]
    return pl.pallas_call(
        _proj_body,
        grid=(BATCH, SEQ // ROW_TILE),
        in_specs=[row(D_MODEL), pl.BlockSpec((1, N_MOD, D_MODEL), lambda b, i: (b, 0, 0))]
                 + [_const_spec(a.shape) for a in consts],
        out_specs=[_dilated_spec(d, GRP_W) for d in dils] + [row(GLA_KEY_DIM), kt_spec]
                  + [row(w) for w in widths],
        out_shape=[jax.ShapeDtypeStruct((BATCH, d, SEQ // d, GRP_W), BF) for d in dils]
                  + [out(GLA_KEY_DIM), kt_shape] + [out(w) for w in widths],
        scratch_shapes=[pltpu.VMEM((GRP_W // 128, ROW_TILE, 128), F32)],
        compiler_params=_params("parallel", "parallel"),
        name="proj",
    )(h, mod, *consts)


def _attn_block(a_ref, bias, o_ref, l_ref, r, q0, k0, nk):
    W = ATT_OUT_WIDTH
    q = a_ref[0, r, pl.ds(q0, ATT_BLK), :W]
    k = a_ref[0, r, pl.ds(k0, nk), W:2 * W]
    v = a_ref[0, r, pl.ds(k0, nk), 2 * W:]
    head_of_lane = lax.broadcasted_iota(jnp.int32, (ATT_BLK, ATT_OUT_WIDTH), 1) // ATT_HEAD_DIM
    zero = jnp.zeros_like(q)
    q4 = jnp.concatenate([jnp.where(head_of_lane == j, q, zero)
                          for j in range(ATT_HEADS_PER_GROUP)], axis=0)
    s = _dot_nt(q4, k) + bias
    mx = jnp.max(s, axis=-1, keepdims=True)
    e = jnp.exp(s - mx)
    den = jnp.sum(e, axis=-1, keepdims=True)
    pv = _dot(e.astype(BF), v) / den
    lse = mx + jnp.log(den)
    lane = lax.broadcasted_iota(jnp.int32, (ATT_BLK, 128), 1)
    o = jnp.zeros((ATT_BLK, ATT_OUT_WIDTH), F32)
    l = jnp.zeros((ATT_BLK, 128), F32)
    for j in range(ATT_HEADS_PER_GROUP):
        rows = slice(j * ATT_BLK, (j + 1) * ATT_BLK)
        o = jnp.where(head_of_lane == j, pv[rows], o)
        l = jnp.where(lane == j, lse[rows], l)
    o_ref[0, r, pl.ds(q0, ATT_BLK), :] = o.astype(o_ref.dtype)
    l_ref[0, r, pl.ds(q0, ATT_BLK), :] = l


def _attn_body(a_ref, bias_ref, o_ref, l_ref, *, subseqs, n_blocks):
    for r in range(subseqs):
        _attn_block(a_ref, bias_ref[:, ATT_BLK:], o_ref, l_ref, r, 0, 0, ATT_BLK)
        if n_blocks > 1:
            def step(n, carry, r=r):
                q0 = pl.multiple_of(n * ATT_BLK, ATT_BLK)
                k0 = pl.multiple_of((n - 1) * ATT_BLK, ATT_BLK)
                _attn_block(a_ref, bias_ref[...], o_ref, l_ref, r, q0, k0, 2 * ATT_BLK)
                return carry
            lax.fori_loop(1, n_blocks, step, 0)


def _attn_bias(group):
    _, dilation = ATT_GROUPS[group]
    heads = np.arange(group * ATT_HEADS_PER_GROUP, (group + 1) * ATT_HEADS_PER_GROUP)
    slopes = (2.0 ** (-ALIBI_MAX * (heads + 1).astype(np.float32) / ATT_HEADS)).astype(np.float32)
    qi = np.arange(ATT_BLK)[:, None]
    kj = np.arange(2 * ATT_BLK)[None, :]
    steps = qi + ATT_BLK - kj
    valid = (steps >= 0) & (steps <= ATT_BLK)
    bias = -(slopes * np.float32(dilation))[:, None, None] * steps.astype(np.float32)[None]
    bias = np.where(valid[None], bias, np.float32(NEG_BIG)).astype(np.float32)
    return bias.reshape(ATT_HEADS_PER_GROUP * ATT_BLK, 2 * ATT_BLK)


def _attn_call(a, group):
    _, d = ATT_GROUPS[group]
    L = SEQ // d
    n_blocks = L // ATT_BLK
    subseqs = min(d, 4)

    def spec(w):
        return pl.BlockSpec((1, subseqs, L, w), lambda b, r: (b, r, 0, 0))

    return pl.pallas_call(
        functools.partial(_attn_body, subseqs=subseqs, n_blocks=n_blocks),
        grid=(BATCH, d // subseqs),
        in_specs=[spec(GRP_W), _const_spec((ATT_HEADS_PER_GROUP * ATT_BLK, 2 * ATT_BLK))],
        out_specs=[spec(ATT_OUT_WIDTH), spec(128)],
        out_shape=[jax.ShapeDtypeStruct((BATCH, d, L, ATT_OUT_WIDTH), BF),
                   jax.ShapeDtypeStruct((BATCH, d, L, 128), F32)],
        compiler_params=_params("parallel", "parallel"),
        name="attn%d" % d,
    )(a, jnp.asarray(_attn_bias(group)))


GLA_PAIR = 2 * GLA_CHUNK


def _gla_pair(p, states, gq_ref, gkt_ref, gv_ref, gd_ref, up_ref, gb_ref, tril2_ref, o_ref):
    C, P = GLA_CHUNK, GLA_PAIR
    rows = slice(p * P, (p + 1) * P)
    x = _dot(gd_ref[0, rows, :], up_ref[...]) + gb_ref[...]
    log_a = (jnp.minimum(x, 0.0) - jnp.log1p(jnp.exp(-jnp.abs(x)))) * (1.0 / GLA_TAU)
    la_hi, la_lo = _split_bf16(log_a)
    b = _dot(tril2_ref[...], jnp.concatenate([la_hi, la_lo], axis=0))
    q_dec = (gq_ref[0, rows, :].astype(F32) * (GLA_DK ** -0.5) * jnp.exp(b)).astype(BF)
    b_t = b.T
    k_dec_t = gkt_ref[0, :, rows].astype(F32) * jnp.exp(-b_t)
    decay_t = jnp.exp(b_t)
    lane = lax.broadcasted_iota(jnp.int32, (GLA_KEY_DIM, P), 1)
    k_dec_t_bf = k_dec_t.astype(BF)
    k_dec_t_chunk = [jnp.where(lane < C, k_dec_t, 0.0).astype(BF),
                     jnp.where(lane >= C, k_dec_t, 0.0).astype(BF)]
    ri = lax.broadcasted_iota(jnp.int32, (P, P), 0)
    ci = lax.broadcasted_iota(jnp.int32, (P, P), 1)
    causal = (ri >= ci) & ((ri >= C) == (ci >= C))
    new_states = []
    for h in range(GLA_HEADS):
        kc = slice(h * GLA_DK, (h + 1) * GLA_DK)
        vc = slice(h * GLA_DV, (h + 1) * GLA_DV)
        v = gv_ref[0, rows, vc]
        attn = jnp.where(causal, _dot(q_dec[:, kc], k_dec_t_bf[kc, :]), 0.0).astype(BF)
        stacked = _dot(jnp.concatenate([k_dec_t_chunk[0][kc, :], k_dec_t_chunk[1][kc, :], attn],
                                       axis=0), v)
        st = states[h]
        for cc in range(2):
            crow = slice(cc * C, (cc + 1) * C)
            o = stacked[2 * GLA_DK + cc * C:2 * GLA_DK + (cc + 1) * C] + _dot(q_dec[crow, kc],
                                                                              st.astype(BF))
            o_ref[0, p * P + cc * C:p * P + (cc + 1) * C, vc] = o.astype(o_ref.dtype)
            last = (cc + 1) * C - 1
            st = decay_t[kc, last:last + 1] * (st + stacked[cc * GLA_DK:(cc + 1) * GLA_DK])
        new_states.append(st)
    return new_states


def _gla_body(gq_ref, gkt_ref, gv_ref, gd_ref, up_ref, gb_ref, tril2_ref, o_ref, state_ref):
    @pl.when(pl.program_id(1) == 0)
    def _():
        state_ref[...] = jnp.zeros_like(state_ref)

    states = [state_ref[h] for h in range(GLA_HEADS)]
    for p in range(GLA_BLOCK // GLA_PAIR):
        states = _gla_pair(p, states, gq_ref, gkt_ref, gv_ref, gd_ref, up_ref, gb_ref,
                           tril2_ref, o_ref)
    for h in range(GLA_HEADS):
        state_ref[h] = states[h]


def _pair_tril2():
    i = np.arange(GLA_PAIR)
    t = ((i[:, None] // GLA_CHUNK) == (i[None, :] // GLA_CHUNK)) & (i[:, None] >= i[None, :])
    return np.concatenate([t, t], axis=1).astype(np.float32)


def _gla_call(gq, gkt, gv, gd, up, gbias):
    def row(w):
        return pl.BlockSpec((1, GLA_BLOCK, w), lambda b, i: (b, i, 0))

    consts = (up, gbias, jnp.asarray(_pair_tril2(), BF))
    return pl.pallas_call(
        _gla_body,
        grid=(BATCH, SEQ // GLA_BLOCK),
        in_specs=[row(GLA_KEY_DIM),
                  pl.BlockSpec((1, GLA_KEY_DIM, GLA_BLOCK), lambda b, i: (b, 0, i)),
                  row(GLA_VAL_DIM), row(GD_PAD)]
                 + [_const_spec(a.shape) for a in consts],
        out_specs=row(GLA_VAL_DIM),
        out_shape=jax.ShapeDtypeStruct((BATCH, SEQ, GLA_VAL_DIM), BF),
        scratch_shapes=[pltpu.VMEM((GLA_HEADS, GLA_DK, GLA_DV), F32)],
        compiler_params=_params("parallel", "arbitrary"),
        name="gla",
    )(gq, gkt, gv, gd, *consts)


def _undilate(src_ref, scr_ref, d):
    if d == 1:
        return src_ref[0, 0].astype(F32)
    planes = scr_ref.shape[0]
    for r in range(d):
        x = src_ref[0, r].astype(F32)
        for j in range(planes):
            scr_ref[j, pl.ds(r, ROW_TILE // d, stride=d), :] = x[:, j * 128:(j + 1) * 128]
    return jnp.concatenate([scr_ref[j] for j in range(planes)], axis=-1)


def _merge_body(h_ref, mod_ref, o0_ref, o1_ref, o2_ref, l0_ref, l1_ref, l2_ref,
                ogla_ref, gr_ref, gate_ref, hexp_ref, gain_ref, wba_ref, wbg_ref, wout_ref, out_ref,
                os1_ref, os2_ref, ls1_ref, ls2_ref):
    dils = [d for _, d in ATT_GROUPS]
    lses = [_undilate(r, s, d) for r, s, d in
            zip((l0_ref, l1_ref, l2_ref), (None, ls1_ref, ls2_ref), dils)]
    outs = [_undilate(r, s, d) for r, s, d in
            zip((o0_ref, o1_ref, o2_ref), (None, os1_ref, os2_ref), dils)]
    mx = jnp.maximum(jnp.maximum(lses[0], lses[1]), lses[2])
    es = [jnp.exp(l - mx) for l in lses]
    den = es[0] + es[1] + es[2]
    o_att = None
    for e, o in zip(es, outs):
        w_hi, w_lo = _split_bf16(e / den)
        t = (_dot(w_hi, hexp_ref[...]) + _dot(w_lo, hexp_ref[...])) * o
        o_att = t if o_att is None else o_att + t
    att = _dot(o_att.astype(BF), wba_ref[...])
    gated = []
    for h in range(GLA_HEADS):
        vc = slice(h * GLA_DV, (h + 1) * GLA_DV)
        o = ogla_ref[0, :, vc].astype(F32)
        ms = jnp.mean(o * o, axis=-1, keepdims=True)
        y = o * lax.rsqrt(ms + EPS) * gain_ref[:, vc]
        gated.append((y * _silu(gr_ref[0, :, vc].astype(F32))).astype(BF))
    gla = _dot(jnp.concatenate(gated, axis=1), wbg_ref[...])
    g_att = jax.nn.sigmoid(gate_ref[0, :, :D_MODEL].astype(F32))
    g_gla = jax.nn.sigmoid(gate_ref[0, :, D_MODEL:].astype(F32))
    merged = (g_att * att + g_gla * gla).astype(BF)
    gate = mod_ref[0][5:6]
    out_ref[0] = h_ref[0] + (1.0 + gate) * _dot(merged, wout_ref[...])


def _merge_call(h, mod, os_, ls_, ogla, gr, gates, hexp, gain, wba, wbg, wout):
    def row(w):
        return pl.BlockSpec((1, ROW_TILE, w), lambda b, i: (b, i, 0))

    consts = (hexp, gain, wba, wbg, wout)
    dils = [d for _, d in ATT_GROUPS]
    return pl.pallas_call(
        _merge_body,
        grid=(BATCH, SEQ // ROW_TILE),
        in_specs=[row(D_MODEL), pl.BlockSpec((1, N_MOD, D_MODEL), lambda b, i: (b, 0, 0))]
                 + [_dilated_spec(d, ATT_OUT_WIDTH) for d in dils]
                 + [_dilated_spec(d, 128) for d in dils]
                 + [row(GLA_VAL_DIM), row(GLA_VAL_DIM), row(2 * D_MODEL)]
                 + [_const_spec(a.shape) for a in consts],
        out_specs=row(D_MODEL),
        out_shape=jax.ShapeDtypeStruct((BATCH, SEQ, D_MODEL), F32),
        scratch_shapes=[pltpu.VMEM((ATT_OUT_WIDTH // 128, ROW_TILE, 128), F32)] * 2
                       + [pltpu.VMEM((1, ROW_TILE, 128), F32)] * 2,
        compiler_params=_params("parallel", "parallel"),
        name="merge",
    )(h, mod, *os_, *ls_, ogla, gr, gates, *consts)


def _head_matrix(n_lanes_in, width):
    j = np.arange(n_lanes_in)[:, None]
    c = np.arange(width)[None, :]
    return (c // ATT_HEAD_DIM == j).astype(np.float32)


def _layer(h, mod, g_ffn1, f1w1, f1w3, f1w2, g_mix, w_in, q_norm_g, k_norm_g, gate_up, gate_bias,
           out_norm_g, w_branch_att, w_branch_gla, w_out, g_ffn2, f2w1, f2w3, f2w2):
    bf = lambda w: w.astype(BF)
    row = lambda v: v.reshape(1, -1)

    h = _ffn_call(h, mod, row(g_ffn1), bf(f1w1), bf(f1w3), bf(f1w2), 0)

    splits = np.cumsum((ATT_WIDTH, ATT_WIDTH, ATT_WIDTH, GLA_KEY_DIM, GLA_KEY_DIM, GLA_VAL_DIM,
                        GLA_VAL_DIM, GLA_GATE_RANK))
    aq, ak, av, gq, gk, gv, gr, gdn, gates = jnp.split(bf(w_in), [int(s) for s in splits], axis=1)
    W = ATT_OUT_WIDTH
    watt = jnp.concatenate([t[:, gi * W:(gi + 1) * W] for gi in range(len(ATT_GROUPS))
                            for t in (aq, ak, av)], axis=1)
    wgd =jnp.pad(gdn, ((0, 0), (0, GD_PAD - GLA_GATE_RANK)))
    hexp_qk = jnp.asarray(_head_matrix(128, QK_W), BF)
    hsum_qk = hexp_qk.T
    qkg = jnp.concatenate([jnp.tile(q_norm_g * (ATT_HEAD_DIM ** -0.5), ATT_HEADS_PER_GROUP),
                           jnp.tile(k_norm_g, ATT_HEADS_PER_GROUP)])
    a0, a1, a2, pgq, pgkt, pgv, pgr, pgd, pgate = _proj_call(
        h, mod, row(g_mix), watt, gq, gk.T, gv, gr, wgd, gates, hsum_qk, hexp_qk, row(qkg))

    outs = [_attn_call(a, gi) for gi, a in enumerate((a0, a1, a2))]
    up = jnp.pad(bf(gate_up), ((0, GD_PAD - GLA_GATE_RANK), (0, 0)))
    ogla = _gla_call(pgq, pgkt, pgv, pgd, up, row(gate_bias))

    hexp_o = jnp.asarray(_head_matrix(128, ATT_OUT_WIDTH), BF)
    h = _merge_call(h, mod, [o for o, _ in outs], [l for _, l in outs], ogla, pgr, pgate,
                    hexp_o, row(jnp.tile(out_norm_g, GLA_HEADS)), bf(w_branch_att),
                    bf(w_branch_gla), bf(w_out))

    return _ffn_call(h, mod, row(g_ffn2), bf(f2w1), bf(f2w3), bf(f2w2), 6)


def kernel(x, c, w_mod, b_mod, g_ffn1, ffn1_w1, ffn1_w3, ffn1_w2, g_mix, w_in, q_norm_g, k_norm_g,
           gla_gate_up, gla_gate_bias, gla_out_norm_g, w_branch_att, w_branch_gla, w_out,
           g_ffn2, ffn2_w1, ffn2_w3, ffn2_w2):
    h = x
    for l in range(w_mod.shape[0]):
        mod = _mod_call(c, w_mod[l], b_mod[l]).reshape(BATCH, N_MOD, D_MODEL)
        h = _layer(h, mod, g_ffn1[l], ffn1_w1[l], ffn1_w3[l], ffn1_w2[l], g_mix[l], w_in[l],
                   q_norm_g[l], k_norm_g[l], gla_gate_up[l], gla_gate_bias[l], gla_out_norm_g[l],
                   w_branch_att[l], w_branch_gla[l], w_out[l], g_ffn2[l], ffn2_w1[l], ffn2_w3[l],
                   ffn2_w2[l])
    return h
```

```python
import functools

import numpy as np
import jax
import jax.numpy as jnp
from jax import lax
from jax.experimental import pallas as pl
from jax.experimental.pallas import tpu as pltpu

D_MODEL = 1024
BATCH = 16
SEQ = 2048
N_MOD = 9
D_FF = 2816
EPS = 1e-6
ATT_GROUPS = ((128, 1), (512, 4), (2048, 16))
ATT_HEADS_PER_GROUP = 4
ATT_HEADS = 12
ATT_HEAD_DIM = 64
ATT_WIDTH = 768
ATT_OUT_WIDTH = 256
ALIBI_MAX = 8.0
GLA_HEADS = 4
GLA_KEY_DIM = 512
GLA_VAL_DIM = 1024
GLA_DK = 128
GLA_DV = 256
GLA_GATE_RANK = 16
GLA_TAU = 16.0
GLA_CHUNK = 64
ATT_BLK = 128
NEG_BIG = -1e30

BF = jnp.bfloat16
F32 = jnp.float32

VMEM_LIMIT_BYTES = 56 * 1024 * 1024

ROW_TILE = 512
FFN_TILE = 1024
MXU_TILE = 256
FF_CHUNK_TILES = 3
GLA_BLOCK = 512


def _dot(a, b):
    return jnp.dot(a, b, preferred_element_type=F32)


def _dot_nt(a, b):
    return lax.dot_general(a, b, (((1,), (1,)), ((), ())), preferred_element_type=F32)


def _dot_tn(a, b):
    return lax.dot_general(a, b, (((0,), (0,)), ((), ())), preferred_element_type=F32)


def _split_bf16(x):
    hi = x.astype(BF)
    lo = (x - hi.astype(F32)).astype(BF)
    return hi, lo


def _dot_split(x, stacked_ref):
    hi, lo = _split_bf16(x)
    return _dot(jnp.concatenate([hi, lo], axis=1), stacked_ref[...])


def _silu(x):
    return x * jax.nn.sigmoid(x)


def _norm_modulate(x, g, shift, scale):
    ms = jnp.mean(x * x, axis=-1, keepdims=True)
    y = x * lax.rsqrt(ms + EPS) * g
    return y * (1.0 + scale) + shift


def _const_spec(shape):
    nd = len(shape)
    return pl.BlockSpec(shape, lambda *_: (0,) * nd, pipeline_mode=pl.Buffered(1))


def _dilated_spec(d, width):
    return pl.BlockSpec((1, d, ROW_TILE // d, width), lambda b, i: (b, 0, i, 0))


def _params(*sem):
    return pltpu.CompilerParams(dimension_semantics=sem, vmem_limit_bytes=VMEM_LIMIT_BYTES)


def _mod_body(c_ref, w_ref, b_ref, o_ref):
    c = c_ref[...]
    o_ref[...] = _dot(_silu(c).astype(BF), w_ref[...].astype(BF)) + b_ref[...]


def _mod_call(c, w_mod, b_mod):
    n = w_mod.shape[1]
    bn = D_MODEL
    return pl.pallas_call(
        _mod_body,
        grid=(n // bn,),
        in_specs=[pl.BlockSpec((BATCH, D_MODEL), lambda j: (0, 0)),
                  pl.BlockSpec((D_MODEL, bn), lambda j: (0, j)),
                  pl.BlockSpec((1, bn), lambda j: (0, j))],
        out_specs=pl.BlockSpec((BATCH, bn), lambda j: (0, j)),
        out_shape=jax.ShapeDtypeStruct((BATCH, n), F32),
        compiler_params=_params("arbitrary"),
        name="mod",
    )(c, w_mod, b_mod.reshape(1, n))


def _ffn_body(x_ref, mod_ref, g_ref, w1_ref, w3_ref, w2_ref, o_ref, *, mod_row):
    x = x_ref[0]
    m = mod_ref[0]
    shift, scale, gate = (m[mod_row + i:mod_row + i + 1] for i in range(3))
    u = _norm_modulate(x, g_ref[...], shift, scale).astype(BF)
    fc = FF_CHUNK_TILES * MXU_TILE
    acc = None
    for lo in range(0, D_FF, fc):
        cols = slice(lo, min(lo + fc, D_FF))
        h1 = _dot(u, w1_ref[:, cols])
        h3 = _dot(u, w3_ref[:, cols])
        p = _dot((_silu(h1) * h3).astype(BF), w2_ref[cols, :])
        acc = p if acc is None else acc + p
    o_ref[0] = x + (0.5 * (1.0 + gate)) * acc


def _ffn_call(x, mod, g, w1, w3, w2, mod_row):
    row = pl.BlockSpec((1, FFN_TILE, D_MODEL), lambda b, i: (b, i, 0))
    return pl.pallas_call(
        functools.partial(_ffn_body, mod_row=mod_row),
        grid=(BATCH, SEQ // FFN_TILE),
        in_specs=[row,
                  pl.BlockSpec((1, N_MOD, D_MODEL), lambda b, i: (b, 0, 0)),
                  _const_spec((1, D_MODEL)),
                  _const_spec((D_MODEL, D_FF)),
                  _const_spec((D_MODEL, D_FF)),
                  _const_spec((D_FF, D_MODEL))],
        out_specs=row,
        out_shape=jax.ShapeDtypeStruct((BATCH, SEQ, D_MODEL), F32),
        compiler_params=_params("parallel", "parallel"),
        name="ffn%d" % mod_row,
    )(x, mod, g, w1, w3, w2)


QK_W = 2 * ATT_OUT_WIDTH
GRP_W = 3 * ATT_OUT_WIDTH
GD_PAD = 128


def _proj_body(h_ref, mod_ref, g_ref, watt_ref, wgq_ref, wgkt_ref, wgv_ref, wgr_ref, wgd_ref,
               wgate_ref, hsum_ref, hexp_ref, qkg_ref,
               a0_ref, a1_ref, a2_ref, gq_ref, gkt_ref, gv_ref, gr_ref, gd_ref, gate_ref, perm_ref):
    m = mod_ref[0]
    u = _norm_modulate(h_ref[0], g_ref[...], m[3:4], m[4:5]).astype(BF)
    for gi, a_ref in enumerate((a0_ref, a1_ref, a2_ref)):
        d = ATT_GROUPS[gi][1]
        y = _dot(u, watt_ref[:, gi * GRP_W:(gi + 1) * GRP_W])
        qk = y[:, :QK_W]
        ss = _dot((qk * qk).astype(BF), hsum_ref[...])
        rx = _dot_split(lax.rsqrt(ss * (1.0 / ATT_HEAD_DIM) + EPS), hexp_ref)
        qkn = qk * rx * qkg_ref[...]
        if d == 1:
            a_ref[0, 0, :, :QK_W] = qkn.astype(BF)
            a_ref[0, 0, :, QK_W:] = y[:, QK_W:].astype(BF)
        else:
            for j in range(GRP_W // 128):
                cols = slice(j * 128, (j + 1) * 128)
                perm_ref[j] = qkn[:, cols] if j < QK_W // 128 else y[:, cols]
            for r in range(d):
                for j in range(GRP_W // 128):
                    a_ref[0, r, :, j * 128:(j + 1) * 128] = (
                        perm_ref[j, pl.ds(r, ROW_TILE // d, stride=d), :].astype(BF))
    gq_ref[0] = _dot(u, wgq_ref[...]).astype(BF)
    gkt_ref[0] = _dot_nt(wgkt_ref[...], u).astype(BF)
    gv_ref[0] = _dot(u, wgv_ref[...]).astype(BF)
    gr_ref[0] = _dot(u, wgr_ref[...]).astype(BF)
    gd_ref[0] = _dot(u, wgd_ref[...]).astype(BF)
    gate_ref[0] = _dot(u, wgate_ref[...]).astype(BF)


def _proj_call(h, mod, g, watt, wgq, wgkt, wgv, wgr, wgd, wgate, hsum, hexp, qkg):
    def row(w):
        return pl.BlockSpec((1, ROW_TILE, w), lambda b, i: (b, i, 0))

    def out(w):
        return jax.ShapeDtypeStruct((BATCH, SEQ, w), BF)

    kt_spec = pl.BlockSpec((1, GLA_KEY_DIM, ROW_TILE), lambda b, i: (b, 0, i))
    kt_shape = jax.ShapeDtypeStruct((BATCH, GLA_KEY_DIM, SEQ), BF)
    widths = (GLA_VAL_DIM, GLA_VAL_DIM, GD_PAD, 2 * D_MODEL)
    consts = (g, watt, wgq, wgkt, wgv, wgr, wgd, wgate, hsum, hexp, qkg)
    dils = [d for _, d in ATT_GROUPS]
    return pl.pallas_call(
        _proj_body,
        grid=(BATCH, SEQ // ROW_TILE),
        in_specs=[row(D_MODEL), pl.BlockSpec((1, N_MOD, D_MODEL), lambda b, i: (b, 0, 0))]
                 + [_const_spec(a.shape) for a in consts],
        out_specs=[_dilated_spec(d, GRP_W) for d in dils] + [row(GLA_KEY_DIM), kt_spec]
                  + [row(w) for w in widths],
        out_shape=[jax.ShapeDtypeStruct((BATCH, d, SEQ // d, GRP_W), BF) for d in dils]
                  + [out(GLA_KEY_DIM), kt_shape] + [out(w) for w in widths],
        scratch_shapes=[pltpu.VMEM((GRP_W // 128, ROW_TILE, 128), F32)],
        compiler_params=_params("parallel", "parallel"),
        name="proj",
    )(h, mod, *consts)


ATT_GROUP_BLOCKS = 4


def _attn_blocks(a_ref, bias_ref, o_ref, l_ref, blocks):
    W, NH = ATT_OUT_WIDTH, ATT_HEADS_PER_GROUP
    head_of_lane = lax.broadcasted_iota(jnp.int32, (ATT_BLK, W), 1) // ATT_HEAD_DIM
    lane = lax.broadcasted_iota(jnp.int32, (ATT_BLK, 128), 1)
    qs, ks, vs, biases, q0s = [], [], [], [], []
    for r, n, first in blocks:
        if first:
            q0, k0, nk = 0, 0, ATT_BLK
            biases.append(bias_ref[:, ATT_BLK:])
        else:
            q0, k0, nk = n * ATT_BLK, (n - 1) * ATT_BLK, 2 * ATT_BLK
            if not isinstance(n, int):
                q0, k0 = pl.multiple_of(q0, ATT_BLK), pl.multiple_of(k0, ATT_BLK)
            biases.append(bias_ref[...])
        q0s.append(q0)
        q = a_ref[0, r, pl.ds(q0, ATT_BLK), :W]
        zero = jnp.zeros_like(q)
        qs.append(jnp.concatenate([jnp.where(head_of_lane == j, q, zero) for j in range(NH)],
                                  axis=0))
        ks.append(a_ref[0, r, pl.ds(k0, nk), W:2 * W])
        vs.append(a_ref[0, r, pl.ds(k0, nk), 2 * W:])
    ss = [_dot_nt(q4, k) + b for q4, k, b in zip(qs, ks, biases)]
    mxs = [jnp.max(s, axis=-1, keepdims=True) for s in ss]
    es = [jnp.exp(s - mx) for s, mx in zip(ss, mxs)]
    dens = [jnp.sum(e, axis=-1, keepdims=True) for e in es]
    pvs = [_dot(e.astype(BF), v) for e, v in zip(es, vs)]
    for (r, _, _), q0, pv, mx, den in zip(blocks, q0s, pvs, mxs, dens):
        pv = pv * (1.0 / den)
        lse = mx + jnp.log(den)
        o = jnp.zeros((ATT_BLK, W), F32)
        l = jnp.zeros((ATT_BLK, 128), F32)
        for j in range(NH):
            rows = slice(j * ATT_BLK, (j + 1) * ATT_BLK)
            o = jnp.where(head_of_lane == j, pv[rows], o)
            l = jnp.where(lane == j, lse[rows], l)
        o_ref[0, r, pl.ds(q0, ATT_BLK), :] = o.astype(o_ref.dtype)
        l_ref[0, r, pl.ds(q0, ATT_BLK), :] = l


def _attn_body(a_ref, bias_ref, o_ref, l_ref, *, subseqs, n_blocks):
    G = ATT_GROUP_BLOCKS
    if n_blocks == 1:
        for r0 in range(0, subseqs, G):
            _attn_blocks(a_ref, bias_ref, o_ref, l_ref,
                         [(r, 0, True) for r in range(r0, min(r0 + G, subseqs))])
        return
    for r in range(subseqs):
        _attn_blocks(a_ref, bias_ref, o_ref, l_ref, [(r, n, n == 0) for n in range(G)])
        if n_blocks > G:
            def step(i, carry, r=r):
                _attn_blocks(a_ref, bias_ref, o_ref, l_ref,
                             [(r, i * G + j, False) for j in range(G)])
                return carry
            lax.fori_loop(1, n_blocks // G, step, 0)


def _attn_bias(group):
    _, dilation = ATT_GROUPS[group]
    heads = np.arange(group * ATT_HEADS_PER_GROUP, (group + 1) * ATT_HEADS_PER_GROUP)
    slopes = (2.0 ** (-ALIBI_MAX * (heads + 1).astype(np.float32) / ATT_HEADS)).astype(np.float32)
    qi = np.arange(ATT_BLK)[:, None]
    kj = np.arange(2 * ATT_BLK)[None, :]
    steps = qi + ATT_BLK - kj
    valid = (steps >= 0) & (steps <= ATT_BLK)
    bias = -(slopes * np.float32(dilation))[:, None, None] * steps.astype(np.float32)[None]
    bias = np.where(valid[None], bias, np.float32(NEG_BIG)).astype(np.float32)
    return bias.reshape(ATT_HEADS_PER_GROUP * ATT_BLK, 2 * ATT_BLK)


def _attn_call(a, group):
    _, d = ATT_GROUPS[group]
    L = SEQ // d
    n_blocks = L // ATT_BLK
    subseqs = min(d, 4)

    def spec(w):
        return pl.BlockSpec((1, subseqs, L, w), lambda b, r: (b, r, 0, 0))

    return pl.pallas_call(
        functools.partial(_attn_body, subseqs=subseqs, n_blocks=n_blocks),
        grid=(BATCH, d // subseqs),
        in_specs=[spec(GRP_W), _const_spec((ATT_HEADS_PER_GROUP * ATT_BLK, 2 * ATT_BLK))],
        out_specs=[spec(ATT_OUT_WIDTH), spec(128)],
        out_shape=[jax.ShapeDtypeStruct((BATCH, d, L, ATT_OUT_WIDTH), BF),
                   jax.ShapeDtypeStruct((BATCH, d, L, 128), F32)],
        compiler_params=_params("parallel", "parallel"),
        name="attn%d" % d,
    )(a, jnp.asarray(_attn_bias(group)))


GLA_PAIR = 2 * GLA_CHUNK


def _gla_pair(p, states, gq_ref, gkt_ref, gv_ref, gd_ref, up_ref, gb_ref, tril2_ref, o_ref):
    C, P = GLA_CHUNK, GLA_PAIR
    rows = slice(p * P, (p + 1) * P)
    x = _dot(gd_ref[0, rows, :], up_ref[...]) + gb_ref[...]
    log_a = (jnp.minimum(x, 0.0) - jnp.log1p(jnp.exp(-jnp.abs(x)))) * (1.0 / GLA_TAU)
    la_hi, la_lo = _split_bf16(log_a)
    b = _dot(tril2_ref[...], jnp.concatenate([la_hi, la_lo], axis=0))
    q_dec = (gq_ref[0, rows, :].astype(F32) * (GLA_DK ** -0.5) * jnp.exp(b)).astype(BF)
    b_t = b.T
    k_dec_t = gkt_ref[0, :, rows].astype(F32) * jnp.exp(-b_t)
    decay_t = jnp.exp(b_t)
    lane = lax.broadcasted_iota(jnp.int32, (GLA_KEY_DIM, P), 1)
    k_dec_t_bf = k_dec_t.astype(BF)
    k_dec_t_chunk = [jnp.where(lane < C, k_dec_t, 0.0).astype(BF),
                     jnp.where(lane >= C, k_dec_t, 0.0).astype(BF)]
    ri = lax.broadcasted_iota(jnp.int32, (P, P), 0)
    ci = lax.broadcasted_iota(jnp.int32, (P, P), 1)
    causal = (ri >= ci) & ((ri >= C) == (ci >= C))
    new_states = []
    for h in range(GLA_HEADS):
        kc = slice(h * GLA_DK, (h + 1) * GLA_DK)
        vc = slice(h * GLA_DV, (h + 1) * GLA_DV)
        v = gv_ref[0, rows, vc]
        attn = jnp.where(causal, _dot(q_dec[:, kc], k_dec_t_bf[kc, :]), 0.0).astype(BF)
        stacked = _dot(jnp.concatenate([k_dec_t_chunk[0][kc, :], k_dec_t_chunk[1][kc, :], attn],
                                       axis=0), v)
        st = states[h]
        for cc in range(2):
            crow = slice(cc * C, (cc + 1) * C)
            o = stacked[2 * GLA_DK + cc * C:2 * GLA_DK + (cc + 1) * C] + _dot(q_dec[crow, kc],
                                                                              st.astype(BF))
            o_ref[0, p * P + cc * C:p * P + (cc + 1) * C, vc] = o.astype(o_ref.dtype)
            last = (cc + 1) * C - 1
            st = decay_t[kc, last:last + 1] * (st + stacked[cc * GLA_DK:(cc + 1) * GLA_DK])
        new_states.append(st)
    return new_states


def _gla_body(gq_ref, gkt_ref, gv_ref, gd_ref, up_ref, gb_ref, tril2_ref, o_ref, state_ref):
    @pl.when(pl.program_id(1) == 0)
    def _():
        state_ref[...] = jnp.zeros_like(state_ref)

    states = [state_ref[h] for h in range(GLA_HEADS)]
    for p in range(GLA_BLOCK // GLA_PAIR):
        states = _gla_pair(p, states, gq_ref, gkt_ref, gv_ref, gd_ref, up_ref, gb_ref,
                           tril2_ref, o_ref)
    for h in range(GLA_HEADS):
        state_ref[h] = states[h]


def _pair_tril2():
    i = np.arange(GLA_PAIR)
    t = ((i[:, None] // GLA_CHUNK) == (i[None, :] // GLA_CHUNK)) & (i[:, None] >= i[None, :])
    return np.concatenate([t, t], axis=1).astype(np.float32)


def _gla_call(gq, gkt, gv, gd, up, gbias):
    def row(w):
        return pl.BlockSpec((1, GLA_BLOCK, w), lambda b, i: (b, i, 0))

    consts = (up, gbias, jnp.asarray(_pair_tril2(), BF))
    return pl.pallas_call(
        _gla_body,
        grid=(BATCH, SEQ // GLA_BLOCK),
        in_specs=[row(GLA_KEY_DIM),
                  pl.BlockSpec((1, GLA_KEY_DIM, GLA_BLOCK), lambda b, i: (b, 0, i)),
                  row(GLA_VAL_DIM), row(GD_PAD)]
                 + [_const_spec(a.shape) for a in consts],
        out_specs=row(GLA_VAL_DIM),
        out_shape=jax.ShapeDtypeStruct((BATCH, SEQ, GLA_VAL_DIM), BF),
        scratch_shapes=[pltpu.VMEM((GLA_HEADS, GLA_DK, GLA_DV), F32)],
        compiler_params=_params("parallel", "arbitrary"),
        name="gla",
    )(gq, gkt, gv, gd, *consts)


def _undilate(src_ref, scr_ref, d):
    if d == 1:
        return src_ref[0, 0].astype(F32)
    planes = scr_ref.shape[0]
    for r in range(d):
        x = src_ref[0, r].astype(F32)
        for j in range(planes):
            scr_ref[j, pl.ds(r, ROW_TILE // d, stride=d), :] = x[:, j * 128:(j + 1) * 128]
    return jnp.concatenate([scr_ref[j] for j in range(planes)], axis=-1)


def _merge_body(h_ref, mod_ref, o0_ref, o1_ref, o2_ref, l0_ref, l1_ref, l2_ref,
                ogla_ref, gr_ref, gate_ref, hexp_ref, gain_ref, wba_ref, wbg_ref, wout_ref, out_ref,
                os1_ref, os2_ref, ls1_ref, ls2_ref):
    dils = [d for _, d in ATT_GROUPS]
    lses = [_undilate(r, s, d) for r, s, d in
            zip((l0_ref, l1_ref, l2_ref), (None, ls1_ref, ls2_ref), dils)]
    outs = [_undilate(r, s, d) for r, s, d in
            zip((o0_ref, o1_ref, o2_ref), (None, os1_ref, os2_ref), dils)]
    mx = jnp.maximum(jnp.maximum(lses[0], lses[1]), lses[2])
    es = [jnp.exp(l - mx) for l in lses]
    den = es[0] + es[1] + es[2]
    o_att = None
    for e, o in zip(es, outs):
        t = _dot_split(e / den, hexp_ref) * o
        o_att = t if o_att is None else o_att + t
    att = _dot(o_att.astype(BF), wba_ref[...])
    gated = []
    for h in range(GLA_HEADS):
        vc = slice(h * GLA_DV, (h + 1) * GLA_DV)
        o = ogla_ref[0, :, vc].astype(F32)
        ms = jnp.mean(o * o, axis=-1, keepdims=True)
        y = o * lax.rsqrt(ms + EPS) * gain_ref[:, vc]
        gated.append((y * _silu(gr_ref[0, :, vc].astype(F32))).astype(BF))
    gla = _dot(jnp.concatenate(gated, axis=1), wbg_ref[...])
    g_att = jax.nn.sigmoid(gate_ref[0, :, :D_MODEL].astype(F32))
    g_gla = jax.nn.sigmoid(gate_ref[0, :, D_MODEL:].astype(F32))
    merged = (g_att * att + g_gla * gla).astype(BF)
    gate = mod_ref[0][5:6]
    out_ref[0] = h_ref[0] + (1.0 + gate) * _dot(merged, wout_ref[...])


def _merge_call(h, mod, os_, ls_, ogla, gr, gates, hexp, gain, wba, wbg, wout):
    def row(w):
        return pl.BlockSpec((1, ROW_TILE, w), lambda b, i: (b, i, 0))

    consts = (hexp, gain, wba, wbg, wout)
    dils = [d for _, d in ATT_GROUPS]
    return pl.pallas_call(
        _merge_body,
        grid=(BATCH, SEQ // ROW_TILE),
        in_specs=[row(D_MODEL), pl.BlockSpec((1, N_MOD, D_MODEL), lambda b, i: (b, 0, 0))]
                 + [_dilated_spec(d, ATT_OUT_WIDTH) for d in dils]
                 + [_dilated_spec(d, 128) for d in dils]
                 + [row(GLA_VAL_DIM), row(GLA_VAL_DIM), row(2 * D_MODEL)]
                 + [_const_spec(a.shape) for a in consts],
        out_specs=row(D_MODEL),
        out_shape=jax.ShapeDtypeStruct((BATCH, SEQ, D_MODEL), F32),
        scratch_shapes=[pltpu.VMEM((ATT_OUT_WIDTH // 128, ROW_TILE, 128), F32)] * 2
                       + [pltpu.VMEM((1, ROW_TILE, 128), F32)] * 2,
        compiler_params=_params("parallel", "parallel"),
        name="merge",
    )(h, mod, *os_, *ls_, ogla, gr, gates, *consts)


def _head_matrix(n_lanes_in, width):
    j = np.arange(n_lanes_in)[:, None]
    c = np.arange(width)[None, :]
    return (c // ATT_HEAD_DIM == j).astype(np.float32)


def _layer(h, mod, g_ffn1, f1w1, f1w3, f1w2, g_mix, w_in, q_norm_g, k_norm_g, gate_up, gate_bias,
           out_norm_g, w_branch_att, w_branch_gla, w_out, g_ffn2, f2w1, f2w3, f2w2):
    bf = lambda w: w.astype(BF)
    row = lambda v: v.reshape(1, -1)

    h = _ffn_call(h, mod, row(g_ffn1), bf(f1w1), bf(f1w3), bf(f1w2), 0)

    splits = np.cumsum((ATT_WIDTH, ATT_WIDTH, ATT_WIDTH, GLA_KEY_DIM, GLA_KEY_DIM, GLA_VAL_DIM,
                        GLA_VAL_DIM, GLA_GATE_RANK))
    aq, ak, av, gq, gk, gv, gr, gdn, gates = jnp.split(bf(w_in), [int(s) for s in splits], axis=1)
    W = ATT_OUT_WIDTH
    watt = jnp.concatenate([t[:, gi * W:(gi + 1) * W] for gi in range(len(ATT_GROUPS))
                            for t in (aq, ak, av)], axis=1)
    wgd = jnp.pad(gdn, ((0, 0), (0, GD_PAD - GLA_GATE_RANK)))
    head_qk = _head_matrix(128, QK_W)
    hsum_qk = jnp.asarray(head_qk.T, BF)
    hexp_qk = jnp.asarray(np.concatenate([head_qk, head_qk]), BF)
    qkg = jnp.concatenate([jnp.tile(q_norm_g * (ATT_HEAD_DIM ** -0.5), ATT_HEADS_PER_GROUP),
                           jnp.tile(k_norm_g, ATT_HEADS_PER_GROUP)])
    a0, a1, a2, pgq, pgkt, pgv, pgr, pgd, pgate = _proj_call(
        h, mod, row(g_mix), watt, gq, gk.T, gv, gr, wgd, gates, hsum_qk, hexp_qk, row(qkg))

    outs = [_attn_call(a, gi) for gi, a in enumerate((a0, a1, a2))]
    up = jnp.pad(bf(gate_up), ((0, GD_PAD - GLA_GATE_RANK), (0, 0)))
    ogla = _gla_call(pgq, pgkt, pgv, pgd, up, row(gate_bias))

    head_o = _head_matrix(128, ATT_OUT_WIDTH)
    hexp_o = jnp.asarray(np.concatenate([head_o, head_o]), BF)
    h = _merge_call(h, mod, [o for o, _ in outs], [l for _, l in outs], ogla, pgr, pgate,
                    hexp_o, row(jnp.tile(out_norm_g, GLA_HEADS)), bf(w_branch_att),
                    bf(w_branch_gla), bf(w_out))

    return _ffn_call(h, mod, row(g_ffn2), bf(f2w1), bf(f2w3), bf(f2w2), 6)


def kernel(x, c, w_mod, b_mod, g_ffn1, ffn1_w1, ffn1_w3, ffn1_w2, g_mix, w_in, q_norm_g, k_norm_g,
           gla_gate_up, gla_gate_bias, gla_out_norm_g, w_branch_att, w_branch_gla, w_out,
           g_ffn2, ffn2_w1, ffn2_w3, ffn2_w2):
    h = x
    for l in range(w_mod.shape[0]):
        mod = _mod_call(c, w_mod[l], b_mod[l]).reshape(BATCH, N_MOD, D_MODEL)
        h = _layer(h, mod, g_ffn1[l], ffn1_w1[l], ffn1_w3[l], ffn1_w2[l], g_mix[l], w_in[l],
                   q_norm_g[l], k_norm_g[l], gla_gate_up[l], gla_gate_bias[l], gla_out_norm_g[l],
                   w_branch_att[l], w_branch_gla[l], w_out[l], g_ffn2[l], ffn2_w1[l], ffn2_w3[l],
                   ffn2_w2[l])
    return h
```

```python
import functools

import numpy as np
import jax
import jax.numpy as jnp
from jax import lax
from jax.experimental import pallas as pl
from jax.experimental.pallas import tpu as pltpu

D_MODEL = 1024
BATCH = 16
SEQ = 2048
N_MOD = 9
D_FF = 2816
EPS = 1e-6
ATT_GROUPS = ((128, 1), (512, 4), (2048, 16))
ATT_HEADS_PER_GROUP = 4
ATT_HEADS = 12
ATT_HEAD_DIM = 64
ATT_WIDTH = 768
ATT_OUT_WIDTH = 256
ALIBI_MAX = 8.0
GLA_HEADS = 4
GLA_KEY_DIM = 512
GLA_VAL_DIM = 1024
GLA_DK = 128
GLA_DV = 256
GLA_GATE_RANK = 16
GLA_TAU = 16.0
GLA_CHUNK = 64
ATT_BLK = 128
NEG_BIG = -1e30

BF = jnp.bfloat16
F32 = jnp.float32

VMEM_LIMIT_BYTES = 56 * 1024 * 1024

ROW_TILE = 512
FFN_TILE = 1024
FFN_SUBTILES = 2
MXU_TILE = 256
FF_CHUNK_TILES = 3
GLA_BLOCK = 512


def _dot(a, b):
    return jnp.dot(a, b, preferred_element_type=F32)


def _dot_nt(a, b):
    return lax.dot_general(a, b, (((1,), (1,)), ((), ())), preferred_element_type=F32)


def _dot_tn(a, b):
    return lax.dot_general(a, b, (((0,), (0,)), ((), ())), preferred_element_type=F32)


def _split_bf16(x):
    hi = x.astype(BF)
    lo = (x - hi.astype(F32)).astype(BF)
    return hi, lo


def _dot_split(x, stacked_ref):
    hi, lo = _split_bf16(x)
    return _dot(jnp.concatenate([hi, lo], axis=1), stacked_ref[...])


def _silu(x):
    return x * jax.nn.sigmoid(x)


def _const_spec(shape):
    nd = len(shape)
    return pl.BlockSpec(shape, lambda *_: (0,) * nd, pipeline_mode=pl.Buffered(1))


def _dilated_spec(d, width):
    return pl.BlockSpec((1, d, ROW_TILE // d, width), lambda b, i: (b, 0, i, 0))


def _params(*sem):
    return pltpu.CompilerParams(dimension_semantics=sem, vmem_limit_bytes=VMEM_LIMIT_BYTES)


def _mod_body(c_ref, w_ref, b_ref, o_ref):
    c = c_ref[...]
    o_ref[...] = _dot(_silu(c).astype(BF), w_ref[...].astype(BF)) + b_ref[...]


def _mod_call(c, w_mod, b_mod, layer):
    n = w_mod.shape[2]
    bn = D_MODEL
    return pl.pallas_call(
        _mod_body,
        grid=(n // bn,),
        in_specs=[pl.BlockSpec((BATCH, D_MODEL), lambda j: (0, 0)),
                  pl.BlockSpec((None, D_MODEL, bn), lambda j: (layer, 0, j)),
                  pl.BlockSpec((1, bn), lambda j: (0, j))],
        out_specs=pl.BlockSpec((BATCH, bn), lambda j: (0, j)),
        out_shape=jax.ShapeDtypeStruct((BATCH, n), F32),
        compiler_params=_params("arbitrary"),
        name="mod",
    )(c, w_mod, b_mod.reshape(1, n))


def _shiftproj_body(mod_ref, *refs):
    n = len(refs) // 2
    sh = mod_ref[...].astype(BF)
    for w_ref, o_ref in zip(refs[:n], refs[n:]):
        o_ref[:, 0, :] = _dot(sh, w_ref[...])


def _shiftproj_call(mod, mod_row, weights, name):
    mod2 = mod.reshape(BATCH, N_MOD * D_MODEL)
    return pl.pallas_call(
        _shiftproj_body,
        grid=(1,),
        in_specs=[pl.BlockSpec((BATCH, D_MODEL), lambda i: (0, mod_row))]
                 + [_const_spec(w.shape) for w in weights],
        out_specs=[pl.BlockSpec((BATCH, 1, w.shape[1]), lambda i: (0, 0, 0)) for w in weights],
        out_shape=[jax.ShapeDtypeStruct((BATCH, 1, w.shape[1]), F32) for w in weights],
        compiler_params=_params("arbitrary"),
        name=name,
    )(mod2, *weights)


def _row_scale(x):
    return lax.rsqrt(jnp.mean(x * x, axis=-1, keepdims=True) + EPS)


def _ffn_body(x_ref, mod_ref, g_ref, s1_ref, s3_ref, w1_ref, w3_ref, w2_ref, o_ref, *, mod_row):
    m = mod_ref[0]
    scale, gate = m[mod_row + 1:mod_row + 2], m[mod_row + 2:mod_row + 3]
    fc = FF_CHUNK_TILES * MXU_TILE
    sub = FFN_TILE // FFN_SUBTILES
    rows = [slice(s * sub, (s + 1) * sub) for s in range(FFN_SUBTILES)]
    gs = g_ref[...] * (1.0 + scale)
    us = [(x_ref[0, r, :] * gs).astype(BF) for r in rows]
    rs = [_row_scale(x_ref[0, r, :]) for r in rows]
    accs = [None] * FFN_SUBTILES
    for lo in range(0, D_FF, fc):
        cols = slice(lo, min(lo + fc, D_FF))
        for s, u in enumerate(us):
            h1 = rs[s] * _dot(u, w1_ref[:, cols]) + s1_ref[0, :, cols]
            h3 = rs[s] * _dot(u, w3_ref[:, cols]) + s3_ref[0, :, cols]
            p = _dot((_silu(h1) * h3).astype(BF), w2_ref[cols, :])
            accs[s] = p if accs[s] is None else accs[s] + p
    for r, acc in zip(rows, accs):
        o_ref[0, r, :] = x_ref[0, r, :] + (0.5 * (1.0 + gate)) * acc


def _ffn_call(x, mod, g, w1, w3, w2, mod_row):
    row = pl.BlockSpec((1, FFN_TILE, D_MODEL), lambda b, i: (b, i, 0))
    per_batch = pl.BlockSpec((1, 1, D_FF), lambda b, i: (b, 0, 0))
    s1, s3 = _shiftproj_call(mod, mod_row, (w1, w3), "ffn%d_shift" % mod_row)
    return pl.pallas_call(
        functools.partial(_ffn_body, mod_row=mod_row),
        grid=(BATCH, SEQ // FFN_TILE),
        in_specs=[row,
                  pl.BlockSpec((1, N_MOD, D_MODEL), lambda b, i: (b, 0, 0)),
                  _const_spec((1, D_MODEL)),
                  per_batch, per_batch,
                  _const_spec((D_MODEL, D_FF)),
                  _const_spec((D_MODEL, D_FF)),
                  _const_spec((D_FF, D_MODEL))],
        out_specs=row,
        out_shape=jax.ShapeDtypeStruct((BATCH, SEQ, D_MODEL), F32),
        compiler_params=_params("parallel", "parallel"),
        name="ffn%d" % mod_row,
    )(x, mod, g, s1, s3, w1, w3, w2)


QK_W = 2 * ATT_OUT_WIDTH
GRP_W = 3 * ATT_OUT_WIDTH
GD_PAD = 128


def _proj_body(h_ref, mod_ref, g_ref, satt_ref, sgq_ref, sgv_ref, sgr_ref, sgd_ref, sgate_ref,
               watt_ref, wgq_ref, wgkt_ref, wgv_ref, wgr_ref, wgd_ref,
               wgate_ref, hsum_ref, hexp_ref, qkg_ref,
               a0_ref, a1_ref, a2_ref, gq_ref, gkt_ref, gv_ref, gr_ref, gd_ref, gate_ref, perm_ref):
    m = mod_ref[0]
    x = h_ref[0]
    xg = x * (g_ref[...] * (1.0 + m[4:5]))
    u = xg.astype(BF)
    rs = _row_scale(x)

    def proj(w, s):
        return rs * _dot(u, w) + s

    for gi, a_ref in enumerate((a0_ref, a1_ref, a2_ref)):
        d = ATT_GROUPS[gi][1]
        gcols = slice(gi * GRP_W, (gi + 1) * GRP_W)
        y = proj(watt_ref[:, gcols], satt_ref[0, :, gcols])
        qk = y[:, :QK_W]
        ss = _dot((qk * qk).astype(BF), hsum_ref[...])
        rx = _dot_split(lax.rsqrt(ss * (1.0 / ATT_HEAD_DIM) + EPS), hexp_ref)
        qkn = qk * rx * qkg_ref[...]
        if d == 1:
            a_ref[0, 0, :, :QK_W] = qkn.astype(BF)
            a_ref[0, 0, :, QK_W:] = y[:, QK_W:].astype(BF)
        else:
            for j in range(GRP_W // 128):
                cols = slice(j * 128, (j + 1) * 128)
                perm_ref[j] = qkn[:, cols] if j < QK_W // 128 else y[:, cols]
            for r in range(d):
                for j in range(GRP_W // 128):
                    a_ref[0, r, :, j * 128:(j + 1) * 128] = (
                        perm_ref[j, pl.ds(r, ROW_TILE // d, stride=d), :].astype(BF))
    gq_ref[0] = proj(wgq_ref[...], sgq_ref[0]).astype(BF)
    gv_ref[0] = proj(wgv_ref[...], sgv_ref[0]).astype(BF)
    gr_ref[0] = proj(wgr_ref[...], sgr_ref[0]).astype(BF)
    gd_ref[0] = proj(wgd_ref[...], sgd_ref[0]).astype(BF)
    gate_ref[0] = proj(wgate_ref[...], sgate_ref[0]).astype(BF)
    u_full = (xg * rs + m[3:4]).astype(BF)
    gkt_ref[0] = _dot_nt(wgkt_ref[...], u_full).astype(BF)


def _proj_call(h, mod, g, watt, wgq, wgkt, wgv, wgr, wgd, wgate, hsum, hexp, qkg):
    def row(w):
        return pl.BlockSpec((1, ROW_TILE, w), lambda b, i: (b, i, 0))

    def out(w):
        return jax.ShapeDtypeStruct((BATCH, SEQ, w), BF)

    kt_spec = pl.BlockSpec((1, GLA_KEY_DIM, ROW_TILE), lambda b, i: (b, 0, i))
    kt_shape = jax.ShapeDtypeStruct((BATCH, GLA_KEY_DIM, SEQ), BF)
    widths = (GLA_VAL_DIM, GLA_VAL_DIM, GD_PAD, 2 * D_MODEL)
    row_weights = (watt, wgq, wgv, wgr, wgd, wgate)
    shifts = _shiftproj_call(mod, 3, row_weights, "proj_shift")
    consts = (watt, wgq, wgkt, wgv, wgr, wgd, wgate, hsum, hexp, qkg)
    dils = [d for _, d in ATT_GROUPS]
    return pl.pallas_call(
        _proj_body,
        grid=(BATCH, SEQ // ROW_TILE),
        in_specs=[row(D_MODEL), pl.BlockSpec((1, N_MOD, D_MODEL), lambda b, i: (b, 0, 0)),
                  _const_spec(g.shape)]
                 + [pl.BlockSpec((1, 1, w.shape[1]), lambda b, i: (b, 0, 0)) for w in row_weights]
                 + [_const_spec(a.shape) for a in consts],
        out_specs=[_dilated_spec(d, GRP_W) for d in dils] + [row(GLA_KEY_DIM), kt_spec]
                  + [row(w) for w in widths],
        out_shape=[jax.ShapeDtypeStruct((BATCH, d, SEQ // d, GRP_W), BF) for d in dils]
                  + [out(GLA_KEY_DIM), kt_shape] + [out(w) for w in widths],
        scratch_shapes=[pltpu.VMEM((GRP_W // 128, ROW_TILE, 128), F32)],
        compiler_params=_params("parallel", "parallel"),
        name="proj",
    )(h, mod, g, *shifts, *consts)


ATT_GROUP_BLOCKS = 4


def _attn_blocks(a_ref, bias_ref, o_ref, l_ref, blocks):
    W, NH = ATT_OUT_WIDTH, ATT_HEADS_PER_GROUP
    head_of_lane = lax.broadcasted_iota(jnp.int32, (ATT_BLK, W), 1) // ATT_HEAD_DIM
    lane = lax.broadcasted_iota(jnp.int32, (ATT_BLK, 128), 1)
    qs, ks, vs, biases, q0s = [], [], [], [], []
    for r, n, first in blocks:
        if first:
            q0, k0, nk = 0, 0, ATT_BLK
            biases.append(bias_ref[:, ATT_BLK:])
        else:
            q0, k0, nk = n * ATT_BLK, (n - 1) * ATT_BLK, 2 * ATT_BLK
            if not isinstance(n, int):
                q0, k0 = pl.multiple_of(q0, ATT_BLK), pl.multiple_of(k0, ATT_BLK)
            biases.append(bias_ref[...])
        q0s.append(q0)
        q = a_ref[0, r, pl.ds(q0, ATT_BLK), :W]
        zero = jnp.zeros_like(q)
        qs.append(jnp.concatenate([jnp.where(head_of_lane == j, q, zero) for j in range(NH)],
                                  axis=0))
        ks.append(a_ref[0, r, pl.ds(k0, nk), W:2 * W])
        vs.append(a_ref[0, r, pl.ds(k0, nk), 2 * W:])
    ss = [_dot_nt(q4, k) + b for q4, k, b in zip(qs, ks, biases)]
    mxs = [jnp.max(s, axis=-1, keepdims=True) for s in ss]
    es = [jnp.exp(s - mx) for s, mx in zip(ss, mxs)]
    dens = [jnp.sum(e, axis=-1, keepdims=True) for e in es]
    pvs = [_dot(e.astype(BF), v) for e, v in zip(es, vs)]
    for (r, _, _), q0, pv, mx, den in zip(blocks, q0s, pvs, mxs, dens):
        pv = pv * (1.0 / den)
        lse = mx + jnp.log(den)
        o = jnp.zeros((ATT_BLK, W), F32)
        l = jnp.zeros((ATT_BLK, 128), F32)
        for j in range(NH):
            rows = slice(j * ATT_BLK, (j + 1) * ATT_BLK)
            o = jnp.where(head_of_lane == j, pv[rows], o)
            l = jnp.where(lane == j, lse[rows], l)
        o_ref[0, r, pl.ds(q0, ATT_BLK), :] = o.astype(o_ref.dtype)
        l_ref[0, r, pl.ds(q0, ATT_BLK), :] = l


def _attn_body(a_ref, bias_ref, o_ref, l_ref, *, subseqs, n_blocks):
    G = ATT_GROUP_BLOCKS
    if n_blocks == 1:
        for r0 in range(0, subseqs, G):
            _attn_blocks(a_ref, bias_ref, o_ref, l_ref,
                         [(r, 0, True) for r in range(r0, min(r0 + G, subseqs))])
        return
    for r in range(subseqs):
        _attn_blocks(a_ref, bias_ref, o_ref, l_ref, [(r, n, n == 0) for n in range(G)])
        if n_blocks > G:
            def step(i, carry, r=r):
                _attn_blocks(a_ref, bias_ref, o_ref, l_ref,
                             [(r, i * G + j, False) for j in range(G)])
                return carry
            lax.fori_loop(1, n_blocks // G, step, 0)


def _attn_bias(group):
    _, dilation = ATT_GROUPS[group]
    heads = np.arange(group * ATT_HEADS_PER_GROUP, (group + 1) * ATT_HEADS_PER_GROUP)
    slopes = (2.0 ** (-ALIBI_MAX * (heads + 1).astype(np.float32) / ATT_HEADS)).astype(np.float32)
    qi = np.arange(ATT_BLK)[:, None]
    kj = np.arange(2 * ATT_BLK)[None, :]
    steps = qi + ATT_BLK - kj
    valid = (steps >= 0) & (steps <= ATT_BLK)
    bias = -(slopes * np.float32(dilation))[:, None, None] * steps.astype(np.float32)[None]
    bias = np.where(valid[None], bias, np.float32(NEG_BIG)).astype(np.float32)
    return bias.reshape(ATT_HEADS_PER_GROUP * ATT_BLK, 2 * ATT_BLK)


def _attn_call(a, group):
    _, d = ATT_GROUPS[group]
    L = SEQ // d
    n_blocks = L // ATT_BLK
    subseqs = d if n_blocks == 1 else min(d, 4)

    def spec(w):
        return pl.BlockSpec((1, subseqs, L, w), lambda b, r: (b, r, 0, 0))

    return pl.pallas_call(
        functools.partial(_attn_body, subseqs=subseqs, n_blocks=n_blocks),
        grid=(BATCH, d // subseqs),
        in_specs=[spec(GRP_W), _const_spec((ATT_HEADS_PER_GROUP * ATT_BLK, 2 * ATT_BLK))],
        out_specs=[spec(ATT_OUT_WIDTH), spec(128)],
        out_shape=[jax.ShapeDtypeStruct((BATCH, d, L, ATT_OUT_WIDTH), BF),
                   jax.ShapeDtypeStruct((BATCH, d, L, 128), F32)],
        compiler_params=_params("parallel", "parallel"),
        name="attn%d" % d,
    )(a, jnp.asarray(_attn_bias(group)))


GLA_PAIR = 2 * GLA_CHUNK


def _gla_pair(p, states, gq_ref, gkt_ref, gv_ref, gd_ref, up_ref, gb_ref, tril2_ref, o_ref):
    C, P = GLA_CHUNK, GLA_PAIR
    rows = slice(p * P, (p + 1) * P)
    x = _dot(gd_ref[0, rows, :], up_ref[...]) + gb_ref[...]
    log_a = (jnp.minimum(x, 0.0) - jnp.log1p(jnp.exp(-jnp.abs(x)))) * (1.0 / GLA_TAU)
    la_hi, la_lo = _split_bf16(log_a)
    b = _dot(tril2_ref[...], jnp.concatenate([la_hi, la_lo], axis=0))
    q_dec = (gq_ref[0, rows, :].astype(F32) * (GLA_DK ** -0.5) * jnp.exp(b)).astype(BF)
    b_t = b.T
    k_dec_t = gkt_ref[0, :, rows].astype(F32) * jnp.exp(-b_t)
    decay_t = jnp.exp(b_t)
    lane = lax.broadcasted_iota(jnp.int32, (GLA_KEY_DIM, P), 1)
    k_dec_t_bf = k_dec_t.astype(BF)
    k_dec_t_chunk = [jnp.where(lane < C, k_dec_t, 0.0).astype(BF),
                     jnp.where(lane >= C, k_dec_t, 0.0).astype(BF)]
    ri = lax.broadcasted_iota(jnp.int32, (P, P), 0)
    ci = lax.broadcasted_iota(jnp.int32, (P, P), 1)
    causal = (ri >= ci) & ((ri >= C) == (ci >= C))
    new_states = []
    for h in range(GLA_HEADS):
        kc = slice(h * GLA_DK, (h + 1) * GLA_DK)
        vc = slice(h * GLA_DV, (h + 1) * GLA_DV)
        v = gv_ref[0, rows, vc]
        attn = jnp.where(causal, _dot(q_dec[:, kc], k_dec_t_bf[kc, :]), 0.0).astype(BF)
        stacked = _dot(jnp.concatenate([k_dec_t_chunk[0][kc, :], k_dec_t_chunk[1][kc, :], attn],
                                       axis=0), v)
        st = states[h]
        for cc in range(2):
            crow = slice(cc * C, (cc + 1) * C)
            o = stacked[2 * GLA_DK + cc * C:2 * GLA_DK + (cc + 1) * C] + _dot(q_dec[crow, kc],
                                                                              st.astype(BF))
            o_ref[0, p * P + cc * C:p * P + (cc + 1) * C, vc] = o.astype(o_ref.dtype)
            last = (cc + 1) * C - 1
            st = decay_t[kc, last:last + 1] * (st + stacked[cc * GLA_DK:(cc + 1) * GLA_DK])
        new_states.append(st)
    return new_states


def _gla_body(gq_ref, gkt_ref, gv_ref, gd_ref, up_ref, gb_ref, tril2_ref, o_ref, state_ref):
    @pl.when(pl.program_id(1) == 0)
    def _():
        state_ref[...] = jnp.zeros_like(state_ref)

    states = [state_ref[h] for h in range(GLA_HEADS)]
    for p in range(GLA_BLOCK // GLA_PAIR):
        states = _gla_pair(p, states, gq_ref, gkt_ref, gv_ref, gd_ref, up_ref, gb_ref,
                           tril2_ref, o_ref)
    for h in range(GLA_HEADS):
        state_ref[h] = states[h]


def _pair_tril2():
    i = np.arange(GLA_PAIR)
    t = ((i[:, None] // GLA_CHUNK) == (i[None, :] // GLA_CHUNK)) & (i[:, None] >= i[None, :])
    return np.concatenate([t, t], axis=1).astype(np.float32)


def _gla_call(gq, gkt, gv, gd, up, gbias):
    def row(w):
        return pl.BlockSpec((1, GLA_BLOCK, w), lambda b, i: (b, i, 0))

    consts = (up, gbias, jnp.asarray(_pair_tril2(), BF))
    return pl.pallas_call(
        _gla_body,
        grid=(BATCH, SEQ // GLA_BLOCK),
        in_specs=[row(GLA_KEY_DIM),
                  pl.BlockSpec((1, GLA_KEY_DIM, GLA_BLOCK), lambda b, i: (b, 0, i)),
                  row(GLA_VAL_DIM), row(GD_PAD)]
                 + [_const_spec(a.shape) for a in consts],
        out_specs=row(GLA_VAL_DIM),
        out_shape=jax.ShapeDtypeStruct((BATCH, SEQ, GLA_VAL_DIM), BF),
        scratch_shapes=[pltpu.VMEM((GLA_HEADS, GLA_DK, GLA_DV), F32)],
        compiler_params=_params("parallel", "arbitrary"),
        name="gla",
    )(gq, gkt, gv, gd, *consts)


def _undilate(src_ref, scr_ref, d):
    if d == 1:
        return src_ref[0, 0].astype(F32)
    planes = scr_ref.shape[0]
    for r in range(d):
        x = src_ref[0, r].astype(F32)
        for j in range(planes):
            scr_ref[j, pl.ds(r, ROW_TILE // d, stride=d), :] = x[:, j * 128:(j + 1) * 128]
    return jnp.concatenate([scr_ref[j] for j in range(planes)], axis=-1)


def _merge_body(h_ref, mod_ref, o0_ref, o1_ref, o2_ref, l0_ref, l1_ref, l2_ref,
                ogla_ref, gr_ref, gate_ref, hexp_ref, gain_ref, wba_ref, wbg_ref, wout_ref, out_ref,
                os1_ref, os2_ref, ls1_ref, ls2_ref):
    dils = [d for _, d in ATT_GROUPS]
    lses = [_undilate(r, s, d) for r, s, d in
            zip((l0_ref, l1_ref, l2_ref), (None, ls1_ref, ls2_ref), dils)]
    outs = [_undilate(r, s, d) for r, s, d in
            zip((o0_ref, o1_ref, o2_ref), (None, os1_ref, os2_ref), dils)]
    mx = jnp.maximum(jnp.maximum(lses[0], lses[1]), lses[2])
    es = [jnp.exp(l - mx) for l in lses]
    den = es[0] + es[1] + es[2]
    o_att = None
    for e, o in zip(es, outs):
        t = _dot_split(e / den, hexp_ref) * o
        o_att = t if o_att is None else o_att + t
    att = _dot(o_att.astype(BF), wba_ref[...])
    gated = []
    for h in range(GLA_HEADS):
        vc = slice(h * GLA_DV, (h + 1) * GLA_DV)
        o = ogla_ref[0, :, vc].astype(F32)
        ms = jnp.mean(o * o, axis=-1, keepdims=True)
        y = o * lax.rsqrt(ms + EPS) * gain_ref[:, vc]
        gated.append((y * _silu(gr_ref[0, :, vc].astype(F32))).astype(BF))
    gla = _dot(jnp.concatenate(gated, axis=1), wbg_ref[...])
    g_att = jax.nn.sigmoid(gate_ref[0, :, :D_MODEL].astype(F32))
    g_gla = jax.nn.sigmoid(gate_ref[0, :, D_MODEL:].astype(F32))
    merged = (g_att * att + g_gla * gla).astype(BF)
    gate = mod_ref[0][5:6]
    out_ref[0] = h_ref[0] + (1.0 + gate) * _dot(merged, wout_ref[...])


def _merge_call(h, mod, os_, ls_, ogla, gr, gates, hexp, gain, wba, wbg, wout):
    def row(w):
        return pl.BlockSpec((1, ROW_TILE, w), lambda b, i: (b, i, 0))

    consts = (hexp, gain, wba, wbg, wout)
    dils = [d for _, d in ATT_GROUPS]
    return pl.pallas_call(
        _merge_body,
        grid=(BATCH, SEQ // ROW_TILE),
        in_specs=[row(D_MODEL), pl.BlockSpec((1, N_MOD, D_MODEL), lambda b, i: (b, 0, 0))]
                 + [_dilated_spec(d, ATT_OUT_WIDTH) for d in dils]
                 + [_dilated_spec(d, 128) for d in dils]
                 + [row(GLA_VAL_DIM), row(GLA_VAL_DIM), row(2 * D_MODEL)]
                 + [_const_spec(a.shape) for a in consts],
        out_specs=row(D_MODEL),
        out_shape=jax.ShapeDtypeStruct((BATCH, SEQ, D_MODEL), F32),
        scratch_shapes=[pltpu.VMEM((ATT_OUT_WIDTH // 128, ROW_TILE, 128), F32)] * 2
                       + [pltpu.VMEM((1, ROW_TILE, 128), F32)] * 2,
        compiler_params=_params("parallel", "parallel"),
        name="merge",
    )(h, mod, *os_, *ls_, ogla, gr, gates, *consts)


def _head_matrix(n_lanes_in, width):
    j = np.arange(n_lanes_in)[:, None]
    c = np.arange(width)[None, :]
    return (c // ATT_HEAD_DIM == j).astype(np.float32)


def _layer(h, mod, g_ffn1, f1w1, f1w3, f1w2, g_mix, w_in, q_norm_g, k_norm_g, gate_up, gate_bias,
           out_norm_g, w_branch_att, w_branch_gla, w_out, g_ffn2, f2w1, f2w3, f2w2):
    bf = lambda w: w.astype(BF)
    row = lambda v: v.reshape(1, -1)

    h = _ffn_call(h, mod, row(g_ffn1), bf(f1w1), bf(f1w3), bf(f1w2), 0)

    splits = np.cumsum((ATT_WIDTH, ATT_WIDTH, ATT_WIDTH, GLA_KEY_DIM, GLA_KEY_DIM, GLA_VAL_DIM,
                        GLA_VAL_DIM, GLA_GATE_RANK))
    aq, ak, av, gq, gk, gv, gr, gdn, gates = jnp.split(bf(w_in), [int(s) for s in splits], axis=1)
    W = ATT_OUT_WIDTH
    watt = jnp.concatenate([t[:, gi * W:(gi + 1) * W] for gi in range(len(ATT_GROUPS))
                            for t in (aq, ak, av)], axis=1)
    wgd = jnp.pad(gdn, ((0, 0), (0, GD_PAD - GLA_GATE_RANK)))
    head_qk = _head_matrix(128, QK_W)
    hsum_qk = jnp.asarray(head_qk.T, BF)
    hexp_qk = jnp.asarray(np.concatenate([head_qk, head_qk]), BF)
    qkg = jnp.concatenate([jnp.tile(q_norm_g * (ATT_HEAD_DIM ** -0.5), ATT_HEADS_PER_GROUP),
                           jnp.tile(k_norm_g, ATT_HEADS_PER_GROUP)])
    a0, a1, a2, pgq, pgkt, pgv, pgr, pgd, pgate = _proj_call(
        h, mod, row(g_mix), watt, gq, gk.T, gv, gr, wgd, gates, hsum_qk, hexp_qk, row(qkg))

    outs = [_attn_call(a, gi) for gi, a in enumerate((a0, a1, a2))]
    up = jnp.pad(bf(gate_up), ((0, GD_PAD - GLA_GATE_RANK), (0, 0)))
    ogla = _gla_call(pgq, pgkt, pgv, pgd, up, row(gate_bias))

    head_o = _head_matrix(128, ATT_OUT_WIDTH)
    hexp_o = jnp.asarray(np.concatenate([head_o, head_o]), BF)
    h = _merge_call(h, mod, [o for o, _ in outs], [l for _, l in outs], ogla, pgr, pgate,
                    hexp_o, row(jnp.tile(out_norm_g, GLA_HEADS)), bf(w_branch_att),
                    bf(w_branch_gla), bf(w_out))

    return _ffn_call(h, mod, row(g_ffn2), bf(f2w1), bf(f2w3), bf(f2w2), 6)


def kernel(x, c, w_mod, b_mod, g_ffn1, ffn1_w1, ffn1_w3, ffn1_w2, g_mix, w_in, q_norm_g, k_norm_g,
           gla_gate_up, gla_gate_bias, gla_out_norm_g, w_branch_att, w_branch_gla, w_out,
           g_ffn2, ffn2_w1, ffn2_w3, ffn2_w2):
    h = x
    for l in range(w_mod.shape[0]):
        mod = _mod_call(c, w_mod, b_mod[l], l).reshape(BATCH, N_MOD, D_MODEL)
        h = _layer(h, mod, g_ffn1[l], ffn1_w1[l], ffn1_w3[l], ffn1_w2[l], g_mix[l], w_in[l],
                   q_norm_g[l], k_norm_g[l], gla_gate_up[l], gla_gate_bias[l], gla_out_norm_g[l],
                   w_branch_att[l], w_branch_gla[l], w_out[l], g_ffn2[l], ffn2_w1[l], ffn2_w3[l],
                   ffn2_w2[l])
    return h
```

```python
import functools

import numpy as np
import jax
import jax.numpy as jnp
from jax import lax
from jax.experimental import pallas as pl
from jax.experimental.pallas import tpu as pltpu

D_MODEL = 1024
BATCH = 16
SEQ = 2048
N_MOD = 9
D_FF = 2816
EPS = 1e-6
ATT_GROUPS = ((128, 1), (512, 4), (2048, 16))
ATT_HEADS_PER_GROUP = 4
ATT_HEADS = 12
ATT_HEAD_DIM = 64
ATT_WIDTH = 768
ATT_OUT_WIDTH = 256
ALIBI_MAX = 8.0
GLA_HEADS = 4
GLA_KEY_DIM = 512
GLA_VAL_DIM = 1024
GLA_DK = 128
GLA_DV = 256
GLA_GATE_RANK = 16
GLA_TAU = 16.0
GLA_CHUNK = 64
ATT_BLK = 128
NEG_BIG = -1e30

BF = jnp.bfloat16
F32 = jnp.float32

VMEM_LIMIT_BYTES = 56 * 1024 * 1024

ROW_TILE = 512
FFN_TILE = 1024
FFN_SUBTILES = 2
MXU_TILE = 256
FF_CHUNK_TILES = 3
GLA_BLOCK = 512


def _dot(a, b):
    return jnp.dot(a, b, preferred_element_type=F32)


def _dot_nt(a, b):
    return lax.dot_general(a, b, (((1,), (1,)), ((), ())), preferred_element_type=F32)


def _dot_tn(a, b):
    return lax.dot_general(a, b, (((0,), (0,)), ((), ())), preferred_element_type=F32)


def _split_bf16(x):
    hi = x.astype(BF)
    lo = (x - hi.astype(F32)).astype(BF)
    return hi, lo


def _dot_split(x, stacked_ref):
    hi, lo = _split_bf16(x)
    return _dot(jnp.concatenate([hi, lo], axis=1), stacked_ref[...])


def _sigmoid(x):
    return 0.5 * jnp.tanh(0.5 * x) + 0.5


def _silu(x):
    return x * _sigmoid(x)


def _const_spec(shape):
    nd = len(shape)
    return pl.BlockSpec(shape, lambda *_: (0,) * nd, pipeline_mode=pl.Buffered(1))


def _dilated_spec(d, width):
    return pl.BlockSpec((1, d, ROW_TILE // d, width), lambda b, i: (b, 0, i, 0))


def _params(*sem):
    return pltpu.CompilerParams(dimension_semantics=sem, vmem_limit_bytes=VMEM_LIMIT_BYTES)


def _mod_body(c_ref, w_ref, b_ref, o_ref):
    c = c_ref[...]
    o_ref[...] = _dot(_silu(c).astype(BF), w_ref[...].astype(BF)) + b_ref[...]


def _mod_call(c, w_mod, b_mod, layer):
    n = w_mod.shape[2]
    bn = D_MODEL
    return pl.pallas_call(
        _mod_body,
        grid=(n // bn,),
        in_specs=[pl.BlockSpec((BATCH, D_MODEL), lambda j: (0, 0)),
                  pl.BlockSpec((None, D_MODEL, bn), lambda j: (layer, 0, j)),
                  pl.BlockSpec((1, bn), lambda j: (0, j))],
        out_specs=pl.BlockSpec((BATCH, bn), lambda j: (0, j)),
        out_shape=jax.ShapeDtypeStruct((BATCH, n), F32),
        compiler_params=_params("arbitrary"),
        name="mod",
    )(c, w_mod, b_mod.reshape(1, n))


def _shiftproj_body(mod_ref, *refs):
    n = len(refs) // 2
    sh = mod_ref[...].astype(BF)
    for w_ref, o_ref in zip(refs[:n], refs[n:]):
        o_ref[:, 0, :] = _dot(sh, w_ref[...])


def _shiftproj_call(mod, mod_row, weights, name):
    mod2 = mod.reshape(BATCH, N_MOD * D_MODEL)
    return pl.pallas_call(
        _shiftproj_body,
        grid=(1,),
        in_specs=[pl.BlockSpec((BATCH, D_MODEL), lambda i: (0, mod_row))]
                 + [_const_spec(w.shape) for w in weights],
        out_specs=[pl.BlockSpec((BATCH, 1, w.shape[1]), lambda i: (0, 0, 0)) for w in weights],
        out_shape=[jax.ShapeDtypeStruct((BATCH, 1, w.shape[1]), F32) for w in weights],
        compiler_params=_params("arbitrary"),
        name=name,
    )(mod2, *weights)


def _row_scale(x):
    return lax.rsqrt(jnp.mean(x * x, axis=-1, keepdims=True) + EPS)


def _ffn_body(x_ref, mod_ref, g_ref, s1_ref, s3_ref, w1_ref, w3_ref, w2_ref, o_ref, *, mod_row):
    m = mod_ref[0]
    scale, gate = m[mod_row + 1:mod_row + 2], m[mod_row + 2:mod_row + 3]
    fc = FF_CHUNK_TILES * MXU_TILE
    sub = FFN_TILE // FFN_SUBTILES
    rows = [slice(s * sub, (s + 1) * sub) for s in range(FFN_SUBTILES)]
    gs = g_ref[...] * (1.0 + scale)
    us = [(x_ref[0, r, :] * gs).astype(BF) for r in rows]
    rs = [_row_scale(x_ref[0, r, :]) for r in rows]
    accs = [None] * FFN_SUBTILES
    for lo in range(0, D_FF, fc):
        cols = slice(lo, min(lo + fc, D_FF))
        for s, u in enumerate(us):
            h1 = rs[s] * _dot(u, w1_ref[:, cols]) + s1_ref[0, :, cols]
            h3 = rs[s] * _dot(u, w3_ref[:, cols]) + s3_ref[0, :, cols]
            p = _dot((_silu(h1) * h3).astype(BF), w2_ref[cols, :])
            accs[s] = p if accs[s] is None else accs[s] + p
    for r, acc in zip(rows, accs):
        o_ref[0, r, :] = x_ref[0, r, :] + (0.5 * (1.0 + gate)) * acc


def _ffn_call(x, mod, g, w1, w3, w2, mod_row):
    row = pl.BlockSpec((1, FFN_TILE, D_MODEL), lambda b, i: (b, i, 0))
    per_batch = pl.BlockSpec((1, 1, D_FF), lambda b, i: (b, 0, 0))
    s1, s3 = _shiftproj_call(mod, mod_row, (w1, w3), "ffn%d_shift" % mod_row)
    return pl.pallas_call(
        functools.partial(_ffn_body, mod_row=mod_row),
        grid=(BATCH, SEQ // FFN_TILE),
        in_specs=[row,
                  pl.BlockSpec((1, N_MOD, D_MODEL), lambda b, i: (b, 0, 0)),
                  _const_spec((1, D_MODEL)),
                  per_batch, per_batch,
                  _const_spec((D_MODEL, D_FF)),
                  _const_spec((D_MODEL, D_FF)),
                  _const_spec((D_FF, D_MODEL))],
        out_specs=row,
        out_shape=jax.ShapeDtypeStruct((BATCH, SEQ, D_MODEL), F32),
        compiler_params=_params("parallel", "parallel"),
        name="ffn%d" % mod_row,
    )(x, mod, g, s1, s3, w1, w3, w2)


QK_W = 2 * ATT_OUT_WIDTH
GRP_W = 3 * ATT_OUT_WIDTH
GD_PAD = 128


def _proj_body(h_ref, mod_ref, g_ref, satt_ref, sgq_ref, sgv_ref, sgr_ref, sgd_ref, sgate_ref,
               watt_ref, wgq_ref, wgkt_ref, wgv_ref, wgr_ref, wgd_ref,
               wgate_ref, hsum_ref, hexp_ref, qkg_ref,
               a0_ref, a1_ref, a2_ref, gq_ref, gkt_ref, gv_ref, gr_ref, gd_ref, gate_ref, perm_ref):
    m = mod_ref[0]
    x = h_ref[0]
    xg = x * (g_ref[...] * (1.0 + m[4:5]))
    u = xg.astype(BF)
    rs = _row_scale(x)

    def proj(w, s):
        return rs * _dot(u, w) + s

    for gi, a_ref in enumerate((a0_ref, a1_ref, a2_ref)):
        d = ATT_GROUPS[gi][1]
        gcols = slice(gi * GRP_W, (gi + 1) * GRP_W)
        y = proj(watt_ref[:, gcols], satt_ref[0, :, gcols])
        qk = y[:, :QK_W]
        ss = _dot((qk * qk).astype(BF), hsum_ref[...])
        rx = _dot_split(lax.rsqrt(ss * (1.0 / ATT_HEAD_DIM) + EPS), hexp_ref)
        qkn = qk * rx * qkg_ref[...]
        if d == 1:
            a_ref[0, 0, :, :QK_W] = qkn.astype(BF)
            a_ref[0, 0, :, QK_W:] = y[:, QK_W:].astype(BF)
        else:
            for j in range(GRP_W // 128):
                cols = slice(j * 128, (j + 1) * 128)
                perm_ref[j] = qkn[:, cols] if j < QK_W // 128 else y[:, cols]
            for r in range(d):
                for j in range(GRP_W // 128):
                    a_ref[0, r, :, j * 128:(j + 1) * 128] = (
                        perm_ref[j, pl.ds(r, ROW_TILE // d, stride=d), :].astype(BF))
    gq_ref[0] = proj(wgq_ref[...], sgq_ref[0]).astype(BF)
    gv_ref[0] = proj(wgv_ref[...], sgv_ref[0]).astype(BF)
    gr_ref[0] = proj(wgr_ref[...], sgr_ref[0]).astype(BF)
    gd_ref[0] = proj(wgd_ref[...], sgd_ref[0]).astype(BF)
    gate_ref[0] = proj(wgate_ref[...], sgate_ref[0]).astype(BF)
    u_full = (xg * rs + m[3:4]).astype(BF)
    gkt_ref[0] = _dot_nt(wgkt_ref[...], u_full).astype(BF)


def _proj_call(h, mod, g, watt, wgq, wgkt, wgv, wgr, wgd, wgate, hsum, hexp, qkg):
    def row(w):
        return pl.BlockSpec((1, ROW_TILE, w), lambda b, i: (b, i, 0))

    def out(w):
        return jax.ShapeDtypeStruct((BATCH, SEQ, w), BF)

    kt_spec = pl.BlockSpec((1, GLA_KEY_DIM, ROW_TILE), lambda b, i: (b, 0, i))
    kt_shape = jax.ShapeDtypeStruct((BATCH, GLA_KEY_DIM, SEQ), BF)
    widths = (GLA_VAL_DIM, GLA_VAL_DIM, GD_PAD, 2 * D_MODEL)
    row_weights = (watt, wgq, wgv, wgr, wgd, wgate)
    shifts = _shiftproj_call(mod, 3, row_weights, "proj_shift")
    consts = (watt, wgq, wgkt, wgv, wgr, wgd, wgate, hsum, hexp, qkg)
    dils = [d for _, d in ATT_GROUPS]
    return pl.pallas_call(
        _proj_body,
        grid=(BATCH, SEQ // ROW_TILE),
        in_specs=[row(D_MODEL), pl.BlockSpec((1, N_MOD, D_MODEL), lambda b, i: (b, 0, 0)),
                  _const_spec(g.shape)]
                 + [pl.BlockSpec((1, 1, w.shape[1]), lambda b, i: (b, 0, 0)) for w in row_weights]
                 + [_const_spec(a.shape) for a in consts],
        out_specs=[_dilated_spec(d, GRP_W) for d in dils] + [row(GLA_KEY_DIM), kt_spec]
                  + [row(w) for w in widths],
        out_shape=[jax.ShapeDtypeStruct((BATCH, d, SEQ // d, GRP_W), BF) for d in dils]
                  + [out(GLA_KEY_DIM), kt_shape] + [out(w) for w in widths],
        scratch_shapes=[pltpu.VMEM((GRP_W // 128, ROW_TILE, 128), F32)],
        compiler_params=_params("parallel", "parallel"),
        name="proj",
    )(h, mod, g, *shifts, *consts)


ATT_GROUP_BLOCKS = 4


def _attn_blocks(a_ref, bias_ref, o_ref, l_ref, blocks):
    W, NH = ATT_OUT_WIDTH, ATT_HEADS_PER_GROUP
    head_of_lane = lax.broadcasted_iota(jnp.int32, (ATT_BLK, W), 1) // ATT_HEAD_DIM
    lane = lax.broadcasted_iota(jnp.int32, (ATT_BLK, 128), 1)
    qs, ks, vs, biases, q0s = [], [], [], [], []
    for r, n, first in blocks:
        if first:
            q0, k0, nk = 0, 0, ATT_BLK
            biases.append(bias_ref[:, ATT_BLK:])
        else:
            q0, k0, nk = n * ATT_BLK, (n - 1) * ATT_BLK, 2 * ATT_BLK
            if not isinstance(n, int):
                q0, k0 = pl.multiple_of(q0, ATT_BLK), pl.multiple_of(k0, ATT_BLK)
            biases.append(bias_ref[...])
        q0s.append(q0)
        q = a_ref[0, r, pl.ds(q0, ATT_BLK), :W]
        zero = jnp.zeros_like(q)
        qs.append(jnp.concatenate([jnp.where(head_of_lane == j, q, zero) for j in range(NH)],
                                  axis=0))
        ks.append(a_ref[0, r, pl.ds(k0, nk), W:2 * W])
        vs.append(a_ref[0, r, pl.ds(k0, nk), 2 * W:])
    ss = [_dot_nt(q4, k) + b for q4, k, b in zip(qs, ks, biases)]
    mxs = [jnp.max(s, axis=-1, keepdims=True) for s in ss]
    es = [jnp.exp(s - mx) for s, mx in zip(ss, mxs)]
    dens = [jnp.sum(e, axis=-1, keepdims=True) for e in es]
    pvs = [_dot(e.astype(BF), v) for e, v in zip(es, vs)]
    for (r, _, _), q0, pv, mx, den in zip(blocks, q0s, pvs, mxs, dens):
        pv = pv * (1.0 / den)
        lse = mx + jnp.log(den)
        o = jnp.zeros((ATT_BLK, W), F32)
        l = jnp.zeros((ATT_BLK, 128), F32)
        for j in range(NH):
            rows = slice(j * ATT_BLK, (j + 1) * ATT_BLK)
            o = jnp.where(head_of_lane == j, pv[rows], o)
            l = jnp.where(lane == j, lse[rows], l)
        o_ref[0, r, pl.ds(q0, ATT_BLK), :] = o.astype(o_ref.dtype)
        l_ref[0, r, pl.ds(q0, ATT_BLK), :] = l


def _attn_body(a_ref, bias_ref, o_ref, l_ref, *, subseqs, n_blocks):
    G = ATT_GROUP_BLOCKS
    if n_blocks == 1:
        for r0 in range(0, subseqs, G):
            _attn_blocks(a_ref, bias_ref, o_ref, l_ref,
                         [(r, 0, True) for r in range(r0, min(r0 + G, subseqs))])
        return
    for r in range(subseqs):
        for n0 in range(0, n_blocks, G):
            _attn_blocks(a_ref, bias_ref, o_ref, l_ref,
                         [(r, n, n == 0) for n in range(n0, n0 + G)])


def _attn_bias(group):
    _, dilation = ATT_GROUPS[group]
    heads = np.arange(group * ATT_HEADS_PER_GROUP, (group + 1) * ATT_HEADS_PER_GROUP)
    slopes = (2.0 ** (-ALIBI_MAX * (heads + 1).astype(np.float32) / ATT_HEADS)).astype(np.float32)
    qi = np.arange(ATT_BLK)[:, None]
    kj = np.arange(2 * ATT_BLK)[None, :]
    steps = qi + ATT_BLK - kj
    valid = (steps >= 0) & (steps <= ATT_BLK)
    bias = -(slopes * np.float32(dilation))[:, None, None] * steps.astype(np.float32)[None]
    bias = np.where(valid[None], bias, np.float32(NEG_BIG)).astype(np.float32)
    return bias.reshape(ATT_HEADS_PER_GROUP * ATT_BLK, 2 * ATT_BLK)


def _attn_call(a, group):
    _, d = ATT_GROUPS[group]
    L = SEQ // d
    n_blocks = L // ATT_BLK
    subseqs = d if n_blocks == 1 else min(d, 4)

    def spec(w):
        return pl.BlockSpec((1, subseqs, L, w), lambda b, r: (b, r, 0, 0))

    return pl.pallas_call(
        functools.partial(_attn_body, subseqs=subseqs, n_blocks=n_blocks),
        grid=(BATCH, d // subseqs),
        in_specs=[spec(GRP_W), _const_spec((ATT_HEADS_PER_GROUP * ATT_BLK, 2 * ATT_BLK))],
        out_specs=[spec(ATT_OUT_WIDTH), spec(128)],
        out_shape=[jax.ShapeDtypeStruct((BATCH, d, L, ATT_OUT_WIDTH), BF),
                   jax.ShapeDtypeStruct((BATCH, d, L, 128), F32)],
        compiler_params=_params("parallel", "parallel"),
        name="attn%d" % d,
    )(a, jnp.asarray(_attn_bias(group)))


GLA_PAIR = 2 * GLA_CHUNK


def _gla_pair(p, states, gq_ref, gkt_ref, gv_ref, gd_ref, up_ref, gb_ref, tril2_ref, o_ref):
    C, P = GLA_CHUNK, GLA_PAIR
    rows = slice(p * P, (p + 1) * P)
    x = _dot(gd_ref[0, rows, :], up_ref[...]) + gb_ref[...]
    log_a = (jnp.minimum(x, 0.0) - jnp.log1p(jnp.exp(-jnp.abs(x)))) * (1.0 / GLA_TAU)
    la_hi, la_lo = _split_bf16(log_a)
    b = _dot(tril2_ref[...], jnp.concatenate([la_hi, la_lo], axis=0))
    q_dec = (gq_ref[0, rows, :].astype(F32) * (GLA_DK ** -0.5) * jnp.exp(b)).astype(BF)
    b_t = b.T
    k_dec_t = gkt_ref[0, :, rows].astype(F32) * jnp.exp(-b_t)
    decay_t = jnp.exp(b_t)
    lane = lax.broadcasted_iota(jnp.int32, (GLA_KEY_DIM, P), 1)
    k_dec_t_bf = k_dec_t.astype(BF)
    k_dec_t_chunk = [jnp.where(lane < C, k_dec_t, 0.0).astype(BF),
                     jnp.where(lane >= C, k_dec_t, 0.0).astype(BF)]
    ri = lax.broadcasted_iota(jnp.int32, (P, P), 0)
    ci = lax.broadcasted_iota(jnp.int32, (P, P), 1)
    causal = (ri >= ci) & ((ri >= C) == (ci >= C))
    new_states = []
    for h in range(GLA_HEADS):
        kc = slice(h * GLA_DK, (h + 1) * GLA_DK)
        vc = slice(h * GLA_DV, (h + 1) * GLA_DV)
        v = gv_ref[0, rows, vc]
        attn = jnp.where(causal, _dot(q_dec[:, kc], k_dec_t_bf[kc, :]), 0.0).astype(BF)
        stacked = _dot(jnp.concatenate([k_dec_t_chunk[0][kc, :], k_dec_t_chunk[1][kc, :], attn],
                                       axis=0), v)
        st = states[h]
        for cc in range(2):
            crow = slice(cc * C, (cc + 1) * C)
            o = stacked[2 * GLA_DK + cc * C:2 * GLA_DK + (cc + 1) * C] + _dot(q_dec[crow, kc],
                                                                              st.astype(BF))
            o_ref[0, p * P + cc * C:p * P + (cc + 1) * C, vc] = o.astype(o_ref.dtype)
            last = (cc + 1) * C - 1
            st = decay_t[kc, last:last + 1] * (st + stacked[cc * GLA_DK:(cc + 1) * GLA_DK])
        new_states.append(st)
    return new_states


def _gla_body(gq_ref, gkt_ref, gv_ref, gd_ref, up_ref, gb_ref, tril2_ref, o_ref, state_ref):
    @pl.when(pl.program_id(1) == 0)
    def _():
        state_ref[...] = jnp.zeros_like(state_ref)

    states = [state_ref[h] for h in range(GLA_HEADS)]
    for p in range(GLA_BLOCK // GLA_PAIR):
        states = _gla_pair(p, states, gq_ref, gkt_ref, gv_ref, gd_ref, up_ref, gb_ref,
                           tril2_ref, o_ref)
    for h in range(GLA_HEADS):
        state_ref[h] = states[h]


def _pair_tril2():
    i = np.arange(GLA_PAIR)
    t = ((i[:, None] // GLA_CHUNK) == (i[None, :] // GLA_CHUNK)) & (i[:, None] >= i[None, :])
    return np.concatenate([t, t], axis=1).astype(np.float32)


def _gla_call(gq, gkt, gv, gd, up, gbias):
    def row(w):
        return pl.BlockSpec((1, GLA_BLOCK, w), lambda b, i: (b, i, 0))

    consts = (up, gbias, jnp.asarray(_pair_tril2(), BF))
    return pl.pallas_call(
        _gla_body,
        grid=(BATCH, SEQ // GLA_BLOCK),
        in_specs=[row(GLA_KEY_DIM),
                  pl.BlockSpec((1, GLA_KEY_DIM, GLA_BLOCK), lambda b, i: (b, 0, i)),
                  row(GLA_VAL_DIM), row(GD_PAD)]
                 + [_const_spec(a.shape) for a in consts],
        out_specs=row(GLA_VAL_DIM),
        out_shape=jax.ShapeDtypeStruct((BATCH, SEQ, GLA_VAL_DIM), BF),
        scratch_shapes=[pltpu.VMEM((GLA_HEADS, GLA_DK, GLA_DV), F32)],
        compiler_params=_params("parallel", "arbitrary"),
        name="gla",
    )(gq, gkt, gv, gd, *consts)


def _undilate(src_ref, scr_ref, d):
    if d == 1:
        return src_ref[0, 0].astype(F32)
    planes = scr_ref.shape[0]
    for r in range(d):
        x = src_ref[0, r].astype(F32)
        for j in range(planes):
            scr_ref[j, pl.ds(r, ROW_TILE // d, stride=d), :] = x[:, j * 128:(j + 1) * 128]
    return jnp.concatenate([scr_ref[j] for j in range(planes)], axis=-1)


def _merge_body(h_ref, mod_ref, o0_ref, o1_ref, o2_ref, l0_ref, l1_ref, l2_ref,
                ogla_ref, gr_ref, gate_ref, hexp_ref, gain_ref, wba_ref, wbg_ref, wout_ref, out_ref,
                os1_ref, os2_ref, ls1_ref, ls2_ref):
    dils = [d for _, d in ATT_GROUPS]
    lses = [_undilate(r, s, d) for r, s, d in
            zip((l0_ref, l1_ref, l2_ref), (None, ls1_ref, ls2_ref), dils)]
    outs = [_undilate(r, s, d) for r, s, d in
            zip((o0_ref, o1_ref, o2_ref), (None, os1_ref, os2_ref), dils)]
    mx = jnp.maximum(jnp.maximum(lses[0], lses[1]), lses[2])
    es = [jnp.exp(l - mx) for l in lses]
    den = es[0] + es[1] + es[2]
    o_att = None
    for e, o in zip(es, outs):
        t = _dot_split(e / den, hexp_ref) * o
        o_att = t if o_att is None else o_att + t
    att = _dot(o_att.astype(BF), wba_ref[...])
    gated = []
    for h in range(GLA_HEADS):
        vc = slice(h * GLA_DV, (h + 1) * GLA_DV)
        o = ogla_ref[0, :, vc].astype(F32)
        ms = jnp.mean(o * o, axis=-1, keepdims=True)
        y = o * lax.rsqrt(ms + EPS) * gain_ref[:, vc]
        gated.append(y.astype(BF) * _silu(gr_ref[0, :, vc]))
    gla = _dot(jnp.concatenate(gated, axis=1), wbg_ref[...])
    merged = (_sigmoid(gate_ref[0, :, :D_MODEL]) * att.astype(BF)
              + _sigmoid(gate_ref[0, :, D_MODEL:]) * gla.astype(BF))
    gate = mod_ref[0][5:6]
    out_ref[0] = h_ref[0] + (1.0 + gate) * _dot(merged, wout_ref[...])


def _merge_call(h, mod, os_, ls_, ogla, gr, gates, hexp, gain, wba, wbg, wout):
    def row(w):
        return pl.BlockSpec((1, ROW_TILE, w), lambda b, i: (b, i, 0))

    consts = (hexp, gain, wba, wbg, wout)
    dils = [d for _, d in ATT_GROUPS]
    return pl.pallas_call(
        _merge_body,
        grid=(BATCH, SEQ // ROW_TILE),
        in_specs=[row(D_MODEL), pl.BlockSpec((1, N_MOD, D_MODEL), lambda b, i: (b, 0, 0))]
                 + [_dilated_spec(d, ATT_OUT_WIDTH) for d in dils]
                 + [_dilated_spec(d, 128) for d in dils]
                 + [row(GLA_VAL_DIM), row(GLA_VAL_DIM), row(2 * D_MODEL)]
                 + [_const_spec(a.shape) for a in consts],
        out_specs=row(D_MODEL),
        out_shape=jax.ShapeDtypeStruct((BATCH, SEQ, D_MODEL), F32),
        scratch_shapes=[pltpu.VMEM((ATT_OUT_WIDTH // 128, ROW_TILE, 128), F32)] * 2
                       + [pltpu.VMEM((1, ROW_TILE, 128), F32)] * 2,
        compiler_params=_params("parallel", "parallel"),
        name="merge",
    )(h, mod, *os_, *ls_, ogla, gr, gates, *consts)


def _head_matrix(n_lanes_in, width):
    j = np.arange(n_lanes_in)[:, None]
    c = np.arange(width)[None, :]
    return (c // ATT_HEAD_DIM == j).astype(np.float32)


def _layer(h, mod, g_ffn1, f1w1, f1w3, f1w2, g_mix, w_in, q_norm_g, k_norm_g, gate_up, gate_bias,
           out_norm_g, w_branch_att, w_branch_gla, w_out, g_ffn2, f2w1, f2w3, f2w2):
    bf = lambda w: w.astype(BF)
    row = lambda v: v.reshape(1, -1)

    h = _ffn_call(h, mod, row(g_ffn1), bf(f1w1), bf(f1w3), bf(f1w2), 0)

    splits = np.cumsum((ATT_WIDTH, ATT_WIDTH, ATT_WIDTH, GLA_KEY_DIM, GLA_KEY_DIM, GLA_VAL_DIM,
                        GLA_VAL_DIM, GLA_GATE_RANK))
    aq, ak, av, gq, gk, gv, gr, gdn, gates = jnp.split(bf(w_in), [int(s) for s in splits], axis=1)
    W = ATT_OUT_WIDTH
    watt = jnp.concatenate([t[:, gi * W:(gi + 1) * W] for gi in range(len(ATT_GROUPS))
                            for t in (aq, ak, av)], axis=1)
    wgd = jnp.pad(gdn, ((0, 0), (0, GD_PAD - GLA_GATE_RANK)))
    head_qk = _head_matrix(128, QK_W)
    hsum_qk = jnp.asarray(head_qk.T, BF)
    hexp_qk = jnp.asarray(np.concatenate([head_qk, head_qk]), BF)
    qkg = jnp.concatenate([jnp.tile(q_norm_g * (ATT_HEAD_DIM ** -0.5), ATT_HEADS_PER_GROUP),
                           jnp.tile(k_norm_g, ATT_HEADS_PER_GROUP)])
    a0, a1, a2, pgq, pgkt, pgv, pgr, pgd, pgate = _proj_call(
        h, mod, row(g_mix), watt, gq, gk.T, gv, gr, wgd, gates, hsum_qk, hexp_qk, row(qkg))

    outs = [_attn_call(a, gi) for gi, a in enumerate((a0, a1, a2))]
    up = jnp.pad(bf(gate_up), ((0, GD_PAD - GLA_GATE_RANK), (0, 0)))
    ogla = _gla_call(pgq, pgkt, pgv, pgd, up, row(gate_bias))

    head_o = _head_matrix(128, ATT_OUT_WIDTH)
    hexp_o = jnp.asarray(np.concatenate([head_o, head_o]), BF)
    h = _merge_call(h, mod, [o for o, _ in outs], [l for _, l in outs], ogla, pgr, pgate,
                    hexp_o, row(jnp.tile(out_norm_g, GLA_HEADS)), bf(w_branch_att),
                    bf(w_branch_gla), bf(w_out))

    return _ffn_call(h, mod, row(g_ffn2), bf(f2w1), bf(f2w3), bf(f2w2), 6)


def kernel(x, c, w_mod, b_mod, g_ffn1, ffn1_w1, ffn1_w3, ffn1_w2, g_mix, w_in, q_norm_g, k_norm_g,
           gla_gate_up, gla_gate_bias, gla_out_norm_g, w_branch_att, w_branch_gla, w_out,
           g_ffn2, ffn2_w1, ffn2_w3, ffn2_w2):
    h = x
    for l in range(w_mod.shape[0]):
        mod = _mod_call(c, w_mod, b_mod[l], l).reshape(BATCH, N_MOD, D_MODEL)
        h = _layer(h, mod, g_ffn1[l], ffn1_w1[l], ffn1_w3[l], ffn1_w2[l], g_mix[l], w_in[l],
                   q_norm_g[l], k_norm_g[l], gla_gate_up[l], gla_gate_bias[l], gla_out_norm_g[l],
                   w_branch_att[l], w_branch_gla[l], w_out[l], g_ffn2[l], ffn2_w1[l], ffn2_w3[l],
                   ffn2_w2[l])
    return h
```

```python
import functools

import numpy as np
import jax
import jax.numpy as jnp
from jax import lax
from jax.experimental import pallas as pl
from jax.experimental.pallas import tpu as pltpu

D_MODEL = 1024
BATCH = 16
SEQ = 2048
N_MOD = 9
D_FF = 2816
EPS = 1e-6
ATT_GROUPS = ((128, 1), (512, 4), (2048, 16))
ATT_HEADS_PER_GROUP = 4
ATT_HEADS = 12
ATT_HEAD_DIM = 64
ATT_WIDTH = 768
ATT_OUT_WIDTH = 256
ALIBI_MAX = 8.0
GLA_HEADS = 4
GLA_KEY_DIM = 512
GLA_VAL_DIM = 1024
GLA_DK = 128
GLA_DV = 256
GLA_GATE_RANK = 16
GLA_TAU = 16.0
GLA_CHUNK = 64
ATT_BLK = 128
NEG_BIG = -1e30

BF = jnp.bfloat16
F32 = jnp.float32

VMEM_LIMIT_BYTES = 56 * 1024 * 1024

ROW_TILE = 512
FFN_TILE = 1024
FFN_SUBTILES = 2
MXU_TILE = 256
FF_CHUNK_TILES = 3
GLA_BLOCK = 512


def _dot(a, b):
    return jnp.dot(a, b, preferred_element_type=F32)


def _dot_nt(a, b):
    return lax.dot_general(a, b, (((1,), (1,)), ((), ())), preferred_element_type=F32)


def _dot_tn(a, b):
    return lax.dot_general(a, b, (((0,), (0,)), ((), ())), preferred_element_type=F32)


def _split_bf16(x):
    hi = x.astype(BF)
    lo = (x - hi.astype(F32)).astype(BF)
    return hi, lo


def _dot_split(x, stacked_ref):
    hi, lo = _split_bf16(x)
    return _dot(jnp.concatenate([hi, lo], axis=1), stacked_ref[...])


def _sigmoid(x):
    return 0.5 * jnp.tanh(0.5 * x) + 0.5


def _silu(x):
    return x * _sigmoid(x)


def _const_spec(shape):
    nd = len(shape)
    return pl.BlockSpec(shape, lambda *_: (0,) * nd, pipeline_mode=pl.Buffered(1))


def _dilated_spec(d, width):
    return pl.BlockSpec((1, d, ROW_TILE // d, width), lambda b, i: (b, 0, i, 0))


def _params(*sem):
    return pltpu.CompilerParams(dimension_semantics=sem, vmem_limit_bytes=VMEM_LIMIT_BYTES)


def _mod_body(c_ref, w_ref, b_ref, o_ref):
    c = c_ref[...]
    o_ref[...] = _dot(_silu(c).astype(BF), w_ref[...].astype(BF)) + b_ref[...]


def _mod_call(c, w_mod, b_mod, layer):
    n = w_mod.shape[2]
    bn = D_MODEL
    return pl.pallas_call(
        _mod_body,
        grid=(n // bn,),
        in_specs=[pl.BlockSpec((BATCH, D_MODEL), lambda j: (0, 0)),
                  pl.BlockSpec((None, D_MODEL, bn), lambda j: (layer, 0, j)),
                  pl.BlockSpec((1, bn), lambda j: (0, j))],
        out_specs=pl.BlockSpec((BATCH, bn), lambda j: (0, j)),
        out_shape=jax.ShapeDtypeStruct((BATCH, n), F32),
        compiler_params=_params("arbitrary"),
        name="mod",
    )(c, w_mod, b_mod.reshape(1, n))


def _shiftproj_body(mod_ref, *refs):
    n = len(refs) // 2
    sh = mod_ref[...].astype(BF)
    for w_ref, o_ref in zip(refs[:n], refs[n:]):
        o_ref[:, 0, :] = _dot(sh, w_ref[...])


def _shiftproj_call(mod, mod_row, weights, name):
    mod2 = mod.reshape(BATCH, N_MOD * D_MODEL)
    return pl.pallas_call(
        _shiftproj_body,
        grid=(1,),
        in_specs=[pl.BlockSpec((BATCH, D_MODEL), lambda i: (0, mod_row))]
                 + [_const_spec(w.shape) for w in weights],
        out_specs=[pl.BlockSpec((BATCH, 1, w.shape[1]), lambda i: (0, 0, 0)) for w in weights],
        out_shape=[jax.ShapeDtypeStruct((BATCH, 1, w.shape[1]), F32) for w in weights],
        compiler_params=_params("arbitrary"),
        name=name,
    )(mod2, *weights)


def _row_scale(x):
    return lax.rsqrt(jnp.mean(x * x, axis=-1, keepdims=True) + EPS)


def _ffn_body(x_ref, mod_ref, g_ref, s1_ref, s3_ref, w1_ref, w3_ref, w2_ref, o_ref, *, mod_row):
    m = mod_ref[0]
    scale, gate = m[mod_row + 1:mod_row + 2], m[mod_row + 2:mod_row + 3]
    fc = FF_CHUNK_TILES * MXU_TILE
    sub = FFN_TILE // FFN_SUBTILES
    rows = [slice(s * sub, (s + 1) * sub) for s in range(FFN_SUBTILES)]
    gs = g_ref[...] * (1.0 + scale)
    us = [(x_ref[0, r, :] * gs).astype(BF) for r in rows]
    rs = [_row_scale(x_ref[0, r, :]) for r in rows]
    accs = [None] * FFN_SUBTILES
    for lo in range(0, D_FF, fc):
        cols = slice(lo, min(lo + fc, D_FF))
        for s, u in enumerate(us):
            h1 = rs[s] * _dot(u, w1_ref[:, cols]) + s1_ref[0, :, cols]
            h3 = rs[s] * _dot(u, w3_ref[:, cols]) + s3_ref[0, :, cols]
            p = _dot((_silu(h1) * h3).astype(BF), w2_ref[cols, :])
            accs[s] = p if accs[s] is None else accs[s] + p
    for r, acc in zip(rows, accs):
        o_ref[0, r, :] = x_ref[0, r, :] + (0.5 * (1.0 + gate)) * acc


def _ffn_call(x, mod, g, w1, w3, w2, mod_row):
    row = pl.BlockSpec((1, FFN_TILE, D_MODEL), lambda b, i: (b, i, 0))
    per_batch = pl.BlockSpec((1, 1, D_FF), lambda b, i: (b, 0, 0))
    s1, s3 = _shiftproj_call(mod, mod_row, (w1, w3), "ffn%d_shift" % mod_row)
    return pl.pallas_call(
        functools.partial(_ffn_body, mod_row=mod_row),
        grid=(BATCH, SEQ // FFN_TILE),
        in_specs=[row,
                  pl.BlockSpec((1, N_MOD, D_MODEL), lambda b, i: (b, 0, 0)),
                  _const_spec((1, D_MODEL)),
                  per_batch, per_batch,
                  _const_spec((D_MODEL, D_FF)),
                  _const_spec((D_MODEL, D_FF)),
                  _const_spec((D_FF, D_MODEL))],
        out_specs=row,
        out_shape=jax.ShapeDtypeStruct((BATCH, SEQ, D_MODEL), F32),
        compiler_params=_params("parallel", "parallel"),
        name="ffn%d" % mod_row,
    )(x, mod, g, s1, s3, w1, w3, w2)


QK_W = 2 * ATT_OUT_WIDTH
GRP_W = 3 * ATT_OUT_WIDTH
GD_PAD = 128


def _proj_body(h_ref, mod_ref, g_ref, satt_ref, sgq_ref, sgv_ref, sgr_ref, sgd_ref, sgate_ref,
               watt_ref, wgq_ref, wgkt_ref, wgv_ref, wgr_ref, wgd_ref,
               wgate_ref, hsum_ref, hexp_ref, qkg_ref,
               a0_ref, a1_ref, a2_ref, gq_ref, gkt_ref, gv_ref, gr_ref, gd_ref, gate_ref, perm_ref):
    m = mod_ref[0]
    x = h_ref[0]
    xg = x * (g_ref[...] * (1.0 + m[4:5]))
    u = xg.astype(BF)
    rs = _row_scale(x)

    def proj(w, s):
        return rs * _dot(u, w) + s

    for gi, a_ref in enumerate((a0_ref, a1_ref, a2_ref)):
        d = ATT_GROUPS[gi][1]
        gcols = slice(gi * GRP_W, (gi + 1) * GRP_W)
        y = proj(watt_ref[:, gcols], satt_ref[0, :, gcols])
        qk = y[:, :QK_W]
        ss = _dot((qk * qk).astype(BF), hsum_ref[...])
        rx = _dot_split(lax.rsqrt(ss * (1.0 / ATT_HEAD_DIM) + EPS), hexp_ref)
        qkn = qk * rx * qkg_ref[...]
        if d == 1:
            a_ref[0, 0, :, :QK_W] = qkn.astype(BF)
            a_ref[0, 0, :, QK_W:] = y[:, QK_W:].astype(BF)
        else:
            for j in range(GRP_W // 128):
                cols = slice(j * 128, (j + 1) * 128)
                perm_ref[j] = qkn[:, cols] if j < QK_W // 128 else y[:, cols]
            for r in range(d):
                for j in range(GRP_W // 128):
                    a_ref[0, r, :, j * 128:(j + 1) * 128] = (
                        perm_ref[j, pl.ds(r, ROW_TILE // d, stride=d), :].astype(BF))
    gq_ref[0] = proj(wgq_ref[...], sgq_ref[0]).astype(BF)
    gv_ref[0] = proj(wgv_ref[...], sgv_ref[0]).astype(BF)
    gr_ref[0] = proj(wgr_ref[...], sgr_ref[0]).astype(BF)
    gd_ref[0] = proj(wgd_ref[...], sgd_ref[0]).astype(BF)
    gate_ref[0] = proj(wgate_ref[...], sgate_ref[0]).astype(BF)
    u_full = (xg * rs + m[3:4]).astype(BF)
    gkt_ref[0] = _dot_nt(wgkt_ref[...], u_full).astype(BF)


def _proj_call(h, mod, g, watt, wgq, wgkt, wgv, wgr, wgd, wgate, hsum, hexp, qkg):
    def row(w):
        return pl.BlockSpec((1, ROW_TILE, w), lambda b, i: (b, i, 0))

    def out(w):
        return jax.ShapeDtypeStruct((BATCH, SEQ, w), BF)

    kt_spec = pl.BlockSpec((1, GLA_KEY_DIM, ROW_TILE), lambda b, i: (b, 0, i))
    kt_shape = jax.ShapeDtypeStruct((BATCH, GLA_KEY_DIM, SEQ), BF)
    widths = (GLA_VAL_DIM, GLA_VAL_DIM, GD_PAD, 2 * D_MODEL)
    row_weights = (watt, wgq, wgv, wgr, wgd, wgate)
    shifts = _shiftproj_call(mod, 3, row_weights, "proj_shift")
    consts = (watt, wgq, wgkt, wgv, wgr, wgd, wgate, hsum, hexp, qkg)
    dils = [d for _, d in ATT_GROUPS]
    return pl.pallas_call(
        _proj_body,
        grid=(BATCH, SEQ // ROW_TILE),
        in_specs=[row(D_MODEL), pl.BlockSpec((1, N_MOD, D_MODEL), lambda b, i: (b, 0, 0)),
                  _const_spec(g.shape)]
                 + [pl.BlockSpec((1, 1, w.shape[1]), lambda b, i: (b, 0, 0)) for w in row_weights]
                 + [_const_spec(a.shape) for a in consts],
        out_specs=[_dilated_spec(d, GRP_W) for d in dils] + [row(GLA_KEY_DIM), kt_spec]
                  + [row(w) for w in widths],
        out_shape=[jax.ShapeDtypeStruct((BATCH, d, SEQ // d, GRP_W), BF) for d in dils]
                  + [out(GLA_KEY_DIM), kt_shape] + [out(w) for w in widths],
        scratch_shapes=[pltpu.VMEM((GRP_W // 128, ROW_TILE, 128), F32)],
        compiler_params=_params("parallel", "parallel"),
        name="proj",
    )(h, mod, g, *shifts, *consts)


ATT_GROUP_BLOCKS = 4


def _attn_blocks(a_ref, bias_ref, o_ref, l_ref, blocks):
    W, NH = ATT_OUT_WIDTH, ATT_HEADS_PER_GROUP
    head_of_lane = lax.broadcasted_iota(jnp.int32, (ATT_BLK, W), 1) // ATT_HEAD_DIM
    lane = lax.broadcasted_iota(jnp.int32, (ATT_BLK, 128), 1)
    qs, ks, vs, biases, q0s = [], [], [], [], []
    for r, n, first in blocks:
        if first:
            q0, k0, nk = 0, 0, ATT_BLK
            biases.append(bias_ref[:, ATT_BLK:])
        else:
            q0, k0, nk = n * ATT_BLK, (n - 1) * ATT_BLK, 2 * ATT_BLK
            if not isinstance(n, int):
                q0, k0 = pl.multiple_of(q0, ATT_BLK), pl.multiple_of(k0, ATT_BLK)
            biases.append(bias_ref[...])
        q0s.append(q0)
        q = a_ref[0, r, pl.ds(q0, ATT_BLK), :W]
        zero = jnp.zeros_like(q)
        qs.append(jnp.concatenate([jnp.where(head_of_lane == j, q, zero) for j in range(NH)],
                                  axis=0))
        ks.append(a_ref[0, r, pl.ds(k0, nk), W:2 * W])
        vs.append(a_ref[0, r, pl.ds(k0, nk), 2 * W:])
    ss = [_dot_nt(q4, k) + b for q4, k, b in zip(qs, ks, biases)]
    mxs = [jnp.max(s, axis=-1, keepdims=True) for s in ss]
    es = [jnp.exp(s - mx) for s, mx in zip(ss, mxs)]
    dens = [jnp.sum(e, axis=-1, keepdims=True) for e in es]
    pvs = [_dot(e.astype(BF), v) for e, v in zip(es, vs)]
    for (r, _, _), q0, pv, mx, den in zip(blocks, q0s, pvs, mxs, dens):
        pv = pv * (1.0 / den)
        lse = mx + jnp.log(den)
        o = jnp.zeros((ATT_BLK, W), F32)
        l = jnp.zeros((ATT_BLK, 128), F32)
        for j in range(NH):
            rows = slice(j * ATT_BLK, (j + 1) * ATT_BLK)
            o = jnp.where(head_of_lane == j, pv[rows], o)
            l = jnp.where(lane == j, lse[rows], l)
        o_ref[0, r, pl.ds(q0, ATT_BLK), :] = o.astype(o_ref.dtype)
        l_ref[0, r, pl.ds(q0, ATT_BLK), :] = l


def _attn_body(a_ref, bias_ref, o_ref, l_ref, *, subseqs, n_blocks):
    G = ATT_GROUP_BLOCKS
    if n_blocks == 1:
        for r0 in range(0, subseqs, G):
            _attn_blocks(a_ref, bias_ref, o_ref, l_ref,
                         [(r, 0, True) for r in range(r0, min(r0 + G, subseqs))])
        return
    for r in range(subseqs):
        for n0 in range(0, n_blocks, G):
            _attn_blocks(a_ref, bias_ref, o_ref, l_ref,
                         [(r, n, n == 0) for n in range(n0, n0 + G)])


def _attn_bias(group):
    _, dilation = ATT_GROUPS[group]
    heads = np.arange(group * ATT_HEADS_PER_GROUP, (group + 1) * ATT_HEADS_PER_GROUP)
    slopes = (2.0 ** (-ALIBI_MAX * (heads + 1).astype(np.float32) / ATT_HEADS)).astype(np.float32)
    qi = np.arange(ATT_BLK)[:, None]
    kj = np.arange(2 * ATT_BLK)[None, :]
    steps = qi + ATT_BLK - kj
    valid = (steps >= 0) & (steps <= ATT_BLK)
    bias = -(slopes * np.float32(dilation))[:, None, None] * steps.astype(np.float32)[None]
    bias = np.where(valid[None], bias, np.float32(NEG_BIG)).astype(np.float32)
    return bias.reshape(ATT_HEADS_PER_GROUP * ATT_BLK, 2 * ATT_BLK)


def _attn_call(a, group):
    _, d = ATT_GROUPS[group]
    L = SEQ // d
    n_blocks = L // ATT_BLK
    subseqs = d if n_blocks == 1 else min(d, 4)

    def spec(w):
        return pl.BlockSpec((1, subseqs, L, w), lambda b, r: (b, r, 0, 0))

    return pl.pallas_call(
        functools.partial(_attn_body, subseqs=subseqs, n_blocks=n_blocks),
        grid=(BATCH, d // subseqs),
        in_specs=[spec(GRP_W), _const_spec((ATT_HEADS_PER_GROUP * ATT_BLK, 2 * ATT_BLK))],
        out_specs=[spec(ATT_OUT_WIDTH), spec(128)],
        out_shape=[jax.ShapeDtypeStruct((BATCH, d, L, ATT_OUT_WIDTH), BF),
                   jax.ShapeDtypeStruct((BATCH, d, L, 128), F32)],
        compiler_params=_params("parallel", "parallel"),
        name="attn%d" % d,
    )(a, jnp.asarray(_attn_bias(group)))


GLA_PAIR = 2 * GLA_CHUNK


def _gla_cum_decay(gd_ref, up_ref, gb_ref, tril2_ref):
    x = _dot(gd_ref[0], up_ref[...]) + gb_ref[...]
    log_a = (jnp.minimum(x, 0.0) - jnp.log(1.0 + jnp.exp(-jnp.abs(x)))) * (1.0 / GLA_TAU)
    la_hi, la_lo = _split_bf16(log_a)
    pieces = []
    for p in range(GLA_BLOCK // GLA_PAIR):
        rows = slice(p * GLA_PAIR, (p + 1) * GLA_PAIR)
        pieces.append(_dot(tril2_ref[...], jnp.concatenate([la_hi[rows], la_lo[rows]], axis=0)))
    return pieces


def _gla_pair(p, b, states, gq_ref, gkt_ref, gv_ref, o_ref):
    C, P = GLA_CHUNK, GLA_PAIR
    rows = slice(p * P, (p + 1) * P)
    q_dec =(gq_ref[0, rows, :].astype(F32) * (GLA_DK ** -0.5) * jnp.exp(b)).astype(BF)
    b_t = b.T
    k_dec_t_bf = (gkt_ref[0, :, rows].astype(F32) * jnp.exp(-b_t)).astype(BF)
    decay_t = jnp.exp(b_t)
    lane = lax.broadcasted_iota(jnp.int32, (GLA_KEY_DIM, P), 1)
    zero = jnp.zeros_like(k_dec_t_bf)
    k_dec_t_chunk = [jnp.where(lane < C, k_dec_t_bf, zero), jnp.where(lane >= C, k_dec_t_bf, zero)]
    ri = lax.broadcasted_iota(jnp.int32, (P, P), 0)
    ci = lax.broadcasted_iota(jnp.int32, (P, P), 1)
    causal = (ri >= ci) & ((ri >= C) == (ci >= C))
    new_states = []
    for h in range(GLA_HEADS):
        kc = slice(h * GLA_DK, (h + 1) * GLA_DK)
        vc = slice(h * GLA_DV, (h + 1) * GLA_DV)
        v = gv_ref[0, rows, vc]
        attn = jnp.where(causal, _dot(q_dec[:, kc], k_dec_t_bf[kc, :]), 0.0).astype(BF)
        stacked = _dot(jnp.concatenate([k_dec_t_chunk[0][kc, :], k_dec_t_chunk[1][kc, :], attn],
                                       axis=0), v)
        st = states[h]
        for cc in range(2):
            crow = slice(cc * C, (cc + 1) * C)
            o = stacked[2 * GLA_DK + cc * C:2 * GLA_DK + (cc + 1) * C] + _dot(q_dec[crow, kc],
                                                                              st.astype(BF))
            o_ref[0, p * P + cc * C:p * P + (cc + 1) * C, vc] = o.astype(o_ref.dtype)
            last = (cc + 1) * C - 1
            st = decay_t[kc, last:last + 1] * (st + stacked[cc * GLA_DK:(cc + 1) * GLA_DK])
        new_states.append(st)
    return new_states


def _gla_body(gq_ref, gkt_ref, gv_ref, gd_ref, up_ref, gb_ref, tril2_ref, o_ref, state_ref):
    @pl.when(pl.program_id(1) == 0)
    def _():
        state_ref[...] = jnp.zeros_like(state_ref)

    bs = _gla_cum_decay(gd_ref, up_ref, gb_ref, tril2_ref)
    states = [state_ref[h] for h in range(GLA_HEADS)]
    for p in range(GLA_BLOCK // GLA_PAIR):
        states = _gla_pair(p, bs[p], states, gq_ref, gkt_ref, gv_ref, o_ref)
    for h in range(GLA_HEADS):
        state_ref[h] = states[h]


def _pair_tril2():
    i = np.arange(GLA_PAIR)
    t = ((i[:, None] // GLA_CHUNK) == (i[None, :] // GLA_CHUNK)) & (i[:, None] >= i[None, :])
    return np.concatenate([t, t], axis=1).astype(np.float32)


def _gla_call(gq, gkt, gv, gd, up, gbias):
    def row(w):
        return pl.BlockSpec((1, GLA_BLOCK, w), lambda b, i: (b, i, 0))

    consts = (up, gbias, jnp.asarray(_pair_tril2(), BF))
    return pl.pallas_call(
        _gla_body,
        grid=(BATCH, SEQ // GLA_BLOCK),
        in_specs=[row(GLA_KEY_DIM),
                  pl.BlockSpec((1, GLA_KEY_DIM, GLA_BLOCK), lambda b, i: (b, 0, i)),
                  row(GLA_VAL_DIM), row(GD_PAD)]
                 + [_const_spec(a.shape) for a in consts],
        out_specs=row(GLA_VAL_DIM),
        out_shape=jax.ShapeDtypeStruct((BATCH, SEQ, GLA_VAL_DIM), BF),
        scratch_shapes=[pltpu.VMEM((GLA_HEADS, GLA_DK, GLA_DV), F32)],
        compiler_params=_params("parallel", "arbitrary"),
        name="gla",
    )(gq, gkt, gv, gd, *consts)


def _undilate(src_ref, scr_ref, d):
    if d == 1:
        return src_ref[0, 0].astype(F32)
    planes = scr_ref.shape[0]
    for r in range(d):
        x = src_ref[0, r].astype(F32)
        for j in range(planes):
            scr_ref[j, pl.ds(r, ROW_TILE // d, stride=d), :] = x[:, j * 128:(j + 1) * 128]
    return jnp.concatenate([scr_ref[j] for j in range(planes)], axis=-1)


def _merge_body(h_ref, mod_ref, o0_ref, o1_ref, o2_ref, l0_ref, l1_ref, l2_ref,
                ogla_ref, gr_ref, gate_ref, hexp_ref, gain_ref, wba_ref, wbg_ref, wout_ref, out_ref,
                os1_ref, os2_ref, ls1_ref, ls2_ref):
    dils = [d for _, d in ATT_GROUPS]
    lses = [_undilate(r, s, d) for r, s, d in
            zip((l0_ref, l1_ref, l2_ref), (None, ls1_ref, ls2_ref), dils)]
    outs = [_undilate(r, s, d) for r, s, d in
            zip((o0_ref, o1_ref, o2_ref), (None, os1_ref, os2_ref), dils)]
    mx = jnp.maximum(jnp.maximum(lses[0], lses[1]), lses[2])
    es = [jnp.exp(l - mx) for l in lses]
    den = es[0] + es[1] + es[2]
    o_att = None
    for e, o in zip(es, outs):
        t = _dot_split(e / den, hexp_ref) * o
        o_att = t if o_att is None else o_att + t
    att = _dot(o_att.astype(BF), wba_ref[...])
    gated = []
    for h in range(GLA_HEADS):
        vc = slice(h * GLA_DV, (h + 1) * GLA_DV)
        o = ogla_ref[0, :, vc].astype(F32)
        ms = jnp.mean(o * o, axis=-1, keepdims=True)
        y = o * lax.rsqrt(ms + EPS) * gain_ref[:, vc]
        gated.append(y.astype(BF) * _silu(gr_ref[0, :, vc]))
    gla = _dot(jnp.concatenate(gated, axis=1), wbg_ref[...])
    merged = (_sigmoid(gate_ref[0, :, :D_MODEL]) * att.astype(BF)
              + _sigmoid(gate_ref[0, :, D_MODEL:]) * gla.astype(BF))
    gate = mod_ref[0][5:6]
    out_ref[0] = h_ref[0] + (1.0 + gate) * _dot(merged, wout_ref[...])


def _merge_call(h, mod, os_, ls_, ogla, gr, gates, hexp, gain, wba, wbg, wout):
    def row(w):
        return pl.BlockSpec((1, ROW_TILE, w), lambda b, i: (b, i, 0))

    consts = (hexp, gain, wba, wbg, wout)
    dils = [d for _, d in ATT_GROUPS]
    return pl.pallas_call(
        _merge_body,
        grid=(BATCH, SEQ // ROW_TILE),
        in_specs=[row(D_MODEL), pl.BlockSpec((1, N_MOD, D_MODEL), lambda b, i: (b, 0, 0))]
                 + [_dilated_spec(d, ATT_OUT_WIDTH) for d in dils]
                 + [_dilated_spec(d, 128) for d in dils]
                 + [row(GLA_VAL_DIM), row(GLA_VAL_DIM), row(2 * D_MODEL)]
                 + [_const_spec(a.shape) for a in consts],
        out_specs=row(D_MODEL),
        out_shape=jax.ShapeDtypeStruct((BATCH, SEQ, D_MODEL), F32),
        scratch_shapes=[pltpu.VMEM((ATT_OUT_WIDTH // 128, ROW_TILE, 128), F32)] * 2
                       + [pltpu.VMEM((1, ROW_TILE, 128), F32)] * 2,
        compiler_params=_params("parallel", "parallel"),
        name="merge",
    )(h, mod, *os_, *ls_, ogla, gr, gates, *consts)


def _head_matrix(n_lanes_in, width):
    j = np.arange(n_lanes_in)[:, None]
    c = np.arange(width)[None, :]
    return (c // ATT_HEAD_DIM == j).astype(np.float32)


def _layer(h, mod, g_ffn1, f1w1, f1w3, f1w2, g_mix, w_in, q_norm_g, k_norm_g, gate_up, gate_bias,
           out_norm_g, w_branch_att, w_branch_gla, w_out, g_ffn2, f2w1, f2w3, f2w2):
    bf = lambda w: w.astype(BF)
    row = lambda v: v.reshape(1, -1)

    h = _ffn_call(h, mod, row(g_ffn1), bf(f1w1), bf(f1w3), bf(f1w2), 0)

    splits = np.cumsum((ATT_WIDTH, ATT_WIDTH, ATT_WIDTH, GLA_KEY_DIM, GLA_KEY_DIM, GLA_VAL_DIM,
                        GLA_VAL_DIM, GLA_GATE_RANK))
    aq, ak, av, gq, gk, gv, gr, gdn, gates = jnp.split(bf(w_in), [int(s) for s in splits], axis=1)
    W = ATT_OUT_WIDTH
    watt = jnp.concatenate([t[:, gi * W:(gi + 1) * W] for gi in range(len(ATT_GROUPS))
                            for t in (aq, ak, av)], axis=1)
    wgd = jnp.pad(gdn, ((0, 0), (0, GD_PAD - GLA_GATE_RANK)))
    head_qk = _head_matrix(128, QK_W)
    hsum_qk = jnp.asarray(head_qk.T, BF)
    hexp_qk = jnp.asarray(np.concatenate([head_qk, head_qk]), BF)
    qkg = jnp.concatenate([jnp.tile(q_norm_g * (ATT_HEAD_DIM ** -0.5), ATT_HEADS_PER_GROUP),
                           jnp.tile(k_norm_g, ATT_HEADS_PER_GROUP)])
    a0, a1, a2, pgq, pgkt, pgv, pgr, pgd, pgate = _proj_call(
        h, mod, row(g_mix), watt, gq, gk.T, gv, gr, wgd, gates, hsum_qk, hexp_qk, row(qkg))

    outs = [_attn_call(a, gi) for gi, a in enumerate((a0, a1, a2))]
    up = jnp.pad(bf(gate_up), ((0, GD_PAD - GLA_GATE_RANK), (0, 0)))
    ogla = _gla_call(pgq, pgkt, pgv, pgd, up, row(gate_bias))

    head_o = _head_matrix(128, ATT_OUT_WIDTH)
    hexp_o = jnp.asarray(np.concatenate([head_o, head_o]), BF)
    h = _merge_call(h, mod, [o for o, _ in outs], [l for _, l in outs], ogla, pgr, pgate,
                    hexp_o, row(jnp.tile(out_norm_g, GLA_HEADS)), bf(w_branch_att),
                    bf(w_branch_gla), bf(w_out))

    return _ffn_call(h, mod, row(g_ffn2), bf(f2w1), bf(f2w3), bf(f2w2), 6)


def kernel(x, c, w_mod, b_mod, g_ffn1, ffn1_w1, ffn1_w3, ffn1_w2, g_mix, w_in, q_norm_g, k_norm_g,
           gla_gate_up, gla_gate_bias, gla_out_norm_g, w_branch_att, w_branch_gla, w_out,
           g_ffn2, ffn2_w1, ffn2_w3, ffn2_w2):
    h = x
    for l in range(w_mod.shape[0]):
        mod = _mod_call(c, w_mod, b_mod[l], l).reshape(BATCH, N_MOD, D_MODEL)
        h = _layer(h, mod, g_ffn1[l], ffn1_w1[l], ffn1_w3[l], ffn1_w2[l], g_mix[l], w_in[l],
                   q_norm_g[l], k_norm_g[l], gla_gate_up[l], gla_gate_bias[l], gla_out_norm_g[l],
                   w_branch_att[l], w_branch_gla[l], w_out[l], g_ffn2[l], ffn2_w1[l], ffn2_w3[l],
                   ffn2_w2[l])
    return h
```

```python
import functools

import numpy as np
import jax
import jax.numpy as jnp
from jax import lax
from jax.experimental import pallas as pl
from jax.experimental.pallas import tpu as pltpu

D_MODEL = 1024
BATCH = 16
SEQ = 2048
N_MOD = 9
D_FF = 2816
EPS = 1e-6
ATT_GROUPS = ((128, 1), (512, 4), (2048, 16))
ATT_HEADS_PER_GROUP = 4
ATT_HEADS = 12
ATT_HEAD_DIM = 64
ATT_WIDTH = 768
ATT_OUT_WIDTH = 256
ALIBI_MAX = 8.0
GLA_HEADS = 4
GLA_KEY_DIM = 512
GLA_VAL_DIM = 1024
GLA_DK = 128
GLA_DV = 256
GLA_GATE_RANK = 16
GLA_TAU = 16.0
GLA_CHUNK = 64
ATT_BLK = 128
NEG_BIG = -1e30

BF = jnp.bfloat16
F32 = jnp.float32

VMEM_LIMIT_BYTES = 56 * 1024 * 1024

ROW_TILE = 512
FFN_TILE = 1024
FFN_SUBTILES = 2
MXU_TILE = 256
FF_CHUNK_TILES = 3
GLA_BLOCK = 512


def _dot(a, b):
    return jnp.dot(a, b, preferred_element_type=F32)


def _dot_nt(a, b):
    return lax.dot_general(a, b, (((1,), (1,)), ((), ())), preferred_element_type=F32)


def _dot_tn(a, b):
    return lax.dot_general(a, b, (((0,), (0,)), ((), ())), preferred_element_type=F32)


def _split_bf16(x):
    hi = x.astype(BF)
    lo = (x - hi.astype(F32)).astype(BF)
    return hi, lo


def _dot_split(x, stacked_ref):
    hi, lo = _split_bf16(x)
    return _dot(jnp.concatenate([hi, lo], axis=1), stacked_ref[...])


def _sigmoid(x):
    return 0.5 * jnp.tanh(0.5 * x) + 0.5


def _silu(x):
    return x * _sigmoid(x)


def _const_spec(shape):
    nd = len(shape)
    return pl.BlockSpec(shape, lambda *_: (0,) * nd, pipeline_mode=pl.Buffered(1))


def _dilated_spec(d, width):
    return pl.BlockSpec((1, d, ROW_TILE // d, width), lambda b, i: (b, 0, i, 0))


def _params(*sem):
    return pltpu.CompilerParams(dimension_semantics=sem, vmem_limit_bytes=VMEM_LIMIT_BYTES)


def _mod_body(c_ref, w_ref, b_ref, o_ref):
    c = c_ref[...]
    o_ref[...] = _dot(_silu(c).astype(BF), w_ref[...].astype(BF)) + b_ref[...]


def _mod_call(c, w_mod, b_mod, layer):
    n = w_mod.shape[2]
    bn = D_MODEL
    return pl.pallas_call(
        _mod_body,
        grid=(n // bn,),
        in_specs=[pl.BlockSpec((BATCH, D_MODEL), lambda j: (0, 0)),
                  pl.BlockSpec((None, D_MODEL, bn), lambda j: (layer, 0, j)),
                  pl.BlockSpec((1, bn), lambda j: (0, j))],
        out_specs=pl.BlockSpec((BATCH, bn), lambda j: (0, j)),
        out_shape=jax.ShapeDtypeStruct((BATCH, n), F32),
        compiler_params=_params("arbitrary"),
        name="mod",
    )(c, w_mod, b_mod.reshape(1, n))


def _shiftproj_body(mod_ref, *refs):
    n = len(refs) // 2
    sh = mod_ref[...].astype(BF)
    for w_ref, o_ref in zip(refs[:n], refs[n:]):
        o_ref[:, 0, :] = _dot(sh, w_ref[...])


def _shiftproj_call(mod, mod_row, weights, name):
    mod2 = mod.reshape(BATCH, N_MOD * D_MODEL)
    return pl.pallas_call(
        _shiftproj_body,
        grid=(1,),
        in_specs=[pl.BlockSpec((BATCH, D_MODEL), lambda i: (0, mod_row))]
                 + [_const_spec(w.shape) for w in weights],
        out_specs=[pl.BlockSpec((BATCH, 1, w.shape[1]), lambda i: (0, 0, 0)) for w in weights],
        out_shape=[jax.ShapeDtypeStruct((BATCH, 1, w.shape[1]), F32) for w in weights],
        compiler_params=_params("arbitrary"),
        name=name,
    )(mod2, *weights)


def _row_scale(x):
    return lax.rsqrt(jnp.mean(x * x, axis=-1, keepdims=True) + EPS)


def _ffn_tiles(xs, m, mod_row, g_ref, s1_ref, s3_ref, w1_ref, w3_ref, w2_ref):
    scale, gate = m[mod_row + 1:mod_row + 2], m[mod_row + 2:mod_row + 3]
    fc = FF_CHUNK_TILES * MXU_TILE
    gs = g_ref[...] * (1.0 + scale)
    us = [(x * gs).astype(BF) for x in xs]
    rs = [_row_scale(x) for x in xs]
    accs = [None] * len(xs)
    for lo in range(0, D_FF, fc):
        cols = slice(lo, min(lo + fc, D_FF))
        for s, u in enumerate(us):
            h1 = rs[s] * _dot(u, w1_ref[:, cols]) + s1_ref[0, :, cols]
            h3 = rs[s] * _dot(u, w3_ref[:, cols]) + s3_ref[0, :, cols]
            p = _dot((_silu(h1) * h3).astype(BF), w2_ref[cols, :])
            accs[s] = p if accs[s] is None else accs[s] + p
    return [x + (0.5 * (1.0 + gate)) * acc for x, acc in zip(xs, accs)]


def _ffn_body(x_ref, mod_ref, g_ref, s1_ref, s3_ref, w1_ref, w3_ref, w2_ref, o_ref, *, mod_row):
    sub = FFN_TILE // FFN_SUBTILES
    rows = [slice(s * sub, (s + 1) * sub) for s in range(FFN_SUBTILES)]
    outs = _ffn_tiles([x_ref[0, r, :] for r in rows], mod_ref[0], mod_row,
                      g_ref, s1_ref, s3_ref, w1_ref, w3_ref, w2_ref)
    for r, o in zip(rows, outs):
        o_ref[0, r, :] = o


def _ffn_call(x, mod, g, w1, w3, w2, mod_row):
    row = pl.BlockSpec((1, FFN_TILE, D_MODEL), lambda b, i: (b, i, 0))
    per_batch = pl.BlockSpec((1, 1, D_FF), lambda b, i: (b, 0, 0))
    s1, s3 = _shiftproj_call(mod, mod_row, (w1, w3), "ffn%d_shift" % mod_row)
    return pl.pallas_call(
        functools.partial(_ffn_body, mod_row=mod_row),
        grid=(BATCH, SEQ // FFN_TILE),
        in_specs=[row,
                  pl.BlockSpec((1, N_MOD, D_MODEL), lambda b, i: (b, 0, 0)),
                  _const_spec((1, D_MODEL)),
                  per_batch, per_batch,
                  _const_spec((D_MODEL, D_FF)),
                  _const_spec((D_MODEL, D_FF)),
                  _const_spec((D_FF, D_MODEL))],
        out_specs=row,
        out_shape=jax.ShapeDtypeStruct((BATCH, SEQ, D_MODEL), F32),
        compiler_params=_params("parallel", "parallel"),
        name="ffn%d" % mod_row,
    )(x, mod, g, s1, s3, w1, w3, w2)


QK_W = 2 * ATT_OUT_WIDTH
GRP_W = 3 * ATT_OUT_WIDTH
GD_PAD = 128


def _proj_body(h_ref, mod_ref, g_ref, satt_ref, sgq_ref, sgv_ref, sgr_ref, sgd_ref, sgate_ref,
               watt_ref, wgq_ref, wgkt_ref, wgv_ref, wgr_ref, wgd_ref,
               wgate_ref, hsum_ref, hexp_ref, qkg_ref,
               a0_ref, a1_ref, a2_ref, gq_ref, gkt_ref, gv_ref, gr_ref, gd_ref, gate_ref, perm_ref):
    m = mod_ref[0]
    x = h_ref[0]
    xg = x * (g_ref[...] * (1.0 + m[4:5]))
    u = xg.astype(BF)
    rs = _row_scale(x)

    def proj(w, s):
        return rs * _dot(u, w) + s

    for gi, a_ref in enumerate((a0_ref, a1_ref, a2_ref)):
        d = ATT_GROUPS[gi][1]
        gcols = slice(gi * GRP_W, (gi + 1) * GRP_W)
        y = proj(watt_ref[:, gcols], satt_ref[0, :, gcols])
        qk = y[:, :QK_W]
        ss = _dot((qk * qk).astype(BF), hsum_ref[...])
        rx = _dot_split(lax.rsqrt(ss * (1.0 / ATT_HEAD_DIM) + EPS), hexp_ref)
        qkn = qk * rx * qkg_ref[...]
        if d == 1:
            a_ref[0, 0, :, :QK_W] = qkn.astype(BF)
            a_ref[0, 0, :, QK_W:] = y[:, QK_W:].astype(BF)
        else:
            for j in range(GRP_W // 128):
                cols = slice(j * 128, (j + 1) * 128)
                perm_ref[j] = qkn[:, cols] if j < QK_W // 128 else y[:, cols]
            for r in range(d):
                for j in range(GRP_W // 128):
                    a_ref[0, r, :, j * 128:(j + 1) * 128] = (
                        perm_ref[j, pl.ds(r, ROW_TILE // d, stride=d), :].astype(BF))
    gq_ref[0] = proj(wgq_ref[...], sgq_ref[0]).astype(BF)
    gv_ref[0] = proj(wgv_ref[...], sgv_ref[0]).astype(BF)
    gr_ref[0] = proj(wgr_ref[...], sgr_ref[0]).astype(BF)
    gd_ref[0] = proj(wgd_ref[...], sgd_ref[0]).astype(BF)
    gate_ref[0] = proj(wgate_ref[...], sgate_ref[0]).astype(BF)
    u_full = (xg * rs + m[3:4]).astype(BF)
    gkt_ref[0] = _dot_nt(wgkt_ref[...], u_full).astype(BF)


def _proj_call(h, mod, g, watt, wgq, wgkt, wgv, wgr, wgd, wgate, hsum, hexp, qkg):
    def row(w):
        return pl.BlockSpec((1, ROW_TILE, w), lambda b, i: (b, i, 0))

    def out(w):
        return jax.ShapeDtypeStruct((BATCH, SEQ, w), BF)

    kt_spec = pl.BlockSpec((1, GLA_KEY_DIM, ROW_TILE), lambda b, i: (b, 0, i))
    kt_shape = jax.ShapeDtypeStruct((BATCH, GLA_KEY_DIM, SEQ), BF)
    widths = (GLA_VAL_DIM, GLA_VAL_DIM, GD_PAD, 2 * D_MODEL)
    row_weights = (watt, wgq, wgv, wgr, wgd, wgate)
    shifts = _shiftproj_call(mod, 3, row_weights, "proj_shift")
    consts = (watt, wgq, wgkt, wgv, wgr, wgd, wgate, hsum, hexp, qkg)
    dils = [d for _, d in ATT_GROUPS]
    return pl.pallas_call(
        _proj_body,
        grid=(BATCH, SEQ // ROW_TILE),
        in_specs=[row(D_MODEL), pl.BlockSpec((1, N_MOD, D_MODEL), lambda b, i: (b, 0, 0)),
                  _const_spec(g.shape)]
                 + [pl.BlockSpec((1, 1, w.shape[1]), lambda b, i: (b, 0, 0)) for w in row_weights]
                 + [_const_spec(a.shape) for a in consts],
        out_specs=[_dilated_spec(d, GRP_W) for d in dils] + [row(GLA_KEY_DIM), kt_spec]
                  + [row(w) for w in widths],
        out_shape=[jax.ShapeDtypeStruct((BATCH, d, SEQ // d, GRP_W), BF) for d in dils]
                  + [out(GLA_KEY_DIM), kt_shape] + [out(w) for w in widths],
        scratch_shapes=[pltpu.VMEM((GRP_W // 128, ROW_TILE, 128), F32)],
        compiler_params=_params("parallel", "parallel"),
        name="proj",
    )(h, mod, g, *shifts, *consts)


ATT_GROUP_BLOCKS = 4


def _attn_blocks(a_ref, bias_ref, o_ref, l_ref, blocks):
    W, NH = ATT_OUT_WIDTH, ATT_HEADS_PER_GROUP
    head_of_lane = lax.broadcasted_iota(jnp.int32, (ATT_BLK, W), 1) // ATT_HEAD_DIM
    lane = lax.broadcasted_iota(jnp.int32, (ATT_BLK, 128), 1)
    qs, ks, vs, biases, q0s = [], [], [], [], []
    for r, n, first in blocks:
        if first:
            q0, k0, nk = 0, 0, ATT_BLK
            biases.append(bias_ref[:, ATT_BLK:])
        else:
            q0, k0, nk = n * ATT_BLK, (n - 1) * ATT_BLK, 2 * ATT_BLK
            if not isinstance(n, int):
                q0, k0 = pl.multiple_of(q0, ATT_BLK), pl.multiple_of(k0, ATT_BLK)
            biases.append(bias_ref[...])
        q0s.append(q0)
        q = a_ref[0, r, pl.ds(q0, ATT_BLK), :W]
        zero = jnp.zeros_like(q)
        qs.append(jnp.concatenate([jnp.where(head_of_lane == j, q, zero) for j in range(NH)],
                                  axis=0))
        ks.append(a_ref[0, r, pl.ds(k0, nk), W:2 * W])
        vs.append(a_ref[0, r, pl.ds(k0, nk), 2 * W:])
    ss = [_dot_nt(q4, k) + b for q4, k, b in zip(qs, ks, biases)]
    mxs = [jnp.max(s, axis=-1, keepdims=True) for s in ss]
    es = [jnp.exp(s - mx) for s, mx in zip(ss, mxs)]
    dens = [jnp.sum(e, axis=-1, keepdims=True) for e in es]
    pvs = [_dot(e.astype(BF), v) for e, v in zip(es, vs)]
    for (r, _, _), q0, pv, mx, den in zip(blocks, q0s, pvs, mxs, dens):
        pv = pv * (1.0 / den)
        lse = mx + jnp.log(den)
        o = jnp.zeros((ATT_BLK, W), F32)
        l = jnp.zeros((ATT_BLK, 128), F32)
        for j in range(NH):
            rows = slice(j * ATT_BLK, (j + 1) * ATT_BLK)
            o = jnp.where(head_of_lane == j, pv[rows], o)
            l = jnp.where(lane == j, lse[rows], l)
        o_ref[0, r, pl.ds(q0, ATT_BLK), :] = o.astype(o_ref.dtype)
        l_ref[0, r, pl.ds(q0, ATT_BLK), :] = l


def _attn_body(a_ref, bias_ref, o_ref, l_ref, *, subseqs, n_blocks):
    G = ATT_GROUP_BLOCKS
    if n_blocks == 1:
        for r0 in range(0, subseqs, G):
            _attn_blocks(a_ref, bias_ref, o_ref, l_ref,
                         [(r, 0, True) for r in range(r0, min(r0 + G, subseqs))])
        return
    for r in range(subseqs):
        for n0 in range(0, n_blocks, G):
            _attn_blocks(a_ref, bias_ref, o_ref, l_ref,
                         [(r, n, n == 0) for n in range(n0, n0 + G)])


def _attn_bias(group):
    _, dilation = ATT_GROUPS[group]
    heads = np.arange(group * ATT_HEADS_PER_GROUP, (group + 1) * ATT_HEADS_PER_GROUP)
    slopes = (2.0 ** (-ALIBI_MAX * (heads + 1).astype(np.float32) / ATT_HEADS)).astype(np.float32)
    qi = np.arange(ATT_BLK)[:, None]
    kj = np.arange(2 * ATT_BLK)[None, :]
    steps = qi + ATT_BLK - kj
    valid = (steps >= 0) & (steps <= ATT_BLK)
    bias = -(slopes * np.float32(dilation))[:, None, None] * steps.astype(np.float32)[None]
    bias = np.where(valid[None], bias, np.float32(NEG_BIG)).astype(np.float32)
    return bias.reshape(ATT_HEADS_PER_GROUP * ATT_BLK, 2 * ATT_BLK)


def _attn_call(a, group):
    _, d = ATT_GROUPS[group]
    L = SEQ // d
    n_blocks = L // ATT_BLK
    subseqs = d if n_blocks == 1 else min(d, 4)

    def spec(w):
        return pl.BlockSpec((1, subseqs, L, w), lambda b, r: (b, r, 0, 0))

    return pl.pallas_call(
        functools.partial(_attn_body, subseqs=subseqs, n_blocks=n_blocks),
        grid=(BATCH, d // subseqs),
        in_specs=[spec(GRP_W), _const_spec((ATT_HEADS_PER_GROUP * ATT_BLK, 2 * ATT_BLK))],
        out_specs=[spec(ATT_OUT_WIDTH), spec(128)],
        out_shape=[jax.ShapeDtypeStruct((BATCH, d, L, ATT_OUT_WIDTH), BF),
                   jax.ShapeDtypeStruct((BATCH, d, L, 128), F32)],
        compiler_params=_params("parallel", "parallel"),
        name="attn%d" % d,
    )(a, jnp.asarray(_attn_bias(group)))


GLA_PAIR = 2 * GLA_CHUNK


def _gla_cum_decay(gd_ref, up_ref, gb_ref, tril2_ref):
    x = _dot(gd_ref[0], up_ref[...]) + gb_ref[...]
    log_a = (jnp.minimum(x, 0.0) - jnp.log(1.0 + jnp.exp(-jnp.abs(x)))) * (1.0 / GLA_TAU)
    la_hi, la_lo = _split_bf16(log_a)
    pieces = []
    for p in range(GLA_BLOCK // GLA_PAIR):
        rows = slice(p * GLA_PAIR, (p + 1) * GLA_PAIR)
        pieces.append(_dot(tril2_ref[...], jnp.concatenate([la_hi[rows], la_lo[rows]], axis=0)))
    return pieces


def _gla_pair(p, b, states, gq_ref, gkt_ref, gv_ref, o_ref):
    C, P = GLA_CHUNK, GLA_PAIR
    rows = slice(p * P, (p + 1) * P)
    q_dec =(gq_ref[0, rows, :].astype(F32) * (GLA_DK ** -0.5) * jnp.exp(b)).astype(BF)
    b_t = b.T
    k_dec_t_bf = (gkt_ref[0, :, rows].astype(F32) * jnp.exp(-b_t)).astype(BF)
    decay_t = jnp.exp(b_t)
    lane = lax.broadcasted_iota(jnp.int32, (GLA_KEY_DIM, P), 1)
    zero = jnp.zeros_like(k_dec_t_bf)
    k_dec_t_chunk = [jnp.where(lane < C, k_dec_t_bf, zero), jnp.where(lane >= C, k_dec_t_bf, zero)]
    ri = lax.broadcasted_iota(jnp.int32, (P, P), 0)
    ci = lax.broadcasted_iota(jnp.int32, (P, P), 1)
    causal = (ri >= ci) & ((ri >= C) == (ci >= C))
    new_states = []
    for h in range(GLA_HEADS):
        kc = slice(h * GLA_DK, (h + 1) * GLA_DK)
        vc = slice(h * GLA_DV, (h + 1) * GLA_DV)
        v = gv_ref[0, rows, vc]
        attn = jnp.where(causal, _dot(q_dec[:, kc], k_dec_t_bf[kc, :]), 0.0).astype(BF)
        stacked = _dot(jnp.concatenate([k_dec_t_chunk[0][kc, :], k_dec_t_chunk[1][kc, :], attn],
                                       axis=0), v)
        st = states[h]
        for cc in range(2):
            crow = slice(cc * C, (cc + 1) * C)
            o = stacked[2 * GLA_DK + cc * C:2 * GLA_DK + (cc + 1) * C] + _dot(q_dec[crow, kc],
                                                                              st.astype(BF))
            o_ref[0, p * P + cc * C:p * P + (cc + 1) * C, vc] = o.astype(o_ref.dtype)
            last = (cc + 1) * C - 1
            st = decay_t[kc, last:last + 1] * (st + stacked[cc * GLA_DK:(cc + 1) * GLA_DK])
        new_states.append(st)
    return new_states


def _gla_body(gq_ref, gkt_ref, gv_ref, gd_ref, up_ref, gb_ref, tril2_ref, o_ref, state_ref):
    @pl.when(pl.program_id(1) == 0)
    def _():
        state_ref[...] = jnp.zeros_like(state_ref)

    bs = _gla_cum_decay(gd_ref, up_ref, gb_ref, tril2_ref)
    states = [state_ref[h] for h in range(GLA_HEADS)]
    for p in range(GLA_BLOCK // GLA_PAIR):
        states = _gla_pair(p, bs[p], states, gq_ref, gkt_ref, gv_ref, o_ref)
    for h in range(GLA_HEADS):
        state_ref[h] = states[h]


def _pair_tril2():
    i = np.arange(GLA_PAIR)
    t = ((i[:, None] // GLA_CHUNK) == (i[None, :] // GLA_CHUNK)) & (i[:, None] >= i[None, :])
    return np.concatenate([t, t], axis=1).astype(np.float32)


def _gla_call(gq, gkt, gv, gd, up, gbias):
    def row(w):
        return pl.BlockSpec((1, GLA_BLOCK, w), lambda b, i: (b, i, 0))

    consts = (up, gbias, jnp.asarray(_pair_tril2(), BF))
    return pl.pallas_call(
        _gla_body,
        grid=(BATCH, SEQ // GLA_BLOCK),
        in_specs=[row(GLA_KEY_DIM),
                  pl.BlockSpec((1, GLA_KEY_DIM, GLA_BLOCK), lambda b, i: (b, 0, i)),
                  row(GLA_VAL_DIM), row(GD_PAD)]
                 + [_const_spec(a.shape) for a in consts],
        out_specs=row(GLA_VAL_DIM),
        out_shape=jax.ShapeDtypeStruct((BATCH, SEQ, GLA_VAL_DIM), BF),
        scratch_shapes=[pltpu.VMEM((GLA_HEADS, GLA_DK, GLA_DV), F32)],
        compiler_params=_params("parallel", "arbitrary"),
        name="gla",
    )(gq, gkt, gv, gd, *consts)


def _undilate(src_ref, scr_ref, d):
    if d == 1:
        return src_ref[0, 0].astype(F32)
    planes = scr_ref.shape[0]
    for r in range(d):
        x = src_ref[0, r].astype(F32)
        for j in range(planes):
            scr_ref[j, pl.ds(r, ROW_TILE // d, stride=d), :] = x[:, j * 128:(j + 1) * 128]
    return jnp.concatenate([scr_ref[j] for j in range(planes)], axis=-1)


def _merge_body(h_ref, mod_ref, o0_ref, o1_ref, o2_ref, l0_ref, l1_ref, l2_ref,
                ogla_ref, gr_ref, gate_ref, hexp_ref, gain_ref, wba_ref, wbg_ref, wout_ref,
                g_ref, s1_ref, s3_ref, w1_ref, w3_ref, w2_ref, out_ref,
                os1_ref, os2_ref, ls1_ref, ls2_ref):
    dils = [d for _, d in ATT_GROUPS]
    lses = [_undilate(r, s, d) for r, s, d in
            zip((l0_ref, l1_ref, l2_ref), (None, ls1_ref, ls2_ref), dils)]
    outs = [_undilate(r, s, d) for r, s, d in
            zip((o0_ref, o1_ref, o2_ref), (None, os1_ref, os2_ref), dils)]
    mx = jnp.maximum(jnp.maximum(lses[0], lses[1]), lses[2])
    es = [jnp.exp(l - mx) for l in lses]
    den = es[0] + es[1] + es[2]
    o_att = None
    for e, o in zip(es, outs):
        t = _dot_split(e / den, hexp_ref) * o
        o_att = t if o_att is None else o_att + t
    att = _dot(o_att.astype(BF), wba_ref[...])
    gated = []
    for h in range(GLA_HEADS):
        vc = slice(h * GLA_DV, (h + 1) * GLA_DV)
        o = ogla_ref[0, :, vc].astype(F32)
        ms = jnp.mean(o * o, axis=-1, keepdims=True)
        y = o * lax.rsqrt(ms + EPS) * gain_ref[:, vc]
        gated.append(y.astype(BF) * _silu(gr_ref[0, :, vc]))
    gla = _dot(jnp.concatenate(gated, axis=1), wbg_ref[...])
    merged = (_sigmoid(gate_ref[0, :, :D_MODEL]) * att.astype(BF)
              + _sigmoid(gate_ref[0, :, D_MODEL:]) * gla.astype(BF))
    m = mod_ref[0]
    h = h_ref[0] + (1.0 + m[5:6]) * _dot(merged, wout_ref[...])
    out_ref[0] = _ffn_tiles([h], m, 6, g_ref, s1_ref, s3_ref, w1_ref, w3_ref, w2_ref)[0]


def _merge_ffn_call(h, mod, os_, ls_, ogla, gr, gates, hexp, gain, wba, wbg, wout, g, w1, w3, w2):
    def row(w):
        return pl.BlockSpec((1, ROW_TILE, w), lambda b, i: (b, i, 0))

    s1, s3 = _shiftproj_call(mod, 6, (w1, w3), "ffn6_shift")
    per_batch = pl.BlockSpec((1, 1, D_FF), lambda b, i: (b, 0, 0))
    consts = (hexp, gain, wba, wbg, wout, g)
    weights = (w1, w3, w2)
    dils = [d for _, d in ATT_GROUPS]
    return pl.pallas_call(
        _merge_body,
        grid=(BATCH, SEQ // ROW_TILE),
        in_specs=[row(D_MODEL), pl.BlockSpec((1, N_MOD, D_MODEL), lambda b, i: (b, 0, 0))]
                 + [_dilated_spec(d, ATT_OUT_WIDTH) for d in dils]
                 + [_dilated_spec(d, 128) for d in dils]
                 + [row(GLA_VAL_DIM), row(GLA_VAL_DIM), row(2 * D_MODEL)]
                 + [_const_spec(a.shape) for a in consts] + [per_batch, per_batch]
                 + [_const_spec(a.shape) for a in weights],
        out_specs=row(D_MODEL),
        out_shape=jax.ShapeDtypeStruct((BATCH, SEQ, D_MODEL), F32),
        scratch_shapes=[pltpu.VMEM((ATT_OUT_WIDTH // 128, ROW_TILE, 128), F32)] * 2
                       + [pltpu.VMEM((1, ROW_TILE, 128), F32)] * 2,
        compiler_params=_params("parallel", "parallel"),
        name="merge_ffn",
    )(h, mod, *os_, *ls_, ogla, gr, gates, *consts, s1, s3, *weights)


def _head_matrix(n_lanes_in, width):
    j = np.arange(n_lanes_in)[:, None]
    c = np.arange(width)[None, :]
    return (c // ATT_HEAD_DIM == j).astype(np.float32)


def _layer(h, mod, g_ffn1, f1w1, f1w3, f1w2, g_mix, w_in, q_norm_g, k_norm_g, gate_up, gate_bias,
           out_norm_g, w_branch_att, w_branch_gla, w_out, g_ffn2, f2w1, f2w3, f2w2):
    bf = lambda w: w.astype(BF)
    row = lambda v: v.reshape(1, -1)

    h = _ffn_call(h, mod, row(g_ffn1), bf(f1w1), bf(f1w3), bf(f1w2), 0)

    splits = np.cumsum((ATT_WIDTH, ATT_WIDTH, ATT_WIDTH, GLA_KEY_DIM, GLA_KEY_DIM, GLA_VAL_DIM,
                        GLA_VAL_DIM, GLA_GATE_RANK))
    aq, ak, av, gq, gk, gv, gr, gdn, gates = jnp.split(bf(w_in), [int(s) for s in splits], axis=1)
    W = ATT_OUT_WIDTH
    watt = jnp.concatenate([t[:, gi * W:(gi + 1) * W] for gi in range(len(ATT_GROUPS))
                            for t in (aq, ak, av)], axis=1)
    wgd = jnp.pad(gdn, ((0, 0), (0, GD_PAD - GLA_GATE_RANK)))
    head_qk = _head_matrix(128, QK_W)
    hsum_qk = jnp.asarray(head_qk.T, BF)
    hexp_qk = jnp.asarray(np.concatenate([head_qk, head_qk]), BF)
    qkg = jnp.concatenate([jnp.tile(q_norm_g * (ATT_HEAD_DIM ** -0.5), ATT_HEADS_PER_GROUP),
                           jnp.tile(k_norm_g, ATT_HEADS_PER_GROUP)])
    a0, a1, a2, pgq, pgkt, pgv, pgr, pgd, pgate = _proj_call(
        h, mod, row(g_mix), watt, gq, gk.T, gv, gr, wgd, gates, hsum_qk, hexp_qk, row(qkg))

    outs = [_attn_call(a, gi) for gi, a in enumerate((a0, a1, a2))]
    up = jnp.pad(bf(gate_up), ((0, GD_PAD - GLA_GATE_RANK), (0, 0)))
    ogla = _gla_call(pgq, pgkt, pgv, pgd, up, row(gate_bias))

    head_o = _head_matrix(128, ATT_OUT_WIDTH)
    hexp_o = jnp.asarray(np.concatenate([head_o, head_o]), BF)
    return _merge_ffn_call(h, mod, [o for o, _ in outs], [l for _, l in outs], ogla, pgr, pgate,
                           hexp_o, row(jnp.tile(out_norm_g, GLA_HEADS)), bf(w_branch_att),
                           bf(w_branch_gla), bf(w_out), row(g_ffn2), bf(f2w1), bf(f2w3), bf(f2w2))


def kernel(x, c, w_mod, b_mod, g_ffn1, ffn1_w1, ffn1_w3, ffn1_w2, g_mix, w_in, q_norm_g, k_norm_g,
           gla_gate_up, gla_gate_bias, gla_out_norm_g, w_branch_att, w_branch_gla, w_out,
           g_ffn2, ffn2_w1, ffn2_w3, ffn2_w2):
    h = x
    for l in range(w_mod.shape[0]):
        mod = _mod_call(c, w_mod, b_mod[l], l).reshape(BATCH, N_MOD, D_MODEL)
        h = _layer(h, mod, g_ffn1[l], ffn1_w1[l], ffn1_w3[l], ffn1_w2[l], g_mix[l], w_in[l],
                   q_norm_g[l], k_norm_g[l], gla_gate_up[l], gla_gate_bias[l], gla_out_norm_g[l],
                   w_branch_att[l], w_branch_gla[l], w_out[l], g_ffn2[l], ffn2_w1[l], ffn2_w3[l],
                   ffn2_w2[l])
    return h
```

```python
import functools

import numpy as np
import jax
import jax.numpy as jnp
from jax import lax
from jax.experimental import pallas as pl
from jax.experimental.pallas import tpu as pltpu

D_MODEL = 1024
BATCH = 16
SEQ = 2048
N_MOD = 9
D_FF = 2816
EPS = 1e-6
ATT_GROUPS = ((128, 1), (512, 4), (2048, 16))
ATT_HEADS_PER_GROUP = 4
ATT_HEADS = 12
ATT_HEAD_DIM = 64
ATT_WIDTH = 768
ATT_OUT_WIDTH = 256
ALIBI_MAX = 8.0
GLA_HEADS = 4
GLA_KEY_DIM = 512
GLA_VAL_DIM = 1024
GLA_DK = 128
GLA_DV = 256
GLA_GATE_RANK = 16
GLA_TAU = 16.0
GLA_CHUNK = 64
ATT_BLK = 128
NEG_BIG = -1e30
LOG2E = float(np.log2(np.e))
LN2 = float(np.log(2.0))

BF = jnp.bfloat16
F32 = jnp.float32

VMEM_LIMIT_BYTES = 56 * 1024 * 1024

ROW_TILE = 512
FFN_TILE = 1024
FFN_SUBTILES = 2
MXU_TILE = 256
FF_CHUNK_TILES = 3
GLA_BLOCK = 1024


def _dot(a, b):
    return jnp.dot(a, b, preferred_element_type=F32)


def _dot_nt(a, b):
    return lax.dot_general(a, b, (((1,), (1,)), ((), ())), preferred_element_type=F32)


def _dot_tn(a, b):
    return lax.dot_general(a, b, (((0,), (0,)), ((), ())), preferred_element_type=F32)


def _split_bf16(x):
    hi = x.astype(BF)
    lo = (x - hi.astype(F32)).astype(BF)
    return hi, lo


def _dot_split(x, stacked_ref):
    hi, lo = _split_bf16(x)
    return _dot(jnp.concatenate([hi, lo], axis=1), stacked_ref[...])


def _sigmoid(x):
    return 0.5 * jnp.tanh(0.5 * x) + 0.5


def _silu(x):
    return x * _sigmoid(x)


def _const_spec(shape):
    nd = len(shape)
    return pl.BlockSpec(shape, lambda *_: (0,) * nd, pipeline_mode=pl.Buffered(1))


def _dilated_spec(d, width):
    return pl.BlockSpec((1, d, ROW_TILE // d, width), lambda b, i: (b, 0, i, 0))


def _params(*sem):
    return pltpu.CompilerParams(dimension_semantics=sem, vmem_limit_bytes=VMEM_LIMIT_BYTES)


def _mod_body(c_ref, w_ref, b_ref, o_ref):
    c = c_ref[...]
    o_ref[...] = _dot(_silu(c).astype(BF), w_ref[...].astype(BF)) + b_ref[...]


def _mod_call(c, w_mod, b_mod, layer):
    n = w_mod.shape[2]
    bn = D_MODEL
    return pl.pallas_call(
        _mod_body,
        grid=(n // bn,),
        in_specs=[pl.BlockSpec((BATCH, D_MODEL), lambda j: (0, 0)),
                  pl.BlockSpec((None, D_MODEL, bn), lambda j: (layer, 0, j)),
                  pl.BlockSpec((1, bn), lambda j: (0, j))],
        out_specs=pl.BlockSpec((BATCH, bn), lambda j: (0, j)),
        out_shape=jax.ShapeDtypeStruct((BATCH, n), F32),
        compiler_params=_params("arbitrary"),
        name="mod",
    )(c, w_mod, b_mod.reshape(1, n))


def _shiftproj_body(mod_ref, *refs):
    n = len(refs) // 2
    sh = mod_ref[...].astype(BF)
    for w_ref, o_ref in zip(refs[:n], refs[n:]):
        o_ref[:, 0, :] = _dot(sh, w_ref[...])


def _shiftproj_call(mod, mod_row, weights, name):
    mod2 = mod.reshape(BATCH, N_MOD * D_MODEL)
    return pl.pallas_call(
        _shiftproj_body,
        grid=(1,),
        in_specs=[pl.BlockSpec((BATCH, D_MODEL), lambda i: (0, mod_row))]
                 + [_const_spec(w.shape) for w in weights],
        out_specs=[pl.BlockSpec((BATCH, 1, w.shape[1]), lambda i: (0, 0, 0)) for w in weights],
        out_shape=[jax.ShapeDtypeStruct((BATCH, 1, w.shape[1]), F32) for w in weights],
        compiler_params=_params("arbitrary"),
        name=name,
    )(mod2, *weights)


def _row_scale(x):
    return lax.rsqrt(jnp.mean(x * x, axis=-1, keepdims=True) + EPS)


def _ffn_tiles(xs, m, mod_row, g_ref, s1_ref, s3_ref, w1_ref, w3_ref, w2_ref):
    scale, gate = m[mod_row + 1:mod_row + 2], m[mod_row + 2:mod_row + 3]
    fc = FF_CHUNK_TILES * MXU_TILE
    gs = g_ref[...] * (1.0 + scale)
    us = [(x * gs).astype(BF) for x in xs]
    rs = [_row_scale(x) for x in xs]
    accs = [None] * len(xs)
    for lo in range(0, D_FF, fc):
        cols = slice(lo, min(lo + fc, D_FF))
        for s, u in enumerate(us):
            h1 = rs[s] * _dot(u, w1_ref[:, cols]) + s1_ref[0, :, cols]
            h3 = rs[s] * _dot(u, w3_ref[:, cols]) + s3_ref[0, :, cols]
            p = _dot((_silu(h1) * h3).astype(BF), w2_ref[cols, :])
            accs[s] = p if accs[s] is None else accs[s] + p
    return [x + (0.5 * (1.0 + gate)) * acc for x, acc in zip(xs, accs)]


def _ffn_body(x_ref, mod_ref, g_ref, s1_ref, s3_ref, w1_ref, w3_ref, w2_ref, o_ref, *, mod_row):
    sub = FFN_TILE // FFN_SUBTILES
    rows = [slice(s * sub, (s + 1) * sub) for s in range(FFN_SUBTILES)]
    outs = _ffn_tiles([x_ref[0, r, :] for r in rows], mod_ref[0], mod_row,
                      g_ref, s1_ref, s3_ref, w1_ref, w3_ref, w2_ref)
    for r, o in zip(rows, outs):
        o_ref[0, r, :] = o


def _ffn_call(x, mod, g, w1, w3, w2, mod_row):
    row = pl.BlockSpec((1, FFN_TILE, D_MODEL), lambda b, i: (b, i, 0))
    per_batch = pl.BlockSpec((1, 1, D_FF), lambda b, i: (b, 0, 0))
    s1, s3 = _shiftproj_call(mod, mod_row, (w1, w3), "ffn%d_shift" % mod_row)
    return pl.pallas_call(
        functools.partial(_ffn_body, mod_row=mod_row),
        grid=(BATCH, SEQ // FFN_TILE),
        in_specs=[row,
                  pl.BlockSpec((1, N_MOD, D_MODEL), lambda b, i: (b, 0, 0)),
                  _const_spec((1, D_MODEL)),
                  per_batch, per_batch,
                  _const_spec((D_MODEL, D_FF)),
                  _const_spec((D_MODEL, D_FF)),
                  _const_spec((D_FF, D_MODEL))],
        out_specs=row,
        out_shape=jax.ShapeDtypeStruct((BATCH, SEQ, D_MODEL), F32),
        compiler_params=_params("parallel", "parallel"),
        name="ffn%d" % mod_row,
    )(x, mod, g, s1, s3, w1, w3, w2)


QK_W = 2 * ATT_OUT_WIDTH
GRP_W = 3 * ATT_OUT_WIDTH
GD_PAD = 128


def _proj_body(h_ref, mod_ref, g_ref, satt_ref, sgq_ref, sgv_ref, sgr_ref, sgd_ref, sgate_ref,
               watt_ref, wgq_ref, wgkt_ref, wgv_ref, wgr_ref, wgd_ref,
               wgate_ref, hsum_ref, hexp_ref, qkg_ref,
               a0_ref, a1_ref, a2_ref, gq_ref, gkt_ref, gv_ref, gr_ref, gd_ref, gate_ref, perm_ref):
    m = mod_ref[0]
    x = h_ref[0]
    xg = x * (g_ref[...] * (1.0 + m[4:5]))
    u = xg.astype(BF)
    rs = _row_scale(x)

    def proj(w, s):
        return rs * _dot(u, w) + s

    for gi, a_ref in enumerate((a0_ref, a1_ref, a2_ref)):
        d = ATT_GROUPS[gi][1]
        gcols = slice(gi * GRP_W, (gi + 1) * GRP_W)
        y = proj(watt_ref[:, gcols], satt_ref[0, :, gcols])
        qk = y[:, :QK_W]
        ss = _dot((qk * qk).astype(BF), hsum_ref[...])
        rx = _dot_split(lax.rsqrt(ss * (1.0 / ATT_HEAD_DIM) + EPS), hexp_ref)
        qkn = qk * rx * qkg_ref[...]
        if d == 1:
            a_ref[0, 0, :, :QK_W] = qkn.astype(BF)
            a_ref[0, 0, :, QK_W:] = y[:, QK_W:].astype(BF)
        else:
            for j in range(GRP_W // 128):
                cols = slice(j * 128, (j + 1) * 128)
                perm_ref[j] = qkn[:, cols] if j < QK_W // 128 else y[:, cols]
            for r in range(d):
                for j in range(GRP_W // 128):
                    a_ref[0, r, :, j * 128:(j + 1) * 128] = (
                        perm_ref[j, pl.ds(r, ROW_TILE // d, stride=d), :].astype(BF))
    gq_ref[0] = proj(wgq_ref[...], sgq_ref[0]).astype(BF)
    gv_ref[0] = proj(wgv_ref[...], sgv_ref[0]).astype(BF)
    gr_ref[0] = proj(wgr_ref[...], sgr_ref[0]).astype(BF)
    gd_ref[0] = proj(wgd_ref[...], sgd_ref[0]).astype(BF)
    gate_ref[0] = proj(wgate_ref[...], sgate_ref[0]).astype(BF)
    u_full = (xg * rs + m[3:4]).astype(BF)
    gkt_ref[0] = _dot_nt(wgkt_ref[...], u_full).astype(BF)


def _proj_call(h, mod, g, watt, wgq, wgkt, wgv, wgr, wgd, wgate, hsum, hexp, qkg):
    def row(w):
        return pl.BlockSpec((1, ROW_TILE, w), lambda b, i: (b, i, 0))

    def out(w):
        return jax.ShapeDtypeStruct((BATCH, SEQ, w), BF)

    kt_spec = pl.BlockSpec((1, GLA_KEY_DIM, ROW_TILE), lambda b, i: (b, 0, i))
    kt_shape = jax.ShapeDtypeStruct((BATCH, GLA_KEY_DIM, SEQ), BF)
    widths = (GLA_VAL_DIM, GLA_VAL_DIM, GD_PAD, 2 * D_MODEL)
    row_weights = (watt, wgq, wgv, wgr, wgd, wgate)
    shifts = _shiftproj_call(mod, 3, row_weights, "proj_shift")
    consts = (watt, wgq, wgkt, wgv, wgr, wgd, wgate, hsum, hexp, qkg)
    dils = [d for _, d in ATT_GROUPS]
    return pl.pallas_call(
        _proj_body,
        grid=(BATCH, SEQ // ROW_TILE),
        in_specs=[row(D_MODEL), pl.BlockSpec((1, N_MOD, D_MODEL), lambda b, i: (b, 0, 0)),
                  _const_spec(g.shape)]
                 + [pl.BlockSpec((1, 1, w.shape[1]), lambda b, i: (b, 0, 0)) for w in row_weights]
                 + [_const_spec(a.shape) for a in consts],
        out_specs=[_dilated_spec(d, GRP_W) for d in dils] + [row(GLA_KEY_DIM), kt_spec]
                  + [row(w) for w in widths],
        out_shape=[jax.ShapeDtypeStruct((BATCH, d, SEQ // d, GRP_W), BF) for d in dils]
                  + [out(GLA_KEY_DIM), kt_shape] + [out(w) for w in widths],
        scratch_shapes=[pltpu.VMEM((GRP_W // 128, ROW_TILE, 128), F32)],
        compiler_params=_params("parallel", "parallel"),
        name="proj",
    )(h, mod, g, *shifts, *consts)


ATT_GROUP_BLOCKS = 4


def _attn_blocks(a_ref, bias_ref, o_ref, l_ref, blocks):
    W, NH = ATT_OUT_WIDTH, ATT_HEADS_PER_GROUP
    lane = lax.broadcasted_iota(jnp.int32, (ATT_BLK, 128), 1)
    low_half = lane < ATT_HEAD_DIM
    zero_col = jnp.zeros((ATT_BLK, 128), BF)
    qs, ks, vs, biases, q0s = [], [], [], [], []
    for r, n, first in blocks:
        if first:
            q0, k0, nk = 0, 0, ATT_BLK
            biases.append(bias_ref[:, ATT_BLK:])
        else:
            q0, k0, nk = n * ATT_BLK, (n - 1) * ATT_BLK, 2 * ATT_BLK
            if not isinstance(n, int):
                q0, k0 = pl.multiple_of(q0, ATT_BLK), pl.multiple_of(k0, ATT_BLK)
            biases.append(bias_ref[...])
        q0s.append(q0)
        stacked = []
        for j in range(NH):
            col = a_ref[0, r, pl.ds(q0, ATT_BLK), (j // 2) * 128:(j // 2 + 1) * 128]
            col = jnp.where(low_half == (j % 2 == 0), col, zero_col)
            stacked.append(jnp.concatenate([col, zero_col] if j < 2 else [zero_col, col], axis=1))
        qs.append(jnp.concatenate(stacked, axis=0))
        ks.append(a_ref[0, r, pl.ds(k0, nk), W:2 * W])
        vs.append(a_ref[0, r, pl.ds(k0, nk), 2 * W:])
    ss = [_dot_nt(q4, k) + b for q4, k, b in zip(qs, ks, biases)]
    mxs = [jnp.max(s, axis=-1, keepdims=True) for s in ss]
    es = [jnp.exp2(s - mx) for s, mx in zip(ss, mxs)]
    dens = [jnp.sum(e, axis=-1, keepdims=True) for e in es]
    pvs = [_dot(e.astype(BF), v) for e, v in zip(es, vs)]
    for (r, _, _), q0, pv, mx, den in zip(blocks, q0s, pvs, mxs, dens):
        rden = 1.0 / den
        lse = mx * LN2 + jnp.log(den)
        l = jnp.zeros((ATT_BLK, 128), F32)
        heads = []
        for j in range(NH):
            rows = slice(j * ATT_BLK, (j + 1) * ATT_BLK)
            heads.append(pv[rows, (j // 2) * 128:(j // 2 + 1) * 128] * rden[rows])
            l = jnp.where(lane == j, lse[rows], l)
        o = jnp.concatenate([jnp.where(low_half, heads[0], heads[1]),
                             jnp.where(low_half, heads[2], heads[3])], axis=1)
        o_ref[0, r, pl.ds(q0, ATT_BLK), :] = o.astype(o_ref.dtype)
        l_ref[0, r, pl.ds(q0, ATT_BLK), :] = l


def _attn_body(a_ref, bias_ref, o_ref, l_ref, *, subseqs, n_blocks):
    G = ATT_GROUP_BLOCKS
    if n_blocks == 1:
        for r0 in range(0, subseqs, G):
            _attn_blocks(a_ref, bias_ref, o_ref, l_ref,
                         [(r, 0, True) for r in range(r0, min(r0 + G, subseqs))])
        return
    for r in range(subseqs):
        for n0 in range(0, n_blocks, G):
            _attn_blocks(a_ref, bias_ref, o_ref, l_ref,
                         [(r, n, n == 0) for n in range(n0, n0 + G)])


def _attn_bias(group):
    _, dilation = ATT_GROUPS[group]
    heads = np.arange(group * ATT_HEADS_PER_GROUP, (group + 1) * ATT_HEADS_PER_GROUP)
    slopes = (2.0 ** (-ALIBI_MAX * (heads + 1).astype(np.float32) / ATT_HEADS)).astype(np.float32)
    qi = np.arange(ATT_BLK)[:, None]
    kj = np.arange(2 * ATT_BLK)[None, :]
    steps = qi + ATT_BLK - kj
    valid = (steps >= 0) & (steps <= ATT_BLK)
    bias = -(slopes * np.float32(dilation))[:, None, None] * steps.astype(np.float32)[None]
    bias = np.where(valid[None], bias * np.float32(LOG2E), np.float32(NEG_BIG)).astype(np.float32)
    return bias.reshape(ATT_HEADS_PER_GROUP * ATT_BLK, 2 * ATT_BLK)


def _attn_call(a, group):
    _, d = ATT_GROUPS[group]
    L = SEQ // d
    n_blocks = L // ATT_BLK
    subseqs = d if n_blocks == 1 else min(d, 4)

    def spec(w):
        return pl.BlockSpec((1, subseqs, L, w), lambda b, r: (b, r, 0, 0))

    return pl.pallas_call(
        functools.partial(_attn_body, subseqs=subseqs, n_blocks=n_blocks),
        grid=(BATCH, d // subseqs),
        in_specs=[spec(GRP_W), _const_spec((ATT_HEADS_PER_GROUP * ATT_BLK, 2 * ATT_BLK))],
        out_specs=[spec(ATT_OUT_WIDTH), spec(128)],
        out_shape=[jax.ShapeDtypeStruct((BATCH, d, L, ATT_OUT_WIDTH), BF),
                   jax.ShapeDtypeStruct((BATCH, d, L, 128), F32)],
        compiler_params=_params("parallel", "parallel"),
        name="attn%d" % d,
    )(a, jnp.asarray(_attn_bias(group)))


GLA_PAIR = 2 * GLA_CHUNK


def _gla_cum_decay(gd_ref, up_ref, gb_ref, tril2_ref):
    x = _dot(gd_ref[0], up_ref[...]) + gb_ref[...]
    log_a = (jnp.minimum(x, 0.0) - jnp.log(1.0 + jnp.exp(-jnp.abs(x)))) * (1.0 / GLA_TAU)
    la_hi, la_lo = _split_bf16(log_a)
    pieces = []
    for p in range(GLA_BLOCK // GLA_PAIR):
        rows = slice(p * GLA_PAIR, (p + 1) * GLA_PAIR)
        pieces.append(_dot(tril2_ref[...], jnp.concatenate([la_hi[rows], la_lo[rows]], axis=0)))
    return pieces


def _gla_pair(p, b, states, gq_ref, gkt_ref, gv_ref, o_ref):
    C, P = GLA_CHUNK, GLA_PAIR
    rows = slice(p * P, (p + 1) * P)
    q_dec =(gq_ref[0, rows, :].astype(F32) * (GLA_DK ** -0.5) * jnp.exp(b)).astype(BF)
    b_t = b.T
    k_dec_t_bf = (gkt_ref[0, :, rows].astype(F32) * jnp.exp(-b_t)).astype(BF)
    decay_t = jnp.exp(b_t)
    lane = lax.broadcasted_iota(jnp.int32, (GLA_KEY_DIM, P), 1)
    zero = jnp.zeros_like(k_dec_t_bf)
    k_dec_t_chunk = [jnp.where(lane < C, k_dec_t_bf, zero), jnp.where(lane >= C, k_dec_t_bf, zero)]
    ri = lax.broadcasted_iota(jnp.int32, (P, P), 0)
    ci = lax.broadcasted_iota(jnp.int32, (P, P), 1)
    causal = (ri >= ci) & ((ri >= C) == (ci >= C))
    new_states = []
    for h in range(GLA_HEADS):
        kc = slice(h * GLA_DK, (h + 1) * GLA_DK)
        vc = slice(h * GLA_DV, (h + 1) * GLA_DV)
        v = gv_ref[0, rows, vc]
        attn = jnp.where(causal, _dot(q_dec[:, kc], k_dec_t_bf[kc, :]), 0.0).astype(BF)
        stacked = _dot(jnp.concatenate([k_dec_t_chunk[0][kc, :], k_dec_t_chunk[1][kc, :], attn],
                                       axis=0), v)
        st = states[h]
        for cc in range(2):
            crow = slice(cc * C, (cc + 1) * C)
            o = stacked[2 * GLA_DK + cc * C:2 * GLA_DK + (cc + 1) * C] + _dot(q_dec[crow, kc],
                                                                              st.astype(BF))
            o_ref[0, p * P + cc * C:p * P + (cc + 1) * C, vc] = o.astype(o_ref.dtype)
            last = (cc + 1) * C - 1
            st = decay_t[kc, last:last + 1] * (st + stacked[cc * GLA_DK:(cc + 1) * GLA_DK])
        new_states.append(st)
    return new_states


def _gla_body(gq_ref, gkt_ref, gv_ref, gd_ref, up_ref, gb_ref, tril2_ref, o_ref, state_ref):
    @pl.when(pl.program_id(1) == 0)
    def _():
        state_ref[...] = jnp.zeros_like(state_ref)

    bs = _gla_cum_decay(gd_ref, up_ref, gb_ref, tril2_ref)
    states = [state_ref[h] for h in range(GLA_HEADS)]
    for p in range(GLA_BLOCK // GLA_PAIR):
        states = _gla_pair(p, bs[p], states, gq_ref, gkt_ref, gv_ref, o_ref)
    for h in range(GLA_HEADS):
        state_ref[h] = states[h]


def _pair_tril2():
    i = np.arange(GLA_PAIR)
    t = ((i[:, None] // GLA_CHUNK) == (i[None, :] // GLA_CHUNK)) & (i[:, None] >= i[None, :])
    return np.concatenate([t, t], axis=1).astype(np.float32)


def _gla_call(gq, gkt, gv, gd, up, gbias):
    def row(w):
        return pl.BlockSpec((1, GLA_BLOCK, w), lambda b, i: (b, i, 0))

    consts = (up, gbias, jnp.asarray(_pair_tril2(), BF))
    return pl.pallas_call(
        _gla_body,
        grid=(BATCH, SEQ // GLA_BLOCK),
        in_specs=[row(GLA_KEY_DIM),
                  pl.BlockSpec((1, GLA_KEY_DIM, GLA_BLOCK), lambda b, i: (b, 0, i)),
                  row(GLA_VAL_DIM), row(GD_PAD)]
                 + [_const_spec(a.shape) for a in consts],
        out_specs=row(GLA_VAL_DIM),
        out_shape=jax.ShapeDtypeStruct((BATCH, SEQ, GLA_VAL_DIM), BF),
        scratch_shapes=[pltpu.VMEM((GLA_HEADS, GLA_DK, GLA_DV), F32)],
        compiler_params=_params("parallel", "arbitrary"),
        name="gla",
    )(gq, gkt, gv, gd, *consts)


def _undilate(src_ref, scr_ref, d):
    if d == 1:
        return src_ref[0, 0].astype(F32)
    planes = scr_ref.shape[0]
    for r in range(d):
        x = src_ref[0, r].astype(F32)
        for j in range(planes):
            scr_ref[j, pl.ds(r, ROW_TILE // d, stride=d), :] = x[:, j * 128:(j + 1) * 128]
    return jnp.concatenate([scr_ref[j] for j in range(planes)], axis=-1)


def _merge_body(h_ref, mod_ref, o0_ref, o1_ref, o2_ref, l0_ref, l1_ref, l2_ref,
                ogla_ref, gr_ref, gate_ref, hexp_ref, gain_ref, wba_ref, wbg_ref, wout_ref,
                g_ref, s1_ref, s3_ref, w1_ref, w3_ref, w2_ref, out_ref,
                os1_ref, os2_ref, ls1_ref, ls2_ref):
    dils = [d for _, d in ATT_GROUPS]
    lses = [_undilate(r, s, d) for r, s, d in
            zip((l0_ref, l1_ref, l2_ref), (None, ls1_ref, ls2_ref), dils)]
    outs = [_undilate(r, s, d) for r, s, d in
            zip((o0_ref, o1_ref, o2_ref), (None, os1_ref, os2_ref), dils)]
    mx = jnp.maximum(jnp.maximum(lses[0], lses[1]), lses[2])
    es = [jnp.exp(l - mx) for l in lses]
    den = es[0] + es[1] + es[2]
    o_att = None
    for e, o in zip(es, outs):
        t = _dot_split(e / den, hexp_ref) * o
        o_att = t if o_att is None else o_att + t
    att = _dot(o_att.astype(BF), wba_ref[...])
    gated = []
    for h in range(GLA_HEADS):
        vc = slice(h * GLA_DV, (h + 1) * GLA_DV)
        o = ogla_ref[0, :, vc].astype(F32)
        ms = jnp.mean(o * o, axis=-1, keepdims=True)
        y = o * lax.rsqrt(ms + EPS) * gain_ref[:, vc]
        gated.append(y.astype(BF) * _silu(gr_ref[0, :, vc]))
    gla = _dot(jnp.concatenate(gated, axis=1), wbg_ref[...])
    merged = (_sigmoid(gate_ref[0, :, :D_MODEL]) * att.astype(BF)
              + _sigmoid(gate_ref[0, :, D_MODEL:]) * gla.astype(BF))
    m = mod_ref[0]
    h = h_ref[0] + (1.0 + m[5:6]) * _dot(merged, wout_ref[...])
    out_ref[0] = _ffn_tiles([h], m, 6, g_ref, s1_ref, s3_ref, w1_ref, w3_ref, w2_ref)[0]


def _merge_ffn_call(h, mod, os_, ls_, ogla, gr, gates, hexp, gain, wba, wbg, wout, g, w1, w3, w2):
    def row(w):
        return pl.BlockSpec((1, ROW_TILE, w), lambda b, i: (b, i, 0))

    s1, s3 = _shiftproj_call(mod, 6, (w1, w3), "ffn6_shift")
    per_batch = pl.BlockSpec((1, 1, D_FF), lambda b, i: (b, 0, 0))
    consts = (hexp, gain, wba, wbg, wout, g)
    weights = (w1, w3, w2)
    dils = [d for _, d in ATT_GROUPS]
    return pl.pallas_call(
        _merge_body,
        grid=(BATCH, SEQ // ROW_TILE),
        in_specs=[row(D_MODEL), pl.BlockSpec((1, N_MOD, D_MODEL), lambda b, i: (b, 0, 0))]
                 + [_dilated_spec(d, ATT_OUT_WIDTH) for d in dils]
                 + [_dilated_spec(d, 128) for d in dils]
                 + [row(GLA_VAL_DIM), row(GLA_VAL_DIM), row(2 * D_MODEL)]
                 + [_const_spec(a.shape) for a in consts] + [per_batch, per_batch]
                 + [_const_spec(a.shape) for a in weights],
        out_specs=row(D_MODEL),
        out_shape=jax.ShapeDtypeStruct((BATCH, SEQ, D_MODEL), F32),
        scratch_shapes=[pltpu.VMEM((ATT_OUT_WIDTH // 128, ROW_TILE, 128), F32)] * 2
                       + [pltpu.VMEM((1, ROW_TILE, 128), F32)] * 2,
        compiler_params=_params("parallel", "parallel"),
        name="merge_ffn",
    )(h, mod, *os_, *ls_, ogla, gr, gates, *consts, s1, s3, *weights)


def _head_matrix(n_lanes_in, width):
    j = np.arange(n_lanes_in)[:, None]
    c = np.arange(width)[None, :]
    return (c // ATT_HEAD_DIM == j).astype(np.float32)


def _layer(h, mod, g_ffn1, f1w1, f1w3, f1w2, g_mix, w_in, q_norm_g, k_norm_g, gate_up, gate_bias,
           out_norm_g, w_branch_att, w_branch_gla, w_out, g_ffn2, f2w1, f2w3, f2w2):
    bf = lambda w: w.astype(BF)
    row = lambda v: v.reshape(1, -1)

    h = _ffn_call(h, mod, row(g_ffn1), bf(f1w1), bf(f1w3), bf(f1w2), 0)

    splits = np.cumsum((ATT_WIDTH, ATT_WIDTH, ATT_WIDTH, GLA_KEY_DIM, GLA_KEY_DIM, GLA_VAL_DIM,
                        GLA_VAL_DIM, GLA_GATE_RANK))
    aq, ak, av, gq, gk, gv, gr, gdn, gates = jnp.split(w_in, [int(s) for s in splits], axis=1)
    W = ATT_OUT_WIDTH
    watt = bf(jnp.concatenate([t[:, gi * W:(gi + 1) * W] for gi in range(len(ATT_GROUPS))
                               for t in (aq, ak, av)], axis=1))
    gq, gk, gv, gr, gates = bf(gq), bf(gk), bf(gv), bf(gr), bf(gates)
    wgd = bf(jnp.pad(gdn, ((0, 0), (0, GD_PAD - GLA_GATE_RANK))))
    head_qk = _head_matrix(128, QK_W)
    hsum_qk = jnp.asarray(head_qk.T, BF)
    hexp_qk = jnp.asarray(np.concatenate([head_qk, head_qk]), BF)
    qkg = jnp.concatenate([jnp.tile(q_norm_g * (ATT_HEAD_DIM ** -0.5 * LOG2E), ATT_HEADS_PER_GROUP),
                           jnp.tile(k_norm_g, ATT_HEADS_PER_GROUP)])
    a0, a1, a2, pgq, pgkt, pgv, pgr, pgd, pgate = _proj_call(
        h, mod, row(g_mix), watt, gq, gk.T, gv, gr, wgd, gates, hsum_qk, hexp_qk, row(qkg))

    outs = [_attn_call(a, gi) for gi, a in enumerate((a0, a1, a2))]
    up = jnp.pad(bf(gate_up), ((0, GD_PAD - GLA_GATE_RANK), (0, 0)))
    ogla = _gla_call(pgq, pgkt, pgv, pgd, up, row(gate_bias))

    head_o = _head_matrix(128, ATT_OUT_WIDTH)
    hexp_o = jnp.asarray(np.concatenate([head_o, head_o]), BF)
    return _merge_ffn_call(h, mod, [o for o, _ in outs], [l for _, l in outs], ogla, pgr, pgate,
                           hexp_o, row(jnp.tile(out_norm_g, GLA_HEADS)), bf(w_branch_att),
                           bf(w_branch_gla), bf(w_out), row(g_ffn2), bf(f2w1), bf(f2w3), bf(f2w2))


def kernel(x, c, w_mod, b_mod, g_ffn1, ffn1_w1, ffn1_w3, ffn1_w2, g_mix, w_in, q_norm_g, k_norm_g,
           gla_gate_up, gla_gate_bias, gla_out_norm_g, w_branch_att, w_branch_gla, w_out,
           g_ffn2, ffn2_w1, ffn2_w3, ffn2_w2):
    h = x
    for l in range(w_mod.shape[0]):
        mod = _mod_call(c, w_mod, b_mod[l], l).reshape(BATCH, N_MOD, D_MODEL)
        h = _layer(h, mod, g_ffn1[l], ffn1_w1[l], ffn1_w3[l], ffn1_w2[l], g_mix[l], w_in[l],
                   q_norm_g[l], k_norm_g[l], gla_gate_up[l], gla_gate_bias[l], gla_out_norm_g[l],
                   w_branch_att[l], w_branch_gla[l], w_out[l], g_ffn2[l], ffn2_w1[l], ffn2_w3[l],
                   ffn2_w2[l])
    return h
```

```python
import functools

import numpy as np
import jax
import jax.numpy as jnp
from jax import lax
from jax.experimental import pallas as pl
from jax.experimental.pallas import tpu as pltpu

D_MODEL = 1024
BATCH = 16
SEQ = 2048
N_MOD = 9
D_FF = 2816
EPS = 1e-6
ATT_GROUPS = ((128, 1), (512, 4), (2048, 16))
ATT_HEADS_PER_GROUP = 4
ATT_HEADS = 12
ATT_HEAD_DIM = 64
ATT_WIDTH = 768
ATT_OUT_WIDTH = 256
ALIBI_MAX = 8.0
GLA_HEADS = 4
GLA_KEY_DIM = 512
GLA_VAL_DIM = 1024
GLA_DK = 128
GLA_DV = 256
GLA_GATE_RANK = 16
GLA_TAU = 16.0
GLA_CHUNK = 64
ATT_BLK = 128
NEG_BIG = -1e30
LOG2E = float(np.log2(np.e))
LN2 = float(np.log(2.0))

BF = jnp.bfloat16
F32 = jnp.float32

VMEM_LIMIT_BYTES = 56 * 1024 * 1024

ROW_TILE = 512
FFN_TILE = 1024
FFN_SUBTILES = 2
MXU_TILE = 256
FF_CHUNK_TILES = 4
GLA_BLOCK = 512


def _dot(a, b):
    return jnp.dot(a, b, preferred_element_type=F32)


def _dot_nt(a, b):
    return lax.dot_general(a, b, (((1,), (1,)), ((), ())), preferred_element_type=F32)


def _dot_tn(a, b):
    return lax.dot_general(a, b, (((0,), (0,)), ((), ())), preferred_element_type=F32)


def _split_bf16(x):
    hi = x.astype(BF)
    lo = (x - hi.astype(F32)).astype(BF)
    return hi, lo


def _dot_split(x, stacked_ref):
    hi, lo = _split_bf16(x)
    return _dot(jnp.concatenate([hi, lo], axis=1), stacked_ref[...])


def _sigmoid(x):
    return 0.5 * jnp.tanh(0.5 * x) + 0.5


def _silu(x):
    return x * _sigmoid(x)


def _const_spec(shape):
    nd = len(shape)
    return pl.BlockSpec(shape, lambda *_: (0,) * nd, pipeline_mode=pl.Buffered(1))


def _dilated_spec(d, width):
    return pl.BlockSpec((1, d, ROW_TILE // d, width), lambda b, i: (b, 0, i, 0))


def _params(*sem):
    return pltpu.CompilerParams(dimension_semantics=sem, vmem_limit_bytes=VMEM_LIMIT_BYTES)


def _mod_body(c_ref, w_ref, b_ref, o_ref):
    c = c_ref[...]
    o_ref[...] = _dot(_silu(c).astype(BF), w_ref[...].astype(BF)) + b_ref[...]


def _mod_call(c, w_mod, b_mod, layer):
    n = w_mod.shape[2]
    bn = D_MODEL
    return pl.pallas_call(
        _mod_body,
        grid=(n // bn,),
        in_specs=[pl.BlockSpec((BATCH, D_MODEL), lambda j: (0, 0)),
                  pl.BlockSpec((None, D_MODEL, bn), lambda j: (layer, 0, j)),
                  pl.BlockSpec((1, bn), lambda j: (0, j))],
        out_specs=pl.BlockSpec((BATCH, bn), lambda j: (0, j)),
        out_shape=jax.ShapeDtypeStruct((BATCH, n), F32),
        compiler_params=_params("arbitrary"),
        name="mod",
    )(c, w_mod, b_mod.reshape(1, n))


def _shiftproj_body(mod_ref, *refs):
    n = len(refs) // 2
    sh = mod_ref[...].astype(BF)
    for w_ref, o_ref in zip(refs[:n], refs[n:]):
        o_ref[:, 0, :] = _dot(sh, w_ref[...])


def _shiftproj_call(mod, mod_row, weights, name):
    mod2 = mod.reshape(BATCH, N_MOD * D_MODEL)
    return pl.pallas_call(
        _shiftproj_body,
        grid=(1,),
        in_specs=[pl.BlockSpec((BATCH, D_MODEL), lambda i: (0, mod_row))]
                 + [_const_spec(w.shape) for w in weights],
        out_specs=[pl.BlockSpec((BATCH, 1, w.shape[1]), lambda i: (0, 0, 0)) for w in weights],
        out_shape=[jax.ShapeDtypeStruct((BATCH, 1, w.shape[1]), F32) for w in weights],
        compiler_params=_params("arbitrary"),
        name=name,
    )(mod2, *weights)


def _row_scale(x):
    return lax.rsqrt(jnp.mean(x * x, axis=-1, keepdims=True) + EPS)


WEIGHT_PIECE = 256

FFN_WEIGHT_SCRATCH = [
    pltpu.VMEM((D_MODEL, D_FF), BF), pltpu.VMEM((D_MODEL, D_FF), BF), pltpu.VMEM((D_FF, D_MODEL), BF),
    pltpu.VMEM((BATCH, D_FF), F32), pltpu.VMEM((BATCH, D_FF), F32),
    pltpu.VMEM((2, D_MODEL, WEIGHT_PIECE), F32), pltpu.VMEM((2, WEIGHT_PIECE, D_MODEL), F32),
    pltpu.SemaphoreType.DMA((2,)), pltpu.SemaphoreType.DMA((2,))]


def _stream_cast(pieces, stage_ref, sem_ref, store):
    def copy(i):
        return pltpu.make_async_copy(pieces[i], stage_ref.at[i % 2], sem_ref.at[i % 2])

    copy(0).start()
    for i in range(len(pieces)):
        if i + 1 < len(pieces):
            copy(i + 1).start()
        copy(i).wait()
        store(i, stage_ref[i % 2].astype(BF))


def _load_ffn_weights(layer, mod_row, modall_ref, w1_hbm, w3_hbm, w2_hbm,
                      w1_ref, w3_ref, w2_ref, s1_ref, s3_ref, stage_c, stage_r, sem_c, sem_r):
    n = D_FF // WEIGHT_PIECE
    span = [pl.ds(c * WEIGHT_PIECE, WEIGHT_PIECE) for c in range(n)]

    def store_cols(i, v):
        dst = w1_ref if i < n else w3_ref
        dst[:, span[i % n]] = v

    def store_rows(i, v):
        w2_ref[span[i], :] = v

    _stream_cast([w.at[layer, :, c] for w in (w1_hbm, w3_hbm) for c in span],
                 stage_c, sem_c, store_cols)
    _stream_cast([w2_hbm.at[layer, c, :] for c in span], stage_r, sem_r, store_rows)
    shift = modall_ref[:, mod_row, :].astype(BF)
    s1_ref[...] = _dot(shift, w1_ref[...])
    s3_ref[...] = _dot(shift, w3_ref[...])


def _first_step():
    return jnp.logical_and(pl.program_id(0) == 0, pl.program_id(1) == 0)


def _ffn_tiles(xs, m, mod_row, g_ref, s1, s3, w1_ref, w3_ref, w2_ref):
    scale, gate = m[mod_row + 1:mod_row + 2], m[mod_row + 2:mod_row + 3]
    fc = FF_CHUNK_TILES * MXU_TILE
    gs = g_ref[...] * (1.0 + scale)
    us = [(x * gs).astype(BF) for x in xs]
    rs = [_row_scale(x) for x in xs]
    accs = [None] * len(xs)
    for lo in range(0, D_FF, fc):
        cols = slice(lo, min(lo + fc, D_FF))
        for s, u in enumerate(us):
            h1 = rs[s] * _dot(u, w1_ref[:, cols]) + s1[:, cols]
            h3 = rs[s] * _dot(u, w3_ref[:, cols]) + s3[:, cols]
            p = _dot((_silu(h1) * h3).astype(BF), w2_ref[cols, :])
            accs[s] = p if accs[s] is None else accs[s] + p
    return [x + (0.5 * (1.0 + gate)) * acc for x, acc in zip(xs, accs)]


def _ffn_body(x_ref, mod_ref, modall_ref, g_ref, w1_hbm, w3_hbm, w2_hbm, o_ref,
              w1_ref, w3_ref, w2_ref, s1_ref, s3_ref, *dma_scratch, mod_row, layer):
    @pl.when(_first_step())
    def _():
        _load_ffn_weights(layer, mod_row, modall_ref, w1_hbm, w3_hbm, w2_hbm,
                          w1_ref, w3_ref, w2_ref, s1_ref, s3_ref, *dma_scratch)

    b = pl.program_id(0)
    sub = FFN_TILE // FFN_SUBTILES
    rows = [slice(s * sub, (s + 1) * sub) for s in range(FFN_SUBTILES)]
    outs = _ffn_tiles([x_ref[0, r, :] for r in rows], mod_ref[0], mod_row, g_ref,
                      s1_ref[pl.ds(b, 1), :], s3_ref[pl.ds(b, 1), :], w1_ref, w3_ref, w2_ref)
    for r, o in zip(rows, outs):
        o_ref[0, r, :] = o


def _ffn_call(x, mod, g, w1, w3, w2, mod_row, layer):
    row = pl.BlockSpec((1, FFN_TILE, D_MODEL), lambda b, i: (b, i, 0))
    hbm = pl.BlockSpec(memory_space=pl.ANY)
    return pl.pallas_call(
        functools.partial(_ffn_body, mod_row=mod_row, layer=layer),
        grid=(BATCH, SEQ // FFN_TILE),
        in_specs=[row,
                  pl.BlockSpec((1, N_MOD, D_MODEL), lambda b, i: (b, 0, 0)),
                  _const_spec((BATCH, N_MOD, D_MODEL)),
                  _const_spec((1, D_MODEL)),
                  hbm, hbm, hbm],
        out_specs=row,
        out_shape=jax.ShapeDtypeStruct((BATCH, SEQ, D_MODEL), F32),
        scratch_shapes=FFN_WEIGHT_SCRATCH,
        compiler_params=_params("arbitrary", "arbitrary"),
        name="ffn%d" % mod_row,
    )(x, mod, mod, g, w1, w3, w2)


QK_W = 2 * ATT_OUT_WIDTH
GRP_W = 3 * ATT_OUT_WIDTH
GD_PAD = 128


def _proj_body(h_ref, mod_ref, g_ref, satt_ref, sgq_ref, sgv_ref, sgr_ref, sgd_ref, sgate_ref,
               watt_ref, wgq_ref, wgkt_ref, wgv_ref, wgr_ref, wgd_ref,
               wgate_ref, hsum_ref, hexp_ref, qkg_ref,
               a0_ref, a1_ref, a2_ref, gq_ref, gkt_ref, gv_ref, gr_ref, gd_ref, gate_ref, perm_ref):
    m = mod_ref[0]
    x = h_ref[0]
    xg = x * (g_ref[...] * (1.0 + m[4:5]))
    u = xg.astype(BF)
    rs = _row_scale(x)

    def proj(w, s):
        return rs * _dot(u, w) + s

    for gi, a_ref in enumerate((a0_ref, a1_ref, a2_ref)):
        d = ATT_GROUPS[gi][1]
        gcols = slice(gi * GRP_W, (gi + 1) * GRP_W)
        y = proj(watt_ref[:, gcols], satt_ref[0, :, gcols])
        qk = y[:, :QK_W]
        ss = _dot((qk * qk).astype(BF), hsum_ref[...])
        rx = _dot_split(lax.rsqrt(ss * (1.0 / ATT_HEAD_DIM) + EPS), hexp_ref)
        qkn = qk * rx * qkg_ref[...]
        if d == 1:
            a_ref[0, 0, :, :QK_W] = qkn.astype(BF)
            a_ref[0, 0, :, QK_W:] = y[:, QK_W:].astype(BF)
        else:
            for j in range(GRP_W // 128):
                cols = slice(j * 128, (j + 1) * 128)
                perm_ref[j] = qkn[:, cols] if j < QK_W // 128 else y[:, cols]
            for r in range(d):
                for j in range(GRP_W // 128):
                    a_ref[0, r, :, j * 128:(j + 1) * 128] = (
                        perm_ref[j, pl.ds(r, ROW_TILE // d, stride=d), :].astype(BF))
    gq_ref[0] = proj(wgq_ref[...], sgq_ref[0]).astype(BF)
    gv_ref[0] = proj(wgv_ref[...], sgv_ref[0]).astype(BF)
    gr_ref[0] = proj(wgr_ref[...], sgr_ref[0]).astype(BF)
    gd_ref[0] = proj(wgd_ref[...], sgd_ref[0]).astype(BF)
    gate_ref[0] = proj(wgate_ref[...], sgate_ref[0]).astype(BF)
    u_full = (xg * rs + m[3:4]).astype(BF)
    gkt_ref[0] = _dot_nt(wgkt_ref[...], u_full).astype(BF)


def _proj_call(h, mod, g, watt, wgq, wgkt, wgv, wgr, wgd, wgate, hsum, hexp, qkg):
    def row(w):
        return pl.BlockSpec((1, ROW_TILE, w), lambda b, i: (b, i, 0))

    def out(w):
        return jax.ShapeDtypeStruct((BATCH, SEQ, w), BF)

    kt_spec = pl.BlockSpec((1, GLA_KEY_DIM, ROW_TILE), lambda b, i: (b, 0, i))
    kt_shape = jax.ShapeDtypeStruct((BATCH, GLA_KEY_DIM, SEQ), BF)
    widths = (GLA_VAL_DIM, GLA_VAL_DIM, GD_PAD, 2 * D_MODEL)
    row_weights = (watt, wgq, wgv, wgr, wgd, wgate)
    shifts = _shiftproj_call(mod, 3, row_weights, "proj_shift")
    consts = (watt, wgq, wgkt, wgv, wgr, wgd, wgate, hsum, hexp, qkg)
    dils = [d for _, d in ATT_GROUPS]
    return pl.pallas_call(
        _proj_body,
        grid=(BATCH, SEQ // ROW_TILE),
        in_specs=[row(D_MODEL), pl.BlockSpec((1, N_MOD, D_MODEL), lambda b, i: (b, 0, 0)),
                  _const_spec(g.shape)]
                 + [pl.BlockSpec((1, 1, w.shape[1]), lambda b, i: (b, 0, 0)) for w in row_weights]
                 + [_const_spec(a.shape) for a in consts],
        out_specs=[_dilated_spec(d, GRP_W) for d in dils] + [row(GLA_KEY_DIM), kt_spec]
                  + [row(w) for w in widths],
        out_shape=[jax.ShapeDtypeStruct((BATCH, d, SEQ // d, GRP_W), BF) for d in dils]
                  + [out(GLA_KEY_DIM), kt_shape] + [out(w) for w in widths],
        scratch_shapes=[pltpu.VMEM((GRP_W // 128, ROW_TILE, 128), F32)],
        compiler_params=_params("parallel", "parallel"),
        name="proj",
    )(h, mod, g, *shifts, *consts)


ATT_GROUP_BLOCKS = 4


def _attn_blocks(a_ref, bias_ref, o_ref, l_ref, blocks):
    W, NH = ATT_OUT_WIDTH, ATT_HEADS_PER_GROUP
    lane = lax.broadcasted_iota(jnp.int32, (ATT_BLK, 128), 1)
    low_half = lane < ATT_HEAD_DIM
    zero_col = jnp.zeros((ATT_BLK, 128), BF)
    qs, ks, vs, biases, q0s = [], [], [], [], []
    for r, n, first in blocks:
        if first:
            q0, k0, nk = 0, 0, ATT_BLK
            biases.append(bias_ref[:, ATT_BLK:])
        else:
            q0, k0, nk = n * ATT_BLK, (n - 1) * ATT_BLK, 2 * ATT_BLK
            if not isinstance(n, int):
                q0, k0 = pl.multiple_of(q0, ATT_BLK), pl.multiple_of(k0, ATT_BLK)
            biases.append(bias_ref[...])
        q0s.append(q0)
        stacked = []
        for j in range(NH):
            col = a_ref[0, r, pl.ds(q0, ATT_BLK), (j // 2) * 128:(j // 2 + 1) * 128]
            col = jnp.where(low_half == (j % 2 == 0), col, zero_col)
            stacked.append(jnp.concatenate([col, zero_col] if j < 2 else [zero_col, col], axis=1))
        qs.append(jnp.concatenate(stacked, axis=0))
        ks.append(a_ref[0, r, pl.ds(k0, nk), W:2 * W])
        vs.append(a_ref[0, r, pl.ds(k0, nk), 2 * W:])
    ss = [_dot_nt(q4, k) + b for q4, k, b in zip(qs, ks, biases)]
    mxs = [jnp.max(s, axis=-1, keepdims=True) for s in ss]
    es = [jnp.exp2(s - mx) for s, mx in zip(ss, mxs)]
    dens = [jnp.sum(e, axis=-1, keepdims=True) for e in es]
    pvs = [_dot(e.astype(BF), v) for e, v in zip(es, vs)]
    for (r, _, _), q0, pv, mx, den in zip(blocks, q0s, pvs, mxs, dens):
        rden = 1.0 / den
        lse = mx * LN2 + jnp.log(den)
        l = jnp.zeros((ATT_BLK, 128), F32)
        heads = []
        for j in range(NH):
            rows = slice(j * ATT_BLK, (j + 1) * ATT_BLK)
            heads.append(pv[rows, (j // 2) * 128:(j // 2 + 1) * 128] * rden[rows])
            l = jnp.where(lane == j, lse[rows], l)
        o = jnp.concatenate([jnp.where(low_half, heads[0], heads[1]),
                             jnp.where(low_half, heads[2], heads[3])], axis=1)
        o_ref[0, r, pl.ds(q0, ATT_BLK), :] = o.astype(o_ref.dtype)
        l_ref[0, r, pl.ds(q0, ATT_BLK), :] = l


def _attn_body(a_ref, bias_ref, o_ref, l_ref, *, subseqs, n_blocks):
    G = ATT_GROUP_BLOCKS
    if n_blocks == 1:
        for r0 in range(0, subseqs, G):
            _attn_blocks(a_ref, bias_ref, o_ref, l_ref,
                         [(r, 0, True) for r in range(r0, min(r0 + G, subseqs))])
        return
    for r in range(subseqs):
        for n0 in range(0, n_blocks, G):
            _attn_blocks(a_ref, bias_ref, o_ref, l_ref,
                         [(r, n, n == 0) for n in range(n0, n0 + G)])


def _attn_bias(group):
    _, dilation = ATT_GROUPS[group]
    heads = np.arange(group * ATT_HEADS_PER_GROUP, (group + 1) * ATT_HEADS_PER_GROUP)
    slopes = (2.0 ** (-ALIBI_MAX * (heads + 1).astype(np.float32) / ATT_HEADS)).astype(np.float32)
    qi = np.arange(ATT_BLK)[:, None]
    kj = np.arange(2 * ATT_BLK)[None, :]
    steps = qi + ATT_BLK - kj
    valid = (steps >= 0) & (steps <= ATT_BLK)
    bias = -(slopes * np.float32(dilation))[:, None, None] * steps.astype(np.float32)[None]
    bias = np.where(valid[None], bias * np.float32(LOG2E), np.float32(NEG_BIG)).astype(np.float32)
    return bias.reshape(ATT_HEADS_PER_GROUP * ATT_BLK, 2 * ATT_BLK)


def _attn_call(a, group):
    _, d = ATT_GROUPS[group]
    L = SEQ // d
    n_blocks = L // ATT_BLK
    subseqs = d if n_blocks == 1 else min(d, 4)

    def spec(w):
        return pl.BlockSpec((1, subseqs, L, w), lambda b, r: (b, r, 0, 0))

    return pl.pallas_call(
        functools.partial(_attn_body, subseqs=subseqs, n_blocks=n_blocks),
        grid=(BATCH, d // subseqs),
        in_specs=[spec(GRP_W), _const_spec((ATT_HEADS_PER_GROUP * ATT_BLK, 2 * ATT_BLK))],
        out_specs=[spec(ATT_OUT_WIDTH), spec(128)],
        out_shape=[jax.ShapeDtypeStruct((BATCH, d, L, ATT_OUT_WIDTH), BF),
                   jax.ShapeDtypeStruct((BATCH, d, L, 128), F32)],
        compiler_params=_params("parallel", "parallel"),
        name="attn%d" % d,
    )(a, jnp.asarray(_attn_bias(group)))


GLA_PAIR = 2 * GLA_CHUNK


def _gla_cum_decay(gd_ref, up_ref, gb_ref, tril2_ref):
    x = _dot(gd_ref[0], up_ref[...]) + gb_ref[...]
    log_a = (jnp.minimum(x, 0.0) - jnp.log(1.0 + jnp.exp(-jnp.abs(x)))) * (1.0 / GLA_TAU)
    la_hi, la_lo = _split_bf16(log_a)
    pieces = []
    for p in range(GLA_BLOCK // GLA_PAIR):
        rows = slice(p * GLA_PAIR, (p + 1) * GLA_PAIR)
        pieces.append(_dot(tril2_ref[...], jnp.concatenate([la_hi[rows], la_lo[rows]], axis=0)))
    return pieces


def _gla_pair(p, b, states, gq_ref, gkt_ref, gv_ref, o_ref):
    C, P = GLA_CHUNK, GLA_PAIR
    rows = slice(p * P, (p + 1) * P)
    q_dec =(gq_ref[0, rows, :].astype(F32) * (GLA_DK ** -0.5) * jnp.exp(b)).astype(BF)
    b_t = b.T
    k_dec_t_bf = (gkt_ref[0, :, rows].astype(F32) * jnp.exp(-b_t)).astype(BF)
    decay_t = jnp.exp(b_t)
    lane = lax.broadcasted_iota(jnp.int32, (GLA_KEY_DIM, P), 1)
    zero = jnp.zeros_like(k_dec_t_bf)
    k_dec_t_chunk = [jnp.where(lane < C, k_dec_t_bf, zero), jnp.where(lane >= C, k_dec_t_bf, zero)]
    ri = lax.broadcasted_iota(jnp.int32, (P, P), 0)
    ci = lax.broadcasted_iota(jnp.int32, (P, P), 1)
    causal = (ri >= ci) & ((ri >= C) == (ci >= C))
    new_states = []
    for h in range(GLA_HEADS):
        kc = slice(h * GLA_DK, (h + 1) * GLA_DK)
        vc = slice(h * GLA_DV, (h + 1) * GLA_DV)
        v = gv_ref[0, rows, vc]
        attn = jnp.where(causal, _dot(q_dec[:, kc], k_dec_t_bf[kc, :]), 0.0).astype(BF)
        stacked = _dot(jnp.concatenate([k_dec_t_chunk[0][kc, :], k_dec_t_chunk[1][kc, :], attn],
                                       axis=0), v)
        st = states[h]
        for cc in range(2):
            crow = slice(cc * C, (cc + 1) * C)
            o = stacked[2 * GLA_DK + cc * C:2 * GLA_DK + (cc + 1) * C] + _dot(q_dec[crow, kc],
                                                                              st.astype(BF))
            o_ref[0, p * P + cc * C:p * P + (cc + 1) * C, vc] = o.astype(o_ref.dtype)
            last = (cc + 1) * C - 1
            st = decay_t[kc, last:last + 1] * (st + stacked[cc * GLA_DK:(cc + 1) * GLA_DK])
        new_states.append(st)
    return new_states


def _gla_body(gq_ref, gkt_ref, gv_ref, gd_ref, up_ref, gb_ref, tril2_ref, o_ref, state_ref):
    @pl.when(pl.program_id(1) == 0)
    def _():
        state_ref[...] = jnp.zeros_like(state_ref)

    bs = _gla_cum_decay(gd_ref, up_ref, gb_ref, tril2_ref)
    states = [state_ref[h] for h in range(GLA_HEADS)]
    for p in range(GLA_BLOCK // GLA_PAIR):
        states = _gla_pair(p, bs[p], states, gq_ref, gkt_ref, gv_ref, o_ref)
    for h in range(GLA_HEADS):
        state_ref[h] = states[h]


def _pair_tril2():
    i = np.arange(GLA_PAIR)
    t = ((i[:, None] // GLA_CHUNK) == (i[None, :] // GLA_CHUNK)) & (i[:, None] >= i[None, :])
    return np.concatenate([t, t], axis=1).astype(np.float32)


def _gla_call(gq, gkt, gv, gd, up, gbias):
    def row(w):
        return pl.BlockSpec((1, GLA_BLOCK, w), lambda b, i: (b, i, 0))

    consts = (up, gbias, jnp.asarray(_pair_tril2(), BF))
    return pl.pallas_call(
        _gla_body,
        grid=(BATCH, SEQ // GLA_BLOCK),
        in_specs=[row(GLA_KEY_DIM),
                  pl.BlockSpec((1, GLA_KEY_DIM, GLA_BLOCK), lambda b, i: (b, 0, i)),
                  row(GLA_VAL_DIM), row(GD_PAD)]
                 + [_const_spec(a.shape) for a in consts],
        out_specs=row(GLA_VAL_DIM),
        out_shape=jax.ShapeDtypeStruct((BATCH, SEQ, GLA_VAL_DIM), BF),
        scratch_shapes=[pltpu.VMEM((GLA_HEADS, GLA_DK, GLA_DV), F32)],
        compiler_params=_params("parallel", "arbitrary"),
        name="gla",
    )(gq, gkt, gv, gd, *consts)


def _undilate(src_ref, scr_ref, d):
    if d == 1:
        return src_ref[0, 0].astype(F32)
    planes = scr_ref.shape[0]
    for r in range(d):
        x = src_ref[0, r].astype(F32)
        for j in range(planes):
            scr_ref[j, pl.ds(r, ROW_TILE // d, stride=d), :] = x[:, j * 128:(j + 1) * 128]
    return jnp.concatenate([scr_ref[j] for j in range(planes)], axis=-1)


def _merge_body(h_ref, mod_ref, o0_ref, o1_ref, o2_ref, l0_ref, l1_ref, l2_ref,
                ogla_ref, gr_ref, gate_ref, modall_ref, hexp_ref, gain_ref, wba_ref, wbg_ref,
                wout_ref, g_ref, w1_hbm, w3_hbm, w2_hbm, out_ref,
                os1_ref, os2_ref, ls1_ref, ls2_ref,
                w1_ref, w3_ref, w2_ref, s1_ref, s3_ref, *dma_scratch, layer):
    @pl.when(_first_step())
    def _():
        _load_ffn_weights(layer, 6, modall_ref, w1_hbm, w3_hbm, w2_hbm,
                          w1_ref, w3_ref, w2_ref, s1_ref, s3_ref, *dma_scratch)

    dils = [d for _, d in ATT_GROUPS]
    lses = [_undilate(r, s, d) for r, s, d in
            zip((l0_ref, l1_ref, l2_ref), (None, ls1_ref, ls2_ref), dils)]
    outs = [_undilate(r, s, d) for r, s, d in
            zip((o0_ref, o1_ref, o2_ref), (None, os1_ref, os2_ref), dils)]
    mx = jnp.maximum(jnp.maximum(lses[0], lses[1]), lses[2])
    es = [jnp.exp(l - mx) for l in lses]
    den = es[0] + es[1] + es[2]
    o_att = None
    for e, o in zip(es, outs):
        t = _dot_split(e / den, hexp_ref) * o
        o_att = t if o_att is None else o_att + t
    att = _dot(o_att.astype(BF), wba_ref[...])
    gated = []
    for h in range(GLA_HEADS):
        vc = slice(h * GLA_DV, (h + 1) * GLA_DV)
        o = ogla_ref[0, :, vc].astype(F32)
        ms = jnp.mean(o * o, axis=-1, keepdims=True)
        y = o * lax.rsqrt(ms + EPS) * gain_ref[:, vc]
        gated.append(y.astype(BF) * _silu(gr_ref[0, :, vc]))
    gla = _dot(jnp.concatenate(gated, axis=1), wbg_ref[...])
    merged = (_sigmoid(gate_ref[0, :, :D_MODEL]) * att.astype(BF)
              + _sigmoid(gate_ref[0, :, D_MODEL:]) * gla.astype(BF))
    m = mod_ref[0]
    h = h_ref[0] + (1.0 + m[5:6]) * _dot(merged, wout_ref[...])
    b = pl.program_id(0)
    out_ref[0] = _ffn_tiles([h], m, 6, g_ref, s1_ref[pl.ds(b, 1), :], s3_ref[pl.ds(b, 1), :],
                            w1_ref, w3_ref, w2_ref)[0]


def _merge_ffn_call(h, mod, os_, ls_, ogla, gr, gates, hexp, gain, wba, wbg, wout, g, w1, w3, w2,
                    layer):
    def row(w):
        return pl.BlockSpec((1, ROW_TILE, w), lambda b, i: (b, i, 0))

    consts = (mod, hexp, gain, wba, wbg, wout, g)
    hbm = pl.BlockSpec(memory_space=pl.ANY)
    dils = [d for _, d in ATT_GROUPS]
    return pl.pallas_call(
        functools.partial(_merge_body, layer=layer),
        grid=(BATCH, SEQ // ROW_TILE),
        in_specs=[row(D_MODEL), pl.BlockSpec((1, N_MOD, D_MODEL), lambda b, i: (b, 0, 0))]
                 + [_dilated_spec(d, ATT_OUT_WIDTH) for d in dils]
                 + [_dilated_spec(d, 128) for d in dils]
                 + [row(GLA_VAL_DIM), row(GLA_VAL_DIM), row(2 * D_MODEL)]
                 + [_const_spec(a.shape) for a in consts] + [hbm, hbm, hbm],
        out_specs=row(D_MODEL),
        out_shape=jax.ShapeDtypeStruct((BATCH, SEQ, D_MODEL), F32),
        scratch_shapes=[pltpu.VMEM((ATT_OUT_WIDTH // 128, ROW_TILE, 128), F32)] * 2
                       + [pltpu.VMEM((1, ROW_TILE, 128), F32)] * 2 + FFN_WEIGHT_SCRATCH,
        compiler_params=_params("arbitrary", "arbitrary"),
        name="merge_ffn",
    )(h, mod, *os_, *ls_, ogla, gr, gates, *consts, w1, w3, w2)


def _head_matrix(n_lanes_in, width):
    j = np.arange(n_lanes_in)[:, None]
    c = np.arange(width)[None, :]
    return (c // ATT_HEAD_DIM == j).astype(np.float32)


def _layer(l, h, mod, g_ffn1, f1w1, f1w3, f1w2, g_mix, w_in, q_norm_g, k_norm_g, gate_up, gate_bias,
           out_norm_g, w_branch_att, w_branch_gla, w_out, g_ffn2, f2w1, f2w3, f2w2):
    bf = lambda w: w.astype(BF)
    row = lambda v: v.reshape(1, -1)

    h = _ffn_call(h, mod, row(g_ffn1), f1w1, f1w3, f1w2, 0, l)

    splits = np.cumsum((ATT_WIDTH, ATT_WIDTH, ATT_WIDTH, GLA_KEY_DIM, GLA_KEY_DIM, GLA_VAL_DIM,
                        GLA_VAL_DIM, GLA_GATE_RANK))
    aq, ak, av, gq, gk, gv, gr, gdn, gates = jnp.split(w_in, [int(s) for s in splits], axis=1)
    W = ATT_OUT_WIDTH
    watt = bf(jnp.concatenate([t[:, gi * W:(gi + 1) * W] for gi in range(len(ATT_GROUPS))
                               for t in (aq, ak, av)], axis=1))
    gq, gk, gv, gr, gates = bf(gq), bf(gk), bf(gv), bf(gr), bf(gates)
    wgd = bf(jnp.pad(gdn, ((0, 0), (0, GD_PAD - GLA_GATE_RANK))))
    head_qk = _head_matrix(128, QK_W)
    hsum_qk = jnp.asarray(head_qk.T, BF)
    hexp_qk = jnp.asarray(np.concatenate([head_qk, head_qk]), BF)
    qkg = jnp.concatenate([jnp.tile(q_norm_g * (ATT_HEAD_DIM ** -0.5 * LOG2E), ATT_HEADS_PER_GROUP),
                           jnp.tile(k_norm_g, ATT_HEADS_PER_GROUP)])
    a0, a1, a2, pgq, pgkt, pgv, pgr, pgd, pgate = _proj_call(
        h, mod, row(g_mix), watt, gq, gk.T, gv, gr, wgd, gates, hsum_qk, hexp_qk, row(qkg))

    outs = [_attn_call(a, gi) for gi, a in enumerate((a0, a1, a2))]
    up = jnp.pad(bf(gate_up), ((0, GD_PAD - GLA_GATE_RANK), (0, 0)))
    ogla = _gla_call(pgq, pgkt, pgv, pgd, up, row(gate_bias))

    head_o = _head_matrix(128, ATT_OUT_WIDTH)
    hexp_o = jnp.asarray(np.concatenate([head_o, head_o]), BF)
    return _merge_ffn_call(h, mod, [o for o, _ in outs], [l for _, l in outs], ogla, pgr, pgate,
                           hexp_o, row(jnp.tile(out_norm_g, GLA_HEADS)), bf(w_branch_att),
                           bf(w_branch_gla), bf(w_out), row(g_ffn2), f2w1, f2w3, f2w2, l)


def kernel(x, c, w_mod, b_mod, g_ffn1, ffn1_w1, ffn1_w3, ffn1_w2, g_mix, w_in, q_norm_g, k_norm_g,
           gla_gate_up, gla_gate_bias, gla_out_norm_g, w_branch_att, w_branch_gla, w_out,
           g_ffn2, ffn2_w1, ffn2_w3, ffn2_w2):
    h = x
    for l in range(w_mod.shape[0]):
        mod = _mod_call(c, w_mod, b_mod[l], l).reshape(BATCH, N_MOD, D_MODEL)
        h = _layer(l, h, mod, g_ffn1[l], ffn1_w1, ffn1_w3, ffn1_w2, g_mix[l], w_in[l],
                   q_norm_g[l], k_norm_g[l], gla_gate_up[l], gla_gate_bias[l], gla_out_norm_g[l],
                   w_branch_att[l], w_branch_gla[l], w_out[l], g_ffn2[l], ffn2_w1, ffn2_w3,
                   ffn2_w2)
    return h
```

```python
import functools

import numpy as np
import jax
import jax.numpy as jnp
from jax import lax
from jax.experimental import pallas as pl
from jax.experimental.pallas import tpu as pltpu

D_MODEL = 1024
BATCH = 16
SEQ = 2048
N_MOD = 9
D_FF = 2816
EPS = 1e-6
ATT_GROUPS = ((128, 1), (512, 4), (2048, 16))
ATT_HEADS_PER_GROUP = 4
ATT_HEADS = 12
ATT_HEAD_DIM = 64
ATT_WIDTH = 768
ATT_OUT_WIDTH = 256
ALIBI_MAX = 8.0
GLA_HEADS = 4
GLA_KEY_DIM = 512
GLA_VAL_DIM = 1024
GLA_DK = 128
GLA_DV = 256
GLA_GATE_RANK = 16
GLA_TAU = 16.0
GLA_CHUNK = 64
ATT_BLK = 128
NEG_BIG = -1e30
LOG2E = float(np.log2(np.e))
LN2 = float(np.log(2.0))

BF = jnp.bfloat16
F32 = jnp.float32

VMEM_LIMIT_BYTES = 56 * 1024 * 1024

ROW_TILE = 512
FFN_TILE = 1024
FFN_SUBTILES = 2
MXU_TILE = 256
FF_CHUNK_TILES = 4
GLA_BLOCK = 512


def _dot(a, b):
    return jnp.dot(a, b, preferred_element_type=F32)


def _dot_nt(a, b):
    return lax.dot_general(a, b, (((1,), (1,)), ((), ())), preferred_element_type=F32)


def _dot_tn(a, b):
    return lax.dot_general(a, b, (((0,), (0,)), ((), ())), preferred_element_type=F32)


def _split_bf16(x):
    hi = x.astype(BF)
    lo = (x - hi.astype(F32)).astype(BF)
    return hi, lo


def _dot_split(x, stacked_ref):
    hi, lo = _split_bf16(x)
    return _dot(jnp.concatenate([hi, lo], axis=1), stacked_ref[...])


def _sigmoid(x):
    return 0.5 * jnp.tanh(0.5 * x) + 0.5


def _silu(x):
    return x * _sigmoid(x)


def _const_spec(shape):
    nd = len(shape)
    return pl.BlockSpec(shape, lambda *_: (0,) * nd, pipeline_mode=pl.Buffered(1))


def _dilated_spec(d, width):
    return pl.BlockSpec((1, d, ROW_TILE // d, width), lambda b, i: (b, 0, i, 0))


def _params(*sem):
    return pltpu.CompilerParams(dimension_semantics=sem, vmem_limit_bytes=VMEM_LIMIT_BYTES)


def _mod_body(c_ref, w_ref, b_ref, o_ref):
    c = c_ref[...]
    o_ref[...] = _dot(_silu(c).astype(BF), w_ref[...].astype(BF)) + b_ref[...]


def _mod_call(c, w_mod, b_mod, layer):
    n = w_mod.shape[2]
    bn = D_MODEL
    return pl.pallas_call(
        _mod_body,
        grid=(n // bn,),
        in_specs=[pl.BlockSpec((BATCH, D_MODEL), lambda j: (0, 0)),
                  pl.BlockSpec((None, D_MODEL, bn), lambda j: (layer, 0, j)),
                  pl.BlockSpec((1, bn), lambda j: (0, j))],
        out_specs=pl.BlockSpec((BATCH, bn), lambda j: (0, j)),
        out_shape=jax.ShapeDtypeStruct((BATCH, n), F32),
        compiler_params=_params("arbitrary"),
        name="mod",
    )(c, w_mod, b_mod.reshape(1, n))


def _shiftproj_body(mod_ref, *refs):
    n = len(refs) // 2
    sh = mod_ref[...].astype(BF)
    for w_ref, o_ref in zip(refs[:n], refs[n:]):
        o_ref[:, 0, :] = _dot(sh, w_ref[...])


def _shiftproj_call(mod, mod_row, weights, name):
    mod2 = mod.reshape(BATCH, N_MOD * D_MODEL)
    return pl.pallas_call(
        _shiftproj_body,
        grid=(1,),
        in_specs=[pl.BlockSpec((BATCH, D_MODEL), lambda i: (0, mod_row))]
                 + [_const_spec(w.shape) for w in weights],
        out_specs=[pl.BlockSpec((BATCH, 1, w.shape[1]), lambda i: (0, 0, 0)) for w in weights],
        out_shape=[jax.ShapeDtypeStruct((BATCH, 1, w.shape[1]), F32) for w in weights],
        compiler_params=_params("arbitrary"),
        name=name,
    )(mod2, *weights)


def _row_scale(x):
    return lax.rsqrt(jnp.mean(x * x, axis=-1, keepdims=True) + EPS)


UP_PIECE_ROWS = 128
DOWN_PIECE_ROWS = 256

FFN_WEIGHT_SCRATCH = [
    pltpu.VMEM((D_MODEL, D_FF), BF), pltpu.VMEM((D_MODEL, D_FF), BF), pltpu.VMEM((D_FF, D_MODEL), BF),
    pltpu.VMEM((BATCH, D_FF), F32), pltpu.VMEM((BATCH, D_FF), F32),
    pltpu.VMEM((2, UP_PIECE_ROWS, D_FF), F32), pltpu.VMEM((2, DOWN_PIECE_ROWS, D_MODEL), F32),
    pltpu.SemaphoreType.DMA((2,)), pltpu.SemaphoreType.DMA((2,))]


def _stream_cast(pieces, stage_ref, sem_ref, store):
    def copy(i):
        return pltpu.make_async_copy(pieces[i], stage_ref.at[i % 2], sem_ref.at[i % 2])

    copy(0).start()
    for i in range(len(pieces)):
        if i + 1 < len(pieces):
            copy(i + 1).start()
        copy(i).wait()
        store(i, stage_ref[i % 2].astype(BF))


def _load_ffn_weights(layer, mod_row, modall_ref, w1_hbm, w3_hbm, w2_hbm,
                      w1_ref, w3_ref, w2_ref, s1_ref, s3_ref, stage_up, stage_dn, sem_up, sem_dn):
    up_rows = [pl.ds(r, UP_PIECE_ROWS) for r in range(0, D_MODEL, UP_PIECE_ROWS)]
    dn_rows = [pl.ds(r, DOWN_PIECE_ROWS) for r in range(0, D_FF, DOWN_PIECE_ROWS)]

    def store_up(i, v):
        dst = w1_ref if i < len(up_rows) else w3_ref
        dst[up_rows[i % len(up_rows)], :] = v

    def store_dn(i, v):
        w2_ref[dn_rows[i], :] = v

    _stream_cast([w.at[layer, r, :] for w in (w1_hbm, w3_hbm) for r in up_rows],
                 stage_up, sem_up, store_up)
    _stream_cast([w2_hbm.at[layer, r, :] for r in dn_rows], stage_dn, sem_dn, store_dn)
    shift = modall_ref[:, mod_row, :].astype(BF)
    s1_ref[...] = _dot(shift, w1_ref[...])
    s3_ref[...] = _dot(shift, w3_ref[...])


def _first_step():
    return jnp.logical_and(pl.program_id(0) == 0, pl.program_id(1) == 0)


def _ffn_tiles(xs, m, mod_row, g_ref, s1, s3, w1_ref, w3_ref, w2_ref):
    scale, gate = m[mod_row + 1:mod_row + 2], m[mod_row + 2:mod_row + 3]
    fc = FF_CHUNK_TILES * MXU_TILE
    gs = g_ref[...] * (1.0 + scale)
    us = [(x * gs).astype(BF) for x in xs]
    rs = [_row_scale(x) for x in xs]
    accs = [None] * len(xs)
    for lo in range(0, D_FF, fc):
        cols = slice(lo, min(lo + fc, D_FF))
        for s, u in enumerate(us):
            h1 = rs[s] * _dot(u, w1_ref[:, cols]) + s1[:, cols]
            h3 = rs[s] * _dot(u, w3_ref[:, cols]) + s3[:, cols]
            p = _dot((_silu(h1) * h3).astype(BF), w2_ref[cols, :])
            accs[s] = p if accs[s] is None else accs[s] + p
    return [x + (0.5 * (1.0 + gate)) * acc for x, acc in zip(xs, accs)]


def _ffn_body(x_ref, mod_ref, modall_ref, g_ref, w1_hbm, w3_hbm, w2_hbm, o_ref,
              w1_ref, w3_ref, w2_ref, s1_ref, s3_ref, *dma_scratch, mod_row, layer):
    @pl.when(_first_step())
    def _():
        _load_ffn_weights(layer, mod_row, modall_ref, w1_hbm, w3_hbm, w2_hbm,
                          w1_ref, w3_ref, w2_ref, s1_ref, s3_ref, *dma_scratch)

    b = pl.program_id(0)
    sub = FFN_TILE // FFN_SUBTILES
    rows = [slice(s * sub, (s + 1) * sub) for s in range(FFN_SUBTILES)]
    outs = _ffn_tiles([x_ref[0, r, :] for r in rows], mod_ref[0], mod_row, g_ref,
                      s1_ref[pl.ds(b, 1), :], s3_ref[pl.ds(b, 1), :], w1_ref, w3_ref, w2_ref)
    for r, o in zip(rows, outs):
        o_ref[0, r, :] = o


def _ffn_call(x, mod, g, w1, w3, w2, mod_row, layer):
    row = pl.BlockSpec((1, FFN_TILE, D_MODEL), lambda b, i: (b, i, 0))
    hbm = pl.BlockSpec(memory_space=pl.ANY)
    return pl.pallas_call(
        functools.partial(_ffn_body, mod_row=mod_row, layer=layer),
        grid=(BATCH, SEQ // FFN_TILE),
        in_specs=[row,
                  pl.BlockSpec((1, N_MOD, D_MODEL), lambda b, i: (b, 0, 0)),
                  _const_spec((BATCH, N_MOD, D_MODEL)),
                  _const_spec((1, D_MODEL)),
                  hbm, hbm, hbm],
        out_specs=row,
        out_shape=jax.ShapeDtypeStruct((BATCH, SEQ, D_MODEL), F32),
        scratch_shapes=FFN_WEIGHT_SCRATCH,
        compiler_params=_params("arbitrary", "arbitrary"),
        name="ffn%d" % mod_row,
    )(x, mod, mod, g, w1, w3, w2)


QK_W = 2 * ATT_OUT_WIDTH
GRP_W = 3 * ATT_OUT_WIDTH
GD_PAD = 128


def _proj_body(h_ref, mod_ref, g_ref, satt_ref, sgq_ref, sgv_ref, sgr_ref, sgd_ref, sgate_ref,
               watt_ref, wgq_ref, wgkt_ref, wgv_ref, wgr_ref, wgd_ref,
               wgate_ref, hsum_ref, hexp_ref, qkg_ref,
               a0_ref, a1_ref, a2_ref, gq_ref, gkt_ref, gv_ref, gr_ref, gd_ref, gate_ref, perm_ref):
    m = mod_ref[0]
    x = h_ref[0]
    xg = x * (g_ref[...] * (1.0 + m[4:5]))
    u = xg.astype(BF)
    rs = _row_scale(x)

    def proj(w, s):
        return rs * _dot(u, w) + s

    for gi, a_ref in enumerate((a0_ref, a1_ref, a2_ref)):
        d = ATT_GROUPS[gi][1]
        gcols = slice(gi * GRP_W, (gi + 1) * GRP_W)
        y = proj(watt_ref[:, gcols], satt_ref[0, :, gcols])
        qk = y[:, :QK_W]
        ss = _dot((qk * qk).astype(BF), hsum_ref[...])
        rx = _dot_split(lax.rsqrt(ss * (1.0 / ATT_HEAD_DIM) + EPS), hexp_ref)
        qkn = qk * rx * qkg_ref[...]
        if d == 1:
            a_ref[0, 0, :, :QK_W] = qkn.astype(BF)
            a_ref[0, 0, :, QK_W:] = y[:, QK_W:].astype(BF)
        else:
            for j in range(GRP_W // 128):
                cols = slice(j * 128, (j + 1) * 128)
                perm_ref[j] = qkn[:, cols] if j < QK_W // 128 else y[:, cols]
            for r in range(d):
                for j in range(GRP_W // 128):
                    a_ref[0, r, :, j * 128:(j + 1) * 128] = (
                        perm_ref[j, pl.ds(r, ROW_TILE // d, stride=d), :].astype(BF))
    gq_ref[0] = proj(wgq_ref[...], sgq_ref[0]).astype(BF)
    gv_ref[0] = proj(wgv_ref[...], sgv_ref[0]).astype(BF)
    gr_ref[0] = proj(wgr_ref[...], sgr_ref[0]).astype(BF)
    gd_ref[0] = proj(wgd_ref[...], sgd_ref[0]).astype(BF)
    gate_ref[0] = proj(wgate_ref[...], sgate_ref[0]).astype(BF)
    u_full = (xg * rs + m[3:4]).astype(BF)
    gkt_ref[0] = _dot_nt(wgkt_ref[...], u_full).astype(BF)


def _proj_call(h, mod, g, watt, wgq, wgkt, wgv, wgr, wgd, wgate, hsum, hexp, qkg):
    def row(w):
        return pl.BlockSpec((1, ROW_TILE, w), lambda b, i: (b, i, 0))

    def out(w):
        return jax.ShapeDtypeStruct((BATCH, SEQ, w), BF)

    kt_spec = pl.BlockSpec((1, GLA_KEY_DIM, ROW_TILE), lambda b, i: (b, 0, i))
    kt_shape = jax.ShapeDtypeStruct((BATCH, GLA_KEY_DIM, SEQ), BF)
    widths = (GLA_VAL_DIM, GLA_VAL_DIM, GD_PAD, 2 * D_MODEL)
    row_weights = (watt, wgq, wgv, wgr, wgd, wgate)
    shifts = _shiftproj_call(mod, 3, row_weights, "proj_shift")
    consts = (watt, wgq, wgkt, wgv, wgr, wgd, wgate, hsum, hexp, qkg)
    dils = [d for _, d in ATT_GROUPS]
    return pl.pallas_call(
        _proj_body,
        grid=(BATCH, SEQ // ROW_TILE),
        in_specs=[row(D_MODEL), pl.BlockSpec((1, N_MOD, D_MODEL), lambda b, i: (b, 0, 0)),
                  _const_spec(g.shape)]
                 + [pl.BlockSpec((1, 1, w.shape[1]), lambda b, i: (b, 0, 0)) for w in row_weights]
                 + [_const_spec(a.shape) for a in consts],
        out_specs=[_dilated_spec(d, GRP_W) for d in dils] + [row(GLA_KEY_DIM), kt_spec]
                  + [row(w) for w in widths],
        out_shape=[jax.ShapeDtypeStruct((BATCH, d, SEQ // d, GRP_W), BF) for d in dils]
                  + [out(GLA_KEY_DIM), kt_shape] + [out(w) for w in widths],
        scratch_shapes=[pltpu.VMEM((GRP_W // 128, ROW_TILE, 128), F32)],
        compiler_params=_params("parallel", "parallel"),
        name="proj",
    )(h, mod, g, *shifts, *consts)


ATT_GROUP_BLOCKS = 4


def _attn_blocks(a_ref, bias_ref, o_ref, l_ref, blocks):
    W, NH = ATT_OUT_WIDTH, ATT_HEADS_PER_GROUP
    lane = lax.broadcasted_iota(jnp.int32, (ATT_BLK, 128), 1)
    low_half = lane < ATT_HEAD_DIM
    zero_col = jnp.zeros((ATT_BLK, 128), BF)
    qs, ks, vs, biases, q0s = [], [], [], [], []
    for r, n, first in blocks:
        if first:
            q0, k0, nk = 0, 0, ATT_BLK
            biases.append(bias_ref[:, ATT_BLK:])
        else:
            q0, k0, nk = n * ATT_BLK, (n - 1) * ATT_BLK, 2 * ATT_BLK
            if not isinstance(n, int):
                q0, k0 = pl.multiple_of(q0, ATT_BLK), pl.multiple_of(k0, ATT_BLK)
            biases.append(bias_ref[...])
        q0s.append(q0)
        stacked = []
        for j in range(NH):
            col = a_ref[0, r, pl.ds(q0, ATT_BLK), (j // 2) * 128:(j // 2 + 1) * 128]
            col = jnp.where(low_half == (j % 2 == 0), col, zero_col)
            stacked.append(jnp.concatenate([col, zero_col] if j < 2 else [zero_col, col], axis=1))
        qs.append(jnp.concatenate(stacked, axis=0))
        ks.append(a_ref[0, r, pl.ds(k0, nk), W:2 * W])
        vs.append(a_ref[0, r, pl.ds(k0, nk), 2 * W:])
    ss = [_dot_nt(q4, k) + b for q4, k, b in zip(qs, ks, biases)]
    mxs = [jnp.max(s, axis=-1, keepdims=True) for s in ss]
    es = [jnp.exp2(s - mx) for s, mx in zip(ss, mxs)]
    dens = [jnp.sum(e, axis=-1, keepdims=True) for e in es]
    pvs = [_dot(e.astype(BF), v) for e, v in zip(es, vs)]
    for (r, _, _), q0, pv, mx, den in zip(blocks, q0s, pvs, mxs, dens):
        rden = 1.0 / den
        lse = mx * LN2 + jnp.log(den)
        l = jnp.zeros((ATT_BLK, 128), F32)
        heads = []
        for j in range(NH):
            rows = slice(j * ATT_BLK, (j + 1) * ATT_BLK)
            heads.append(pv[rows, (j // 2) * 128:(j // 2 + 1) * 128] * rden[rows])
            l = jnp.where(lane == j, lse[rows], l)
        o = jnp.concatenate([jnp.where(low_half, heads[0], heads[1]),
                             jnp.where(low_half, heads[2], heads[3])], axis=1)
        o_ref[0, r, pl.ds(q0, ATT_BLK), :] = o.astype(o_ref.dtype)
        l_ref[0, r, pl.ds(q0, ATT_BLK), :] = l


def _attn_body(a_ref, bias_ref, o_ref, l_ref, *, subseqs, n_blocks):
    G = ATT_GROUP_BLOCKS
    if n_blocks == 1:
        for r0 in range(0, subseqs, G):
            _attn_blocks(a_ref, bias_ref, o_ref, l_ref,
                         [(r, 0, True) for r in range(r0, min(r0 + G, subseqs))])
        return
    for r in range(subseqs):
        for n0 in range(0, n_blocks, G):
            _attn_blocks(a_ref, bias_ref, o_ref, l_ref,
                         [(r, n, n == 0) for n in range(n0, n0 + G)])


def _attn_bias(group):
    _, dilation = ATT_GROUPS[group]
    heads = np.arange(group * ATT_HEADS_PER_GROUP, (group + 1) * ATT_HEADS_PER_GROUP)
    slopes = (2.0 ** (-ALIBI_MAX * (heads + 1).astype(np.float32) / ATT_HEADS)).astype(np.float32)
    qi = np.arange(ATT_BLK)[:, None]
    kj = np.arange(2 * ATT_BLK)[None, :]
    steps = qi + ATT_BLK - kj
    valid = (steps >= 0) & (steps <= ATT_BLK)
    bias = -(slopes * np.float32(dilation))[:, None, None] * steps.astype(np.float32)[None]
    bias = np.where(valid[None], bias * np.float32(LOG2E), np.float32(NEG_BIG)).astype(np.float32)
    return bias.reshape(ATT_HEADS_PER_GROUP * ATT_BLK, 2 * ATT_BLK)


def _attn_call(a, group):
    _, d = ATT_GROUPS[group]
    L = SEQ // d
    n_blocks = L // ATT_BLK
    subseqs = d if n_blocks == 1 else min(d, 4)

    def spec(w):
        return pl.BlockSpec((1, subseqs, L, w), lambda b, r: (b, r, 0, 0))

    return pl.pallas_call(
        functools.partial(_attn_body, subseqs=subseqs, n_blocks=n_blocks),
        grid=(BATCH, d // subseqs),
        in_specs=[spec(GRP_W), _const_spec((ATT_HEADS_PER_GROUP * ATT_BLK, 2 * ATT_BLK))],
        out_specs=[spec(ATT_OUT_WIDTH), spec(128)],
        out_shape=[jax.ShapeDtypeStruct((BATCH, d, L, ATT_OUT_WIDTH), BF),
                   jax.ShapeDtypeStruct((BATCH, d, L, 128), F32)],
        compiler_params=_params("parallel", "parallel"),
        name="attn%d" % d,
    )(a, jnp.asarray(_attn_bias(group)))


GLA_PAIR = 2 * GLA_CHUNK


def _gla_cum_decay(gd_ref, up_ref, gb_ref, tril2_ref):
    x = _dot(gd_ref[0], up_ref[...]) + gb_ref[...]
    log_a = (jnp.minimum(x, 0.0) - jnp.log(1.0 + jnp.exp(-jnp.abs(x)))) * (1.0 / GLA_TAU)
    la_hi, la_lo = _split_bf16(log_a)
    pieces = []
    for p in range(GLA_BLOCK // GLA_PAIR):
        rows = slice(p * GLA_PAIR, (p + 1) * GLA_PAIR)
        pieces.append(_dot(tril2_ref[...], jnp.concatenate([la_hi[rows], la_lo[rows]], axis=0)))
    return pieces


def _gla_pair(p, b, states, gq_ref, gkt_ref, gv_ref, o_ref):
    C, P = GLA_CHUNK, GLA_PAIR
    rows = slice(p * P, (p + 1) * P)
    q_dec =(gq_ref[0, rows, :].astype(F32) * (GLA_DK ** -0.5) * jnp.exp(b)).astype(BF)
    b_t = b.T
    k_dec_t_bf = (gkt_ref[0, :, rows].astype(F32) * jnp.exp(-b_t)).astype(BF)
    decay_t = jnp.exp(b_t)
    lane = lax.broadcasted_iota(jnp.int32, (GLA_KEY_DIM, P), 1)
    zero = jnp.zeros_like(k_dec_t_bf)
    k_dec_t_chunk = [jnp.where(lane < C, k_dec_t_bf, zero), jnp.where(lane >= C, k_dec_t_bf, zero)]
    ri = lax.broadcasted_iota(jnp.int32, (P, P), 0)
    ci = lax.broadcasted_iota(jnp.int32, (P, P), 1)
    causal = (ri >= ci) & ((ri >= C) == (ci >= C))
    new_states = []
    for h in range(GLA_HEADS):
        kc = slice(h * GLA_DK, (h + 1) * GLA_DK)
        vc = slice(h * GLA_DV, (h + 1) * GLA_DV)
        v = gv_ref[0, rows, vc]
        attn = jnp.where(causal, _dot(q_dec[:, kc], k_dec_t_bf[kc, :]), 0.0).astype(BF)
        stacked = _dot(jnp.concatenate([k_dec_t_chunk[0][kc, :], k_dec_t_chunk[1][kc, :], attn],
                                       axis=0), v)
        st = states[h]
        for cc in range(2):
            crow = slice(cc * C, (cc + 1) * C)
            o = stacked[2 * GLA_DK + cc * C:2 * GLA_DK + (cc + 1) * C] + _dot(q_dec[crow, kc],
                                                                              st.astype(BF))
            o_ref[0, p * P + cc * C:p * P + (cc + 1) * C, vc] = o.astype(o_ref.dtype)
            last = (cc + 1) * C - 1
            st = decay_t[kc, last:last + 1] * (st + stacked[cc * GLA_DK:(cc + 1) * GLA_DK])
        new_states.append(st)
    return new_states


def _gla_body(gq_ref, gkt_ref, gv_ref, gd_ref, up_ref, gb_ref, tril2_ref, o_ref, state_ref):
    @pl.when(pl.program_id(1) == 0)
    def _():
        state_ref[...] = jnp.zeros_like(state_ref)

    bs = _gla_cum_decay(gd_ref, up_ref, gb_ref, tril2_ref)
    states = [state_ref[h] for h in range(GLA_HEADS)]
    for p in range(GLA_BLOCK // GLA_PAIR):
        states = _gla_pair(p, bs[p], states, gq_ref, gkt_ref, gv_ref, o_ref)
    for h in range(GLA_HEADS):
        state_ref[h] = states[h]


def _pair_tril2():
    i = np.arange(GLA_PAIR)
    t = ((i[:, None] // GLA_CHUNK) == (i[None, :] // GLA_CHUNK)) & (i[:, None] >= i[None, :])
    return np.concatenate([t, t], axis=1).astype(np.float32)


def _gla_call(gq, gkt, gv, gd, up, gbias):
    def row(w):
        return pl.BlockSpec((1, GLA_BLOCK, w), lambda b, i: (b, i, 0))

    consts = (up, gbias, jnp.asarray(_pair_tril2(), BF))
    return pl.pallas_call(
        _gla_body,
        grid=(BATCH, SEQ // GLA_BLOCK),
        in_specs=[row(GLA_KEY_DIM),
                  pl.BlockSpec((1, GLA_KEY_DIM, GLA_BLOCK), lambda b, i: (b, 0, i)),
                  row(GLA_VAL_DIM), row(GD_PAD)]
                 + [_const_spec(a.shape) for a in consts],
        out_specs=row(GLA_VAL_DIM),
        out_shape=jax.ShapeDtypeStruct((BATCH, SEQ, GLA_VAL_DIM), BF),
        scratch_shapes=[pltpu.VMEM((GLA_HEADS, GLA_DK, GLA_DV), F32)],
        compiler_params=_params("parallel", "arbitrary"),
        name="gla",
    )(gq, gkt, gv, gd, *consts)


def _undilate(src_ref, scr_ref, d):
    if d == 1:
        return src_ref[0, 0].astype(F32)
    planes = scr_ref.shape[0]
    for r in range(d):
        x = src_ref[0, r].astype(F32)
        for j in range(planes):
            scr_ref[j, pl.ds(r, ROW_TILE // d, stride=d), :] = x[:, j * 128:(j + 1) * 128]
    return jnp.concatenate([scr_ref[j] for j in range(planes)], axis=-1)


def _merge_body(h_ref, mod_ref, o0_ref, o1_ref, o2_ref, l0_ref, l1_ref, l2_ref,
                ogla_ref, gr_ref, gate_ref, modall_ref, hexp_ref, gain_ref, wba_ref, wbg_ref,
                wout_ref, g_ref, w1_hbm, w3_hbm, w2_hbm, out_ref,
                os1_ref, os2_ref, ls1_ref, ls2_ref,
                w1_ref, w3_ref, w2_ref, s1_ref, s3_ref, *dma_scratch, layer):
    @pl.when(_first_step())
    def _():
        _load_ffn_weights(layer, 6, modall_ref, w1_hbm, w3_hbm, w2_hbm,
                          w1_ref, w3_ref, w2_ref, s1_ref, s3_ref, *dma_scratch)

    dils = [d for _, d in ATT_GROUPS]
    lses = [_undilate(r, s, d) for r, s, d in
            zip((l0_ref, l1_ref, l2_ref), (None, ls1_ref, ls2_ref), dils)]
    outs = [_undilate(r, s, d) for r, s, d in
            zip((o0_ref, o1_ref, o2_ref), (None, os1_ref, os2_ref), dils)]
    mx = jnp.maximum(jnp.maximum(lses[0], lses[1]), lses[2])
    es = [jnp.exp(l - mx) for l in lses]
    den = es[0] + es[1] + es[2]
    o_att = None
    for e, o in zip(es, outs):
        t = _dot_split(e / den, hexp_ref) * o
        o_att = t if o_att is None else o_att + t
    att = _dot(o_att.astype(BF), wba_ref[...])
    gated = []
    for h in range(GLA_HEADS):
        vc = slice(h * GLA_DV, (h + 1) * GLA_DV)
        o = ogla_ref[0, :, vc].astype(F32)
        ms = jnp.mean(o * o, axis=-1, keepdims=True)
        y = o * lax.rsqrt(ms + EPS) * gain_ref[:, vc]
        gated.append(y.astype(BF) * _silu(gr_ref[0, :, vc]))
    gla = _dot(jnp.concatenate(gated, axis=1), wbg_ref[...])
    merged = (_sigmoid(gate_ref[0, :, :D_MODEL]) * att.astype(BF)
              + _sigmoid(gate_ref[0, :, D_MODEL:]) * gla.astype(BF))
    m = mod_ref[0]
    h = h_ref[0] + (1.0 + m[5:6]) * _dot(merged, wout_ref[...])
    b = pl.program_id(0)
    out_ref[0] = _ffn_tiles([h], m, 6, g_ref, s1_ref[pl.ds(b, 1), :], s3_ref[pl.ds(b, 1), :],
                            w1_ref, w3_ref, w2_ref)[0]


def _merge_ffn_call(h, mod, os_, ls_, ogla, gr, gates, hexp, gain, wba, wbg, wout, g, w1, w3, w2,
                    layer):
    def row(w):
        return pl.BlockSpec((1, ROW_TILE, w), lambda b, i: (b, i, 0))

    consts = (mod, hexp, gain, wba, wbg, wout, g)
    hbm = pl.BlockSpec(memory_space=pl.ANY)
    dils = [d for _, d in ATT_GROUPS]
    return pl.pallas_call(
        functools.partial(_merge_body, layer=layer),
        grid=(BATCH, SEQ // ROW_TILE),
        in_specs=[row(D_MODEL), pl.BlockSpec((1, N_MOD, D_MODEL), lambda b, i: (b, 0, 0))]
                 + [_dilated_spec(d, ATT_OUT_WIDTH) for d in dils]
                 + [_dilated_spec(d, 128) for d in dils]
                 + [row(GLA_VAL_DIM), row(GLA_VAL_DIM), row(2 * D_MODEL)]
                 + [_const_spec(a.shape) for a in consts] + [hbm, hbm, hbm],
        out_specs=row(D_MODEL),
        out_shape=jax.ShapeDtypeStruct((BATCH, SEQ, D_MODEL), F32),
        scratch_shapes=[pltpu.VMEM((ATT_OUT_WIDTH // 128, ROW_TILE, 128), F32)] * 2
                       + [pltpu.VMEM((1, ROW_TILE, 128), F32)] * 2 + FFN_WEIGHT_SCRATCH,
        compiler_params=_params("arbitrary", "arbitrary"),
        name="merge_ffn",
    )(h, mod, *os_, *ls_, ogla, gr, gates, *consts, w1, w3, w2)


def _head_matrix(n_lanes_in, width):
    j = np.arange(n_lanes_in)[:, None]
    c = np.arange(width)[None, :]
    return (c // ATT_HEAD_DIM == j).astype(np.float32)


def _layer(l, h, mod, g_ffn1, f1w1, f1w3, f1w2, g_mix, w_in, q_norm_g, k_norm_g, gate_up, gate_bias,
           out_norm_g, w_branch_att, w_branch_gla, w_out, g_ffn2, f2w1, f2w3, f2w2):
    bf = lambda w: w.astype(BF)
    row = lambda v: v.reshape(1, -1)

    h = _ffn_call(h, mod, row(g_ffn1), f1w1, f1w3, f1w2, 0, l)

    splits = np.cumsum((ATT_WIDTH, ATT_WIDTH, ATT_WIDTH, GLA_KEY_DIM, GLA_KEY_DIM, GLA_VAL_DIM,
                        GLA_VAL_DIM, GLA_GATE_RANK))
    aq, ak, av, gq, gk, gv, gr, gdn, gates = jnp.split(w_in, [int(s) for s in splits], axis=1)
    W = ATT_OUT_WIDTH
    watt = bf(jnp.concatenate([t[:, gi * W:(gi + 1) * W] for gi in range(len(ATT_GROUPS))
                               for t in (aq, ak, av)], axis=1))
    gq, gk, gv, gr, gates = bf(gq), bf(gk), bf(gv), bf(gr), bf(gates)
    wgd = bf(jnp.pad(gdn, ((0, 0), (0, GD_PAD - GLA_GATE_RANK))))
    head_qk = _head_matrix(128, QK_W)
    hsum_qk = jnp.asarray(head_qk.T, BF)
    hexp_qk = jnp.asarray(np.concatenate([head_qk, head_qk]), BF)
    qkg = jnp.concatenate([jnp.tile(q_norm_g * (ATT_HEAD_DIM ** -0.5 * LOG2E), ATT_HEADS_PER_GROUP),
                           jnp.tile(k_norm_g, ATT_HEADS_PER_GROUP)])
    a0, a1, a2, pgq, pgkt, pgv, pgr, pgd, pgate = _proj_call(
        h, mod, row(g_mix), watt, gq, gk.T, gv, gr, wgd, gates, hsum_qk, hexp_qk, row(qkg))

    outs = [_attn_call(a, gi) for gi, a in enumerate((a0, a1, a2))]
    up = jnp.pad(bf(gate_up), ((0, GD_PAD - GLA_GATE_RANK), (0, 0)))
    ogla = _gla_call(pgq, pgkt, pgv, pgd, up, row(gate_bias))

    head_o = _head_matrix(128, ATT_OUT_WIDTH)
    hexp_o = jnp.asarray(np.concatenate([head_o, head_o]), BF)
    return _merge_ffn_call(h, mod, [o for o, _ in outs], [l for _, l in outs], ogla, pgr, pgate,
                           hexp_o, row(jnp.tile(out_norm_g, GLA_HEADS)), bf(w_branch_att),
                           bf(w_branch_gla), bf(w_out), row(g_ffn2), f2w1, f2w3, f2w2, l)


def kernel(x, c, w_mod, b_mod, g_ffn1, ffn1_w1, ffn1_w3, ffn1_w2, g_mix, w_in, q_norm_g, k_norm_g,
           gla_gate_up, gla_gate_bias, gla_out_norm_g, w_branch_att, w_branch_gla, w_out,
           g_ffn2, ffn2_w1, ffn2_w3, ffn2_w2):
    h = x
    for l in range(w_mod.shape[0]):
        mod = _mod_call(c, w_mod, b_mod[l], l).reshape(BATCH, N_MOD, D_MODEL)
        h = _layer(l, h, mod, g_ffn1[l], ffn1_w1, ffn1_w3, ffn1_w2, g_mix[l], w_in[l],
                   q_norm_g[l], k_norm_g[l], gla_gate_up[l], gla_gate_bias[l], gla_out_norm_g[l],
                   w_branch_att[l], w_branch_gla[l], w_out[l], g_ffn2[l], ffn2_w1, ffn2_w3,
                   ffn2_w2)
    return h
```

```python
import functools

import numpy as np
import jax
import jax.numpy as jnp
from jax import lax
from jax.experimental import pallas as pl
from jax.experimental.pallas import tpu as pltpu

D_MODEL = 1024
BATCH = 16
SEQ = 2048
N_MOD = 9
D_FF = 2816
EPS = 1e-6
ATT_GROUPS = ((128, 1), (512, 4), (2048, 16))
ATT_HEADS_PER_GROUP = 4
ATT_HEADS = 12
ATT_HEAD_DIM = 64
ATT_WIDTH = 768
ATT_OUT_WIDTH = 256
ALIBI_MAX = 8.0
GLA_HEADS = 4
GLA_KEY_DIM = 512
GLA_VAL_DIM = 1024
GLA_DK = 128
GLA_DV = 256
GLA_GATE_RANK = 16
GLA_TAU = 16.0
GLA_CHUNK = 64
ATT_BLK = 128
NEG_BIG = -1e30
LOG2E = float(np.log2(np.e))
LN2 = float(np.log(2.0))

BF = jnp.bfloat16
F32 = jnp.float32

VMEM_LIMIT_BYTES = 56 * 1024 * 1024

ROW_TILE = 512
FFN_TILE = 1024
FFN_SUBTILES = 2
MXU_TILE = 256
FF_CHUNK_TILES = 4
GLA_BLOCK = 512


def _dot(a, b):
    return jnp.dot(a, b, preferred_element_type=F32)


def _dot_nt(a, b):
    return lax.dot_general(a, b, (((1,), (1,)), ((), ())), preferred_element_type=F32)


def _split_bf16(x):
    hi = x.astype(BF)
    lo = (x - hi.astype(F32)).astype(BF)
    return hi, lo


def _dot_split(x, stacked_ref):
    hi, lo = _split_bf16(x)
    return _dot(jnp.concatenate([hi, lo], axis=1), stacked_ref[...])


def _sigmoid(x):
    return 0.5 * jnp.tanh(0.5 * x) + 0.5


def _silu(x):
    return x * _sigmoid(x)


def _const_spec(shape):
    nd = len(shape)
    return pl.BlockSpec(shape, lambda *_: (0,) * nd, pipeline_mode=pl.Buffered(1))


def _dilated_spec(d, width):
    return pl.BlockSpec((1, d, ROW_TILE // d, width), lambda b, i: (b, 0, i, 0))


def _params(*sem):
    return pltpu.CompilerParams(dimension_semantics=sem, vmem_limit_bytes=VMEM_LIMIT_BYTES)


def _mod_body(c_ref, w_ref, b_ref, o_ref):
    c = c_ref[...]
    o_ref[...] = _dot(_silu(c).astype(BF), w_ref[...].astype(BF)) + b_ref[...]


def _mod_call(c, w_mod, b_mod, layer):
    n = w_mod.shape[2]
    bn = D_MODEL
    return pl.pallas_call(
        _mod_body,
        grid=(n // bn,),
        in_specs=[pl.BlockSpec((BATCH, D_MODEL), lambda j: (0, 0)),
                  pl.BlockSpec((None, D_MODEL, bn), lambda j: (layer, 0, j)),
                  pl.BlockSpec((1, bn), lambda j: (0, j))],
        out_specs=pl.BlockSpec((BATCH, bn), lambda j: (0, j)),
        out_shape=jax.ShapeDtypeStruct((BATCH, n), F32),
        compiler_params=_params("arbitrary"),
        name="mod",
    )(c, w_mod, b_mod.reshape(1, n))


def _row_scale(x):
    return lax.rsqrt(jnp.mean(x * x, axis=-1, keepdims=True) + EPS)


UP_PIECE_ROWS = 128
DOWN_PIECE_ROWS = 256

FFN_WEIGHT_SCRATCH = [
    pltpu.VMEM((D_MODEL, D_FF), BF), pltpu.VMEM((D_MODEL, D_FF), BF), pltpu.VMEM((D_FF, D_MODEL), BF),
    pltpu.VMEM((BATCH, D_FF), F32), pltpu.VMEM((BATCH, D_FF), F32),
    pltpu.VMEM((2, UP_PIECE_ROWS, D_FF), F32), pltpu.VMEM((2, DOWN_PIECE_ROWS, D_MODEL), F32),
    pltpu.SemaphoreType.DMA((2,)), pltpu.SemaphoreType.DMA((2,))]


def _stream_cast(pieces, stage_ref, sem_ref, store, cast=True):
    def copy(i):
        return pltpu.make_async_copy(pieces[i], stage_ref.at[i % 2], sem_ref.at[i % 2])

    copy(0).start()
    for i in range(len(pieces)):
        if i + 1 < len(pieces):
            copy(i + 1).start()
        copy(i).wait()
        v = stage_ref[i % 2]
        store(i, v.astype(BF) if cast else v)


def _load_ffn_weights(layer, mod_row, modall_ref, w1_hbm, w3_hbm, w2_hbm,
                      w1_ref, w3_ref, w2_ref, s1_ref, s3_ref, stage_up, stage_dn, sem_up, sem_dn):
    up_rows = [pl.ds(r, UP_PIECE_ROWS) for r in range(0, D_MODEL, UP_PIECE_ROWS)]
    dn_rows = [pl.ds(r, DOWN_PIECE_ROWS) for r in range(0, D_FF, DOWN_PIECE_ROWS)]

    def store_up(i, v):
        dst = w1_ref if i < len(up_rows) else w3_ref
        dst[up_rows[i % len(up_rows)], :] = v

    def store_dn(i, v):
        w2_ref[dn_rows[i], :] = v

    _stream_cast([w.at[layer, r, :] for w in (w1_hbm, w3_hbm) for r in up_rows],
                 stage_up, sem_up, store_up)
    _stream_cast([w2_hbm.at[layer, r, :] for r in dn_rows], stage_dn, sem_dn, store_dn)
    shift = modall_ref[:, mod_row, :].astype(BF)
    s1_ref[...] = _dot(shift, w1_ref[...])
    s3_ref[...] = _dot(shift, w3_ref[...])


def _first_step():
    return jnp.logical_and(pl.program_id(0) == 0, pl.program_id(1) == 0)


def _ffn_tiles(xs, m, mod_row, g_ref, s1, s3, w1_ref, w3_ref, w2_ref):
    scale, gate = m[mod_row + 1:mod_row + 2], m[mod_row + 2:mod_row + 3]
    fc = FF_CHUNK_TILES * MXU_TILE
    gs = g_ref[...] * (1.0 + scale)
    us = [(x * gs).astype(BF) for x in xs]
    rs = [_row_scale(x) for x in xs]
    accs = [None] * len(xs)
    for lo in range(0, D_FF, fc):
        cols = slice(lo, min(lo + fc, D_FF))
        for s, u in enumerate(us):
            h1 = rs[s] * _dot(u, w1_ref[:, cols]) + s1[:, cols]
            h3 = rs[s] * _dot(u, w3_ref[:, cols]) + s3[:, cols]
            p = _dot((_silu(h1) * h3).astype(BF), w2_ref[cols, :])
            accs[s] = p if accs[s] is None else accs[s] + p
    return [x + (0.5 * (1.0 + gate)) * acc for x, acc in zip(xs, accs)]


def _ffn_body(x_ref, mod_ref, modall_ref, g_ref, w1_hbm, w3_hbm, w2_hbm, o_ref,
              w1_ref, w3_ref, w2_ref, s1_ref, s3_ref, *dma_scratch, mod_row, layer):
    @pl.when(_first_step())
    def _():
        _load_ffn_weights(layer, mod_row, modall_ref, w1_hbm, w3_hbm, w2_hbm,
                          w1_ref, w3_ref, w2_ref, s1_ref, s3_ref, *dma_scratch)

    b = pl.program_id(0)
    sub = FFN_TILE // FFN_SUBTILES
    rows = [slice(s * sub, (s + 1) * sub) for s in range(FFN_SUBTILES)]
    outs = _ffn_tiles([x_ref[0, r, :] for r in rows], mod_ref[0], mod_row, g_ref,
                      s1_ref[pl.ds(b, 1), :], s3_ref[pl.ds(b, 1), :], w1_ref, w3_ref, w2_ref)
    for r, o in zip(rows, outs):
        o_ref[0, r, :] = o


def _ffn_call(x, mod, g, w1, w3, w2, mod_row, layer):
    row = pl.BlockSpec((1, FFN_TILE, D_MODEL), lambda b, i: (b, i, 0))
    hbm = pl.BlockSpec(memory_space=pl.ANY)
    return pl.pallas_call(
        functools.partial(_ffn_body, mod_row=mod_row, layer=layer),
        grid=(BATCH, SEQ // FFN_TILE),
        in_specs=[row,
                  pl.BlockSpec((1, N_MOD, D_MODEL), lambda b, i: (b, 0, 0)),
                  _const_spec((BATCH, N_MOD, D_MODEL)),
                  _const_spec((1, D_MODEL)),
                  hbm, hbm, hbm],
        out_specs=row,
        out_shape=jax.ShapeDtypeStruct((BATCH, SEQ, D_MODEL), F32),
        scratch_shapes=FFN_WEIGHT_SCRATCH,
        compiler_params=_params("arbitrary", "arbitrary"),
        name="ffn%d" % mod_row,
    )(x, mod, mod, g, w1, w3, w2)


QK_W = 2 * ATT_OUT_WIDTH
GRP_W = 3 * ATT_OUT_WIDTH
GD_PAD = 128


IN_GQ = 3 * ATT_WIDTH
IN_GK = IN_GQ + GLA_KEY_DIM
IN_GV = IN_GK + GLA_KEY_DIM
IN_GR = IN_GV + GLA_VAL_DIM
IN_GD = IN_GR + GLA_VAL_DIM
IN_GATES = IN_GD + GLA_GATE_RANK
IN_PIECE_ROWS = 64
GK_PIECE_ROWS = 128

PROJ_WEIGHT_SCRATCH = [
    pltpu.VMEM((D_MODEL, 3 * GRP_W), BF), pltpu.VMEM((D_MODEL, GLA_KEY_DIM), BF),
    pltpu.VMEM((GLA_KEY_DIM, D_MODEL), BF), pltpu.VMEM((D_MODEL, GLA_VAL_DIM), BF),
    pltpu.VMEM((D_MODEL, GLA_VAL_DIM), BF)]
PROJ_SHIFT_SCRATCH = [pltpu.VMEM((BATCH, w), F32) for w in
                      (3 * GRP_W, GLA_KEY_DIM, GLA_VAL_DIM, GLA_VAL_DIM, GD_PAD, 2 * D_MODEL)]
PROJ_DMA_SCRATCH = [
    pltpu.VMEM((2, IN_PIECE_ROWS, IN_GD), F32), pltpu.VMEM((2, GK_PIECE_ROWS, GLA_KEY_DIM), F32),
    pltpu.SemaphoreType.DMA((2,)), pltpu.SemaphoreType.DMA((2,))]


def _load_proj_weights(layer, modall_ref, win_hbm, wgd_ref, wgate_ref, weights, shifts,
                       stage_ref, stage_k_ref, sem_ref, sem_k_ref):
    watt_ref, wgq_ref, wgkt_ref, wgv_ref, wgr_ref = weights
    W = ATT_OUT_WIDTH
    rows = [pl.ds(r, IN_PIECE_ROWS) for r in range(0, D_MODEL, IN_PIECE_ROWS)]

    def store(i, v):
        r = rows[i]
        for gi in range(len(ATT_GROUPS)):
            for t in range(3):
                src = t * ATT_WIDTH + gi * W
                watt_ref[r, gi * GRP_W + t * W:gi * GRP_W + (t + 1) * W] = v[:, src:src + W]
        wgq_ref[r, :] = v[:, IN_GQ:IN_GK]
        wgv_ref[r, :] = v[:, IN_GV:IN_GR]
        wgr_ref[r, :] = v[:, IN_GR:IN_GD]

    _stream_cast([win_hbm.at[layer, r, pl.ds(0, IN_GD)] for r in rows], stage_ref, sem_ref, store)

    krows = [pl.ds(r, GK_PIECE_ROWS) for r in range(0, D_MODEL, GK_PIECE_ROWS)]

    def store_k(i, v):
        wgkt_ref[:, krows[i]] = v.T.astype(BF)

    _stream_cast([win_hbm.at[layer, r, pl.ds(IN_GK, GLA_KEY_DIM)] for r in krows],
                 stage_k_ref, sem_k_ref, store_k, cast=False)

    shift = modall_ref[:, 3, :].astype(BF)
    for s_ref, w_ref in zip(shifts, (watt_ref, wgq_ref, wgv_ref, wgr_ref, wgd_ref, wgate_ref)):
        s_ref[...] = _dot(shift, w_ref[...])


def _proj_body(h_ref, mod_ref, modall_ref, g_ref, wgd_ref, wgate_ref, hsum_ref, hexp_ref, qkg_ref,
               win_hbm,
               a0_ref, a1_ref, a2_ref, gq_ref, gkt_ref, gv_ref, gr_ref, gd_ref, gate_ref, perm_ref,
               watt_ref, wgq_ref, wgkt_ref, wgv_ref, wgr_ref,
               satt_ref, sgq_ref, sgv_ref, sgr_ref, sgd_ref, sgate_ref, *dma_scratch, layer):
    @pl.when(_first_step())
    def _():
        _load_proj_weights(layer, modall_ref, win_hbm, wgd_ref, wgate_ref,
                           (watt_ref, wgq_ref, wgkt_ref, wgv_ref, wgr_ref),
                           (satt_ref, sgq_ref, sgv_ref, sgr_ref, sgd_ref, sgate_ref), *dma_scratch)

    b = pl.ds(pl.program_id(0), 1)
    m = mod_ref[0]
    x = h_ref[0]
    xg = x * (g_ref[...] * (1.0 + m[4:5]))
    u = xg.astype(BF)
    rs = _row_scale(x)

    def proj(w, s):
        return rs * _dot(u, w) + s

    for gi, a_ref in enumerate((a0_ref, a1_ref, a2_ref)):
        d = ATT_GROUPS[gi][1]
        gcols = slice(gi * GRP_W, (gi + 1) * GRP_W)
        y = proj(watt_ref[:, gcols], satt_ref[b, gcols])
        qk = y[:, :QK_W]
        ss = _dot((qk * qk).astype(BF), hsum_ref[...])
        rx = _dot_split(lax.rsqrt(ss * (1.0 / ATT_HEAD_DIM) + EPS), hexp_ref)
        qkn = qk * rx * qkg_ref[...]
        if d == 1:
            a_ref[0, 0, :, :QK_W] = qkn.astype(BF)
            a_ref[0, 0, :, QK_W:] = y[:, QK_W:].astype(BF)
        else:
            for j in range(GRP_W // 128):
                cols = slice(j * 128, (j + 1) * 128)
                perm_ref[j] = qkn[:, cols] if j < QK_W // 128 else y[:, cols]
            for r in range(d):
                for j in range(GRP_W // 128):
                    a_ref[0, r, :, j * 128:(j + 1) * 128] = (
                        perm_ref[j, pl.ds(r, ROW_TILE // d, stride=d), :].astype(BF))
    gq_ref[0] = proj(wgq_ref[...], sgq_ref[b, :]).astype(BF)
    gv_ref[0] = proj(wgv_ref[...], sgv_ref[b, :]).astype(BF)
    gr_ref[0] = proj(wgr_ref[...], sgr_ref[b, :]).astype(BF)
    gd_ref[0] = proj(wgd_ref[...], sgd_ref[b, :]).astype(BF)
    gate_ref[0] = proj(wgate_ref[...], sgate_ref[b, :]).astype(BF)
    u_full = (xg * rs + m[3:4]).astype(BF)
    gkt_ref[0] = _dot_nt(wgkt_ref[...], u_full).astype(BF)


def _proj_call(h, mod, g, w_in, wgd, wgate, hsum, hexp, qkg, layer):
    def row(w):
        return pl.BlockSpec((1, ROW_TILE, w), lambda b, i: (b, i, 0))

    def out(w):
        return jax.ShapeDtypeStruct((BATCH, SEQ, w), BF)

    kt_spec = pl.BlockSpec((1, GLA_KEY_DIM, ROW_TILE), lambda b, i: (b, 0, i))
    kt_shape = jax.ShapeDtypeStruct((BATCH, GLA_KEY_DIM, SEQ), BF)
    widths = (GLA_VAL_DIM, GLA_VAL_DIM, GD_PAD, 2 * D_MODEL)
    consts = (mod, g, wgd, wgate, hsum, hexp, qkg)
    dils = [d for _, d in ATT_GROUPS]
    return pl.pallas_call(
        functools.partial(_proj_body, layer=layer),
        grid=(BATCH, SEQ // ROW_TILE),
        in_specs=[row(D_MODEL), pl.BlockSpec((1, N_MOD, D_MODEL), lambda b, i: (b, 0, 0))]
                 + [_const_spec(a.shape) for a in consts]
                 + [pl.BlockSpec(memory_space=pl.ANY)],
        out_specs=[_dilated_spec(d, GRP_W) for d in dils] + [row(GLA_KEY_DIM), kt_spec]
                  + [row(w) for w in widths],
        out_shape=[jax.ShapeDtypeStruct((BATCH, d, SEQ // d, GRP_W), BF) for d in dils]
                  + [out(GLA_KEY_DIM), kt_shape] + [out(w) for w in widths],
        scratch_shapes=[pltpu.VMEM((GRP_W // 128, ROW_TILE, 128), F32)]
                       + PROJ_WEIGHT_SCRATCH + PROJ_SHIFT_SCRATCH + PROJ_DMA_SCRATCH,
        compiler_params=_params("arbitrary", "arbitrary"),
        name="proj",
    )(h, mod, *consts, w_in)


ATT_GROUP_BLOCKS = 4


def _attn_blocks(a_ref, bias_ref, o_ref, l_ref, blocks):
    W, NH = ATT_OUT_WIDTH, ATT_HEADS_PER_GROUP
    lane = lax.broadcasted_iota(jnp.int32, (ATT_BLK, 128), 1)
    low_half = lane < ATT_HEAD_DIM
    zero_col = jnp.zeros((ATT_BLK, 128), BF)
    qs, ks, vs, biases, q0s = [], [], [], [], []
    for r, n, first in blocks:
        if first:
            q0, k0, nk = 0, 0, ATT_BLK
            biases.append(bias_ref[:, ATT_BLK:])
        else:
            q0, k0, nk = n * ATT_BLK, (n - 1) * ATT_BLK, 2 * ATT_BLK
            if not isinstance(n, int):
                q0, k0 = pl.multiple_of(q0, ATT_BLK), pl.multiple_of(k0, ATT_BLK)
            biases.append(bias_ref[...])
        q0s.append(q0)
        stacked = []
        for j in range(NH):
            col = a_ref[0, r, pl.ds(q0, ATT_BLK), (j // 2) * 128:(j // 2 + 1) * 128]
            col = jnp.where(low_half == (j % 2 == 0), col, zero_col)
            stacked.append(jnp.concatenate([col, zero_col] if j < 2 else [zero_col, col], axis=1))
        qs.append(jnp.concatenate(stacked, axis=0))
        ks.append(a_ref[0, r, pl.ds(k0, nk), W:2 * W])
        vs.append(a_ref[0, r, pl.ds(k0, nk), 2 * W:])
    ss = [_dot_nt(q4, k) + b for q4, k, b in zip(qs, ks, biases)]
    mxs = [jnp.max(s, axis=-1, keepdims=True) for s in ss]
    es = [jnp.exp2(s - mx) for s, mx in zip(ss, mxs)]
    dens = [jnp.sum(e, axis=-1, keepdims=True) for e in es]
    pvs = [_dot(e.astype(BF), v) for e, v in zip(es, vs)]
    for (r, _, _), q0, pv, mx, den in zip(blocks, q0s, pvs, mxs, dens):
        rden = 1.0 / den
        lse = mx * LN2 + jnp.log(den)
        l = jnp.zeros((ATT_BLK, 128), F32)
        heads = []
        for j in range(NH):
            rows = slice(j * ATT_BLK, (j + 1) * ATT_BLK)
            heads.append(pv[rows, (j // 2) * 128:(j // 2 + 1) * 128] * rden[rows])
            l = jnp.where(lane == j, lse[rows], l)
        o = jnp.concatenate([jnp.where(low_half, heads[0], heads[1]),
                             jnp.where(low_half, heads[2], heads[3])], axis=1)
        o_ref[0, r, pl.ds(q0, ATT_BLK), :] = o.astype(o_ref.dtype)
        l_ref[0, r, pl.ds(q0, ATT_BLK), :] = l


def _attn_body(a_ref, bias_ref, o_ref, l_ref, *, subseqs, n_blocks):
    G = ATT_GROUP_BLOCKS
    if n_blocks == 1:
        for r0 in range(0, subseqs, G):
            _attn_blocks(a_ref, bias_ref, o_ref, l_ref,
                         [(r, 0, True) for r in range(r0, min(r0 + G, subseqs))])
        return
    for r in range(subseqs):
        for n0 in range(0, n_blocks, G):
            _attn_blocks(a_ref, bias_ref, o_ref, l_ref,
                         [(r, n, n == 0) for n in range(n0, n0 + G)])


def _attn_bias(group):
    _, dilation = ATT_GROUPS[group]
    heads = np.arange(group * ATT_HEADS_PER_GROUP, (group + 1) * ATT_HEADS_PER_GROUP)
    slopes = (2.0 ** (-ALIBI_MAX * (heads + 1).astype(np.float32) / ATT_HEADS)).astype(np.float32)
    qi = np.arange(ATT_BLK)[:, None]
    kj = np.arange(2 * ATT_BLK)[None, :]
    steps = qi + ATT_BLK - kj
    valid = (steps >= 0) & (steps <= ATT_BLK)
    bias = -(slopes * np.float32(dilation))[:, None, None] * steps.astype(np.float32)[None]
    bias = np.where(valid[None], bias * np.float32(LOG2E), np.float32(NEG_BIG)).astype(np.float32)
    return bias.reshape(ATT_HEADS_PER_GROUP * ATT_BLK, 2 * ATT_BLK)


def _attn_call(a, group):
    _, d = ATT_GROUPS[group]
    L = SEQ // d
    n_blocks = L // ATT_BLK
    subseqs = d if n_blocks == 1 else min(d, 4)

    def spec(w):
        return pl.BlockSpec((1, subseqs, L, w), lambda b, r: (b, r, 0, 0))

    return pl.pallas_call(
        functools.partial(_attn_body, subseqs=subseqs, n_blocks=n_blocks),
        grid=(BATCH, d // subseqs),
        in_specs=[spec(GRP_W), _const_spec((ATT_HEADS_PER_GROUP * ATT_BLK, 2 * ATT_BLK))],
        out_specs=[spec(ATT_OUT_WIDTH), spec(128)],
        out_shape=[jax.ShapeDtypeStruct((BATCH, d, L, ATT_OUT_WIDTH), BF),
                   jax.ShapeDtypeStruct((BATCH, d, L, 128), F32)],
        compiler_params=_params("parallel", "parallel"),
        name="attn%d" % d,
    )(a, jnp.asarray(_attn_bias(group)))


GLA_PAIR = 2 * GLA_CHUNK


def _gla_cum_decay(gd_ref, up_ref, gb_ref, tril2_ref):
    x = _dot(gd_ref[0], up_ref[...]) + gb_ref[...]
    log_a = (jnp.minimum(x, 0.0) - jnp.log(1.0 + jnp.exp(-jnp.abs(x)))) * (1.0 / GLA_TAU)
    la_hi, la_lo = _split_bf16(log_a)
    pieces = []
    for p in range(GLA_BLOCK // GLA_PAIR):
        rows = slice(p * GLA_PAIR, (p + 1) * GLA_PAIR)
        pieces.append(_dot(tril2_ref[...], jnp.concatenate([la_hi[rows], la_lo[rows]], axis=0)))
    return pieces


def _gla_pair(p, b, states, gq_ref, gkt_ref, gv_ref, o_ref):
    C, P = GLA_CHUNK, GLA_PAIR
    rows = slice(p * P, (p + 1) * P)
    q_dec =(gq_ref[0, rows, :].astype(F32) * (GLA_DK ** -0.5) * jnp.exp(b)).astype(BF)
    b_t = b.T
    k_dec_t_bf = (gkt_ref[0, :, rows].astype(F32) * jnp.exp(-b_t)).astype(BF)
    decay_t = jnp.exp(b_t)
    lane = lax.broadcasted_iota(jnp.int32, (GLA_KEY_DIM, P), 1)
    zero = jnp.zeros_like(k_dec_t_bf)
    k_dec_t_chunk = [jnp.where(lane < C, k_dec_t_bf, zero), jnp.where(lane >= C, k_dec_t_bf, zero)]
    ri = lax.broadcasted_iota(jnp.int32, (P, P), 0)
    ci = lax.broadcasted_iota(jnp.int32, (P, P), 1)
    causal = (ri >= ci) & ((ri >= C) == (ci >= C))
    new_states = []
    for h in range(GLA_HEADS):
        kc = slice(h * GLA_DK, (h + 1) * GLA_DK)
        vc = slice(h * GLA_DV, (h + 1) * GLA_DV)
        v = gv_ref[0, rows, vc]
        attn = jnp.where(causal, _dot(q_dec[:, kc], k_dec_t_bf[kc, :]), 0.0).astype(BF)
        stacked = _dot(jnp.concatenate([k_dec_t_chunk[0][kc, :], k_dec_t_chunk[1][kc, :], attn],
                                       axis=0), v)
        st = states[h]
        for cc in range(2):
            crow = slice(cc * C, (cc + 1) * C)
            o = stacked[2 * GLA_DK + cc * C:2 * GLA_DK + (cc + 1) * C] + _dot(q_dec[crow, kc],
                                                                              st.astype(BF))
            o_ref[0, p * P + cc * C:p * P + (cc + 1) * C, vc] = o.astype(o_ref.dtype)
            last = (cc + 1) * C - 1
            st = decay_t[kc, last:last + 1] * (st + stacked[cc * GLA_DK:(cc + 1) * GLA_DK])
        new_states.append(st)
    return new_states


def _gla_body(gq_ref, gkt_ref, gv_ref, gd_ref, up_ref, gb_ref, tril2_ref, o_ref, state_ref):
    @pl.when(pl.program_id(1) == 0)
    def _():
        state_ref[...] = jnp.zeros_like(state_ref)

    bs = _gla_cum_decay(gd_ref, up_ref, gb_ref, tril2_ref)
    states = [state_ref[h] for h in range(GLA_HEADS)]
    for p in range(GLA_BLOCK // GLA_PAIR):
        states = _gla_pair(p, bs[p], states, gq_ref, gkt_ref, gv_ref, o_ref)
    for h in range(GLA_HEADS):
        state_ref[h] = states[h]


def _pair_tril2():
    i = np.arange(GLA_PAIR)
    t = ((i[:, None] // GLA_CHUNK) == (i[None, :] // GLA_CHUNK)) & (i[:, None] >= i[None, :])
    return np.concatenate([t, t], axis=1).astype(np.float32)


def _gla_call(gq, gkt, gv, gd, up, gbias):
    def row(w):
        return pl.BlockSpec((1, GLA_BLOCK, w), lambda b, i: (b, i, 0))

    consts = (up, gbias, jnp.asarray(_pair_tril2(), BF))
    return pl.pallas_call(
        _gla_body,
        grid=(BATCH, SEQ // GLA_BLOCK),
        in_specs=[row(GLA_KEY_DIM),
                  pl.BlockSpec((1, GLA_KEY_DIM, GLA_BLOCK), lambda b, i: (b, 0, i)),
                  row(GLA_VAL_DIM), row(GD_PAD)]
                 + [_const_spec(a.shape) for a in consts],
        out_specs=row(GLA_VAL_DIM),
        out_shape=jax.ShapeDtypeStruct((BATCH, SEQ, GLA_VAL_DIM), BF),
        scratch_shapes=[pltpu.VMEM((GLA_HEADS, GLA_DK, GLA_DV), F32)],
        compiler_params=_params("parallel", "arbitrary"),
        name="gla",
    )(gq, gkt, gv, gd, *consts)


def _undilate(src_ref, scr_ref, d):
    if d == 1:
        return src_ref[0, 0].astype(F32)
    planes = scr_ref.shape[0]
    for r in range(d):
        x = src_ref[0, r].astype(F32)
        for j in range(planes):
            scr_ref[j, pl.ds(r, ROW_TILE // d, stride=d), :] = x[:, j * 128:(j + 1) * 128]
    return jnp.concatenate([scr_ref[j] for j in range(planes)], axis=-1)


def _merge_body(h_ref, mod_ref, o0_ref, o1_ref, o2_ref, l0_ref, l1_ref, l2_ref,
                ogla_ref, gr_ref, gate_ref, modall_ref, hexp_ref, gain_ref, wba_ref, wbg_ref,
                wout_ref, g_ref, w1_hbm, w3_hbm, w2_hbm, out_ref,
                os1_ref, os2_ref, ls1_ref, ls2_ref,
                w1_ref, w3_ref, w2_ref, s1_ref, s3_ref, *dma_scratch, layer):
    @pl.when(_first_step())
    def _():
        _load_ffn_weights(layer, 6, modall_ref, w1_hbm, w3_hbm, w2_hbm,
                          w1_ref, w3_ref, w2_ref, s1_ref, s3_ref, *dma_scratch)

    dils = [d for _, d in ATT_GROUPS]
    lses = [_undilate(r, s, d) for r, s, d in
            zip((l0_ref, l1_ref, l2_ref), (None, ls1_ref, ls2_ref), dils)]
    outs = [_undilate(r, s, d) for r, s, d in
            zip((o0_ref, o1_ref, o2_ref), (None, os1_ref, os2_ref), dils)]
    mx = jnp.maximum(jnp.maximum(lses[0], lses[1]), lses[2])
    es = [jnp.exp(l - mx) for l in lses]
    den = es[0] + es[1] + es[2]
    o_att = None
    for e, o in zip(es, outs):
        t = _dot_split(e / den, hexp_ref) * o
        o_att = t if o_att is None else o_att + t
    att = _dot(o_att.astype(BF), wba_ref[...])
    gated = []
    for h in range(GLA_HEADS):
        vc = slice(h * GLA_DV, (h + 1) * GLA_DV)
        o = ogla_ref[0, :, vc].astype(F32)
        ms = jnp.mean(o * o, axis=-1, keepdims=True)
        y = o * lax.rsqrt(ms + EPS) * gain_ref[:, vc]
        gated.append(y.astype(BF) * _silu(gr_ref[0, :, vc]))
    gla = _dot(jnp.concatenate(gated, axis=1), wbg_ref[...])
    merged = (_sigmoid(gate_ref[0, :, :D_MODEL]) * att.astype(BF)
              + _sigmoid(gate_ref[0, :, D_MODEL:]) * gla.astype(BF))
    m = mod_ref[0]
    h = h_ref[0] + (1.0 + m[5:6]) * _dot(merged, wout_ref[...])
    b = pl.program_id(0)
    out_ref[0] = _ffn_tiles([h], m, 6, g_ref, s1_ref[pl.ds(b, 1), :], s3_ref[pl.ds(b, 1), :],
                            w1_ref, w3_ref, w2_ref)[0]


def _merge_ffn_call(h, mod, os_, ls_, ogla, gr, gates, hexp, gain, wba, wbg, wout, g, w1, w3, w2,
                    layer):
    def row(w):
        return pl.BlockSpec((1, ROW_TILE, w), lambda b, i: (b, i, 0))

    consts = (mod, hexp, gain, wba, wbg, wout, g)
    hbm = pl.BlockSpec(memory_space=pl.ANY)
    dils = [d for _, d in ATT_GROUPS]
    return pl.pallas_call(
        functools.partial(_merge_body, layer=layer),
        grid=(BATCH, SEQ // ROW_TILE),
        in_specs=[row(D_MODEL), pl.BlockSpec((1, N_MOD, D_MODEL), lambda b, i: (b, 0, 0))]
                 + [_dilated_spec(d, ATT_OUT_WIDTH) for d in dils]
                 + [_dilated_spec(d, 128) for d in dils]
                 + [row(GLA_VAL_DIM), row(GLA_VAL_DIM), row(2 * D_MODEL)]
                 + [_const_spec(a.shape) for a in consts] + [hbm, hbm, hbm],
        out_specs=row(D_MODEL),
        out_shape=jax.ShapeDtypeStruct((BATCH, SEQ, D_MODEL), F32),
        scratch_shapes=[pltpu.VMEM((ATT_OUT_WIDTH // 128, ROW_TILE, 128), F32)] * 2
                       + [pltpu.VMEM((1, ROW_TILE, 128), F32)] * 2 + FFN_WEIGHT_SCRATCH,
        compiler_params=_params("arbitrary", "arbitrary"),
        name="merge_ffn",
    )(h, mod, *os_, *ls_, ogla, gr, gates, *consts, w1, w3, w2)


def _head_matrix(n_lanes_in, width):
    j = np.arange(n_lanes_in)[:, None]
    c = np.arange(width)[None, :]
    return (c // ATT_HEAD_DIM == j).astype(np.float32)


def _layer(l, h, mod, g_ffn1, f1w1, f1w3, f1w2, g_mix, w_in, q_norm_g, k_norm_g, gate_up, gate_bias,
           out_norm_g, w_branch_att, w_branch_gla, w_out, g_ffn2, f2w1, f2w3, f2w2):
    bf = lambda w: w.astype(BF)
    row = lambda v: v.reshape(1, -1)

    h = _ffn_call(h, mod, row(g_ffn1), f1w1, f1w3, f1w2, 0, l)

    wgd = bf(jnp.pad(w_in[l, :, IN_GD:IN_GATES], ((0, 0), (0, GD_PAD - GLA_GATE_RANK))))
    wgate = bf(w_in[l, :, IN_GATES:])
    head_qk = _head_matrix(128, QK_W)
    hsum_qk = jnp.asarray(head_qk.T, BF)
    hexp_qk = jnp.asarray(np.concatenate([head_qk, head_qk]), BF)
    qkg = jnp.concatenate([jnp.tile(q_norm_g * (ATT_HEAD_DIM ** -0.5 * LOG2E), ATT_HEADS_PER_GROUP),
                           jnp.tile(k_norm_g, ATT_HEADS_PER_GROUP)])
    a0, a1, a2, pgq, pgkt, pgv, pgr, pgd, pgate = _proj_call(
        h, mod, row(g_mix), w_in, wgd, wgate, hsum_qk, hexp_qk, row(qkg), l)

    outs = [_attn_call(a, gi) for gi, a in enumerate((a0, a1, a2))]
    up = jnp.pad(bf(gate_up), ((0, GD_PAD - GLA_GATE_RANK), (0, 0)))
    ogla = _gla_call(pgq, pgkt, pgv, pgd, up, row(gate_bias))

    head_o = _head_matrix(128, ATT_OUT_WIDTH)
    hexp_o = jnp.asarray(np.concatenate([head_o, head_o]), BF)
    return _merge_ffn_call(h, mod, [o for o, _ in outs], [l for _, l in outs], ogla, pgr, pgate,
                           hexp_o, row(jnp.tile(out_norm_g, GLA_HEADS)), bf(w_branch_att),
                           bf(w_branch_gla), bf(w_out), row(g_ffn2), f2w1, f2w3, f2w2, l)


def kernel(x, c, w_mod, b_mod, g_ffn1, ffn1_w1, ffn1_w3, ffn1_w2, g_mix, w_in, q_norm_g, k_norm_g,
           gla_gate_up, gla_gate_bias, gla_out_norm_g, w_branch_att, w_branch_gla, w_out,
           g_ffn2, ffn2_w1, ffn2_w3, ffn2_w2):
    h = x
    for l in range(w_mod.shape[0]):
        mod = _mod_call(c, w_mod, b_mod[l], l).reshape(BATCH, N_MOD, D_MODEL)
        h = _layer(l, h, mod, g_ffn1[l], ffn1_w1, ffn1_w3, ffn1_w2, g_mix[l], w_in,
                   q_norm_g[l], k_norm_g[l], gla_gate_up[l], gla_gate_bias[l], gla_out_norm_g[l],
                   w_branch_att[l], w_branch_gla[l], w_out[l], g_ffn2[l], ffn2_w1, ffn2_w3,
                   ffn2_w2)
    return h
```

```python
import functools

import numpy as np
import jax
import jax.numpy as jnp
from jax import lax
from jax.experimental import pallas as pl
from jax.experimental.pallas import tpu as pltpu

D_MODEL = 1024
BATCH = 16
SEQ = 2048
N_MOD = 9
D_FF = 2816
EPS = 1e-6
ATT_GROUPS = ((128, 1), (512, 4), (2048, 16))
ATT_HEADS_PER_GROUP = 4
ATT_HEADS = 12
ATT_HEAD_DIM = 64
ATT_WIDTH = 768
ATT_OUT_WIDTH = 256
ALIBI_MAX = 8.0
GLA_HEADS = 4
GLA_KEY_DIM = 512
GLA_VAL_DIM = 1024
GLA_DK = 128
GLA_DV = 256
GLA_GATE_RANK = 16
GLA_TAU = 16.0
GLA_CHUNK = 64
ATT_BLK = 128
NEG_BIG = -1e30
LOG2E = float(np.log2(np.e))
LN2 = float(np.log(2.0))

BF = jnp.bfloat16
F32 = jnp.float32

VMEM_LIMIT_BYTES = 56 * 1024 * 1024

ROW_TILE = 512
FFN_TILE = 1024
FFN_SUBTILES = 2
MXU_TILE = 256
FF_CHUNK_TILES = 4
GLA_BLOCK = 512


def _dot(a, b):
    return jnp.dot(a, b, preferred_element_type=F32)


def _dot_nt(a, b):
    return lax.dot_general(a, b, (((1,), (1,)), ((), ())), preferred_element_type=F32)


def _split_bf16(x):
    hi = x.astype(BF)
    lo = (x - hi.astype(F32)).astype(BF)
    return hi, lo


def _dot_split(x, stacked_ref):
    hi, lo = _split_bf16(x)
    return _dot(jnp.concatenate([hi, lo], axis=1), stacked_ref[...])


def _sigmoid(x):
    return 0.5 * jnp.tanh(0.5 * x) + 0.5


def _silu(x):
    return x * _sigmoid(x)


def _const_spec(shape):
    nd = len(shape)
    return pl.BlockSpec(shape, lambda *_: (0,) * nd, pipeline_mode=pl.Buffered(1))


def _dilated_spec(d, width):
    return pl.BlockSpec((1, d, ROW_TILE // d, width), lambda b, i: (b, 0, i, 0))


def _params(*sem):
    return pltpu.CompilerParams(dimension_semantics=sem, vmem_limit_bytes=VMEM_LIMIT_BYTES)


def _mod_body(c_ref, w_ref, b_ref, o_ref):
    c = c_ref[...]
    o_ref[...] = _dot(_silu(c).astype(BF), w_ref[...].astype(BF)) + b_ref[...]


def _mod_call(c, w_mod, b_mod, layer):
    n = w_mod.shape[2]
    bn = D_MODEL
    return pl.pallas_call(
        _mod_body,
        grid=(n // bn,),
        in_specs=[pl.BlockSpec((BATCH, D_MODEL), lambda j: (0, 0)),
                  pl.BlockSpec((None, D_MODEL, bn), lambda j: (layer, 0, j)),
                  pl.BlockSpec((1, bn), lambda j: (0, j))],
        out_specs=pl.BlockSpec((BATCH, bn), lambda j: (0, j)),
        out_shape=jax.ShapeDtypeStruct((BATCH, n), F32),
        compiler_params=_params("arbitrary"),
        name="mod",
    )(c, w_mod, b_mod.reshape(1, n))


def _row_scale(x):
    return lax.rsqrt(jnp.mean(x * x, axis=-1, keepdims=True) + EPS)


UP_PIECE_ROWS = 128
DOWN_PIECE_ROWS = 256

FFN_WEIGHT_SCRATCH = [
    pltpu.VMEM((D_MODEL, D_FF), BF), pltpu.VMEM((D_MODEL, D_FF), BF), pltpu.VMEM((D_FF, D_MODEL), BF),
    pltpu.VMEM((BATCH, D_FF), F32), pltpu.VMEM((BATCH, D_FF), F32),
    pltpu.VMEM((2, UP_PIECE_ROWS, D_FF), F32), pltpu.VMEM((2, DOWN_PIECE_ROWS, D_MODEL), F32),
    pltpu.SemaphoreType.DMA((2,)), pltpu.SemaphoreType.DMA((2,))]


def _stream_cast(pieces, stage_ref, sem_ref, store, cast=True):
    def copy(i):
        return pltpu.make_async_copy(pieces[i], stage_ref.at[i % 2], sem_ref.at[i % 2])

    copy(0).start()
    for i in range(len(pieces)):
        if i + 1 < len(pieces):
            copy(i + 1).start()
        copy(i).wait()
        v = stage_ref[i % 2]
        store(i, v.astype(BF) if cast else v)


def _load_ffn_weights(layer, mod_row, modall_ref, w1_hbm, w3_hbm, w2_hbm,
                      w1_ref, w3_ref, w2_ref, s1_ref, s3_ref, stage_up, stage_dn, sem_up, sem_dn):
    up_rows = [pl.ds(r, UP_PIECE_ROWS) for r in range(0, D_MODEL, UP_PIECE_ROWS)]
    dn_rows = [pl.ds(r, DOWN_PIECE_ROWS) for r in range(0, D_FF, DOWN_PIECE_ROWS)]

    def store_up(i, v):
        dst = w1_ref if i < len(up_rows) else w3_ref
        dst[up_rows[i % len(up_rows)], :] = v

    def store_dn(i, v):
        w2_ref[dn_rows[i], :] = v

    _stream_cast([w.at[layer, r, :] for w in (w1_hbm, w3_hbm) for r in up_rows],
                 stage_up, sem_up, store_up)
    _stream_cast([w2_hbm.at[layer, r, :] for r in dn_rows], stage_dn, sem_dn, store_dn)
    shift = modall_ref[:, mod_row, :].astype(BF)
    s1_ref[...] = _dot(shift, w1_ref[...])
    s3_ref[...] = _dot(shift, w3_ref[...])


def _first_step():
    return jnp.logical_and(pl.program_id(0) == 0, pl.program_id(1) == 0)


def _ffn_tiles(xs, m, mod_row, g_ref, s1, s3, w1_ref, w3_ref, w2_ref):
    scale, gate = m[mod_row + 1:mod_row + 2], m[mod_row + 2:mod_row + 3]
    fc = FF_CHUNK_TILES * MXU_TILE
    gs = g_ref[...] * (1.0 + scale)
    us = [(x * gs).astype(BF) for x in xs]
    rs = [_row_scale(x) for x in xs]
    accs = [None] * len(xs)
    for lo in range(0, D_FF, fc):
        cols = slice(lo, min(lo + fc, D_FF))
        for s, u in enumerate(us):
            h1 = rs[s] * _dot(u, w1_ref[:, cols]) + s1[:, cols]
            h3 = rs[s] * _dot(u, w3_ref[:, cols]) + s3[:, cols]
            p = _dot((_silu(h1) * h3).astype(BF), w2_ref[cols, :])
            accs[s] = p if accs[s] is None else accs[s] + p
    return [x + (0.5 * (1.0 + gate)) * acc for x, acc in zip(xs, accs)]


def _ffn_body(x_ref, mod_ref, modall_ref, g_ref, w1_hbm, w3_hbm, w2_hbm, o_ref,
              w1_ref, w3_ref, w2_ref, s1_ref, s3_ref, *dma_scratch, mod_row, layer):
    @pl.when(_first_step())
    def _():
        _load_ffn_weights(layer, mod_row, modall_ref, w1_hbm, w3_hbm, w2_hbm,
                          w1_ref, w3_ref, w2_ref, s1_ref, s3_ref, *dma_scratch)

    b = pl.program_id(0)
    sub = FFN_TILE // FFN_SUBTILES
    rows = [slice(s * sub, (s + 1) * sub) for s in range(FFN_SUBTILES)]
    outs = _ffn_tiles([x_ref[0, r, :] for r in rows], mod_ref[0], mod_row, g_ref,
                      s1_ref[pl.ds(b, 1), :], s3_ref[pl.ds(b, 1), :], w1_ref, w3_ref, w2_ref)
    for r, o in zip(rows, outs):
        o_ref[0, r, :] = o


def _ffn_call(x, mod, g, w1, w3, w2, mod_row, layer):
    row = pl.BlockSpec((1, FFN_TILE, D_MODEL), lambda b, i: (b, i, 0))
    hbm = pl.BlockSpec(memory_space=pl.ANY)
    return pl.pallas_call(
        functools.partial(_ffn_body, mod_row=mod_row, layer=layer),
        grid=(BATCH, SEQ // FFN_TILE),
        in_specs=[row,
                  pl.BlockSpec((1, N_MOD, D_MODEL), lambda b, i: (b, 0, 0)),
                  _const_spec((BATCH, N_MOD, D_MODEL)),
                  _const_spec((1, D_MODEL)),
                  hbm, hbm, hbm],
        out_specs=row,
        out_shape=jax.ShapeDtypeStruct((BATCH, SEQ, D_MODEL), F32),
        scratch_shapes=FFN_WEIGHT_SCRATCH,
        compiler_params=_params("arbitrary", "arbitrary"),
        name="ffn%d" % mod_row,
    )(x, mod, mod, g, w1, w3, w2)


QK_W = 2 * ATT_OUT_WIDTH
GRP_W = 3 * ATT_OUT_WIDTH
GD_PAD = 128


IN_GQ = 3 * ATT_WIDTH
IN_GK = IN_GQ + GLA_KEY_DIM
IN_GV = IN_GK + GLA_KEY_DIM
IN_GR = IN_GV + GLA_VAL_DIM
IN_GD = IN_GR + GLA_VAL_DIM
IN_GATES = IN_GD + GLA_GATE_RANK
IN_PIECE_ROWS = 64
GK_PIECE_ROWS = 128

PROJ_WEIGHT_SCRATCH = [
    pltpu.VMEM((D_MODEL, 3 * GRP_W), BF), pltpu.VMEM((D_MODEL, GLA_KEY_DIM), BF),
    pltpu.VMEM((GLA_KEY_DIM, D_MODEL), BF), pltpu.VMEM((D_MODEL, GLA_VAL_DIM), BF),
    pltpu.VMEM((D_MODEL, GLA_VAL_DIM), BF), pltpu.VMEM((D_MODEL, GD_PAD), BF)]
PROJ_SHIFT_SCRATCH = [pltpu.VMEM((BATCH, w), F32) for w in
                      (3 * GRP_W, GLA_KEY_DIM, GLA_VAL_DIM, GLA_VAL_DIM, GD_PAD, 2 * D_MODEL)]
PROJ_DMA_SCRATCH = [
    pltpu.VMEM((2, IN_PIECE_ROWS, IN_GD + GD_PAD), F32),
    pltpu.VMEM((2, GK_PIECE_ROWS, GLA_KEY_DIM), F32),
    pltpu.SemaphoreType.DMA((2,)), pltpu.SemaphoreType.DMA((2,))]


def _load_proj_weights(layer, modall_ref, win_hbm, wgate_ref, weights, shifts,
                       stage_ref, stage_k_ref, sem_ref, sem_k_ref):
    watt_ref, wgq_ref, wgkt_ref, wgv_ref, wgr_ref, wgd_ref = weights
    gd_lane = lax.broadcasted_iota(jnp.int32, (IN_PIECE_ROWS, GD_PAD), 1) < GLA_GATE_RANK
    W = ATT_OUT_WIDTH
    rows = [pl.ds(r, IN_PIECE_ROWS) for r in range(0, D_MODEL, IN_PIECE_ROWS)]

    def store(i, v):
        r = rows[i]
        for gi in range(len(ATT_GROUPS)):
            for t in range(3):
                src = t * ATT_WIDTH + gi * W
                watt_ref[r, gi * GRP_W + t * W:gi * GRP_W + (t + 1) * W] = v[:, src:src + W]
        wgq_ref[r, :] = v[:, IN_GQ:IN_GK]
        wgv_ref[r, :] = v[:, IN_GV:IN_GR]
        wgr_ref[r, :] = v[:, IN_GR:IN_GD]
        gd = v[:, IN_GD:IN_GD + GD_PAD]
        wgd_ref[r, :] = jnp.where(gd_lane, gd, jnp.zeros_like(gd))

    _stream_cast([win_hbm.at[layer, r, pl.ds(0, IN_GD + GD_PAD)] for r in rows],
                 stage_ref, sem_ref, store)

    krows = [pl.ds(r, GK_PIECE_ROWS) for r in range(0, D_MODEL, GK_PIECE_ROWS)]

    def store_k(i, v):
        wgkt_ref[:, krows[i]] = v.T.astype(BF)

    _stream_cast([win_hbm.at[layer, r, pl.ds(IN_GK, GLA_KEY_DIM)] for r in krows],
                 stage_k_ref, sem_k_ref, store_k, cast=False)

    shift = modall_ref[:, 3, :].astype(BF)
    for s_ref, w_ref in zip(shifts, (watt_ref, wgq_ref, wgv_ref, wgr_ref, wgd_ref, wgate_ref)):
        s_ref[...] = _dot(shift, w_ref[...])


def _proj_body(h_ref, mod_ref, modall_ref, g_ref, wgate_ref, hsum_ref, hexp_ref, qkg_ref,
               win_hbm,
               a0_ref, a1_ref, a2_ref, gq_ref, gkt_ref, gv_ref, gr_ref, gd_ref, gate_ref, perm_ref,
               watt_ref, wgq_ref, wgkt_ref, wgv_ref, wgr_ref, wgd_ref,
               satt_ref, sgq_ref, sgv_ref, sgr_ref, sgd_ref, sgate_ref, *dma_scratch, layer):
    @pl.when(_first_step())
    def _():
        _load_proj_weights(layer, modall_ref, win_hbm, wgate_ref,
                           (watt_ref, wgq_ref, wgkt_ref, wgv_ref, wgr_ref, wgd_ref),
                           (satt_ref, sgq_ref, sgv_ref, sgr_ref, sgd_ref, sgate_ref), *dma_scratch)

    b = pl.ds(pl.program_id(0), 1)
    m = mod_ref[0]
    x = h_ref[0]
    xg = x * (g_ref[...] * (1.0 + m[4:5]))
    u = xg.astype(BF)
    rs = _row_scale(x)

    def proj(w, s):
        return rs * _dot(u, w) + s

    for gi, a_ref in enumerate((a0_ref, a1_ref, a2_ref)):
        d = ATT_GROUPS[gi][1]
        gcols = slice(gi * GRP_W, (gi + 1) * GRP_W)
        y = proj(watt_ref[:, gcols], satt_ref[b, gcols])
        qk = y[:, :QK_W]
        ss = _dot((qk * qk).astype(BF), hsum_ref[...])
        rx = _dot_split(lax.rsqrt(ss * (1.0 / ATT_HEAD_DIM) + EPS), hexp_ref)
        qkn = qk * rx * qkg_ref[...]
        if d == 1:
            a_ref[0, 0, :, :QK_W] = qkn.astype(BF)
            a_ref[0, 0, :, QK_W:] = y[:, QK_W:].astype(BF)
        else:
            for j in range(GRP_W // 128):
                cols = slice(j * 128, (j + 1) * 128)
                perm_ref[j] = qkn[:, cols] if j < QK_W // 128 else y[:, cols]
            for r in range(d):
                for j in range(GRP_W // 128):
                    a_ref[0, r, :, j * 128:(j + 1) * 128] = (
                        perm_ref[j, pl.ds(r, ROW_TILE // d, stride=d), :].astype(BF))
    gq_ref[0] = proj(wgq_ref[...], sgq_ref[b, :]).astype(BF)
    gv_ref[0] = proj(wgv_ref[...], sgv_ref[b, :]).astype(BF)
    gr_ref[0] = proj(wgr_ref[...], sgr_ref[b, :]).astype(BF)
    gd_ref[0] = proj(wgd_ref[...], sgd_ref[b, :]).astype(BF)
    gate_ref[0] = proj(wgate_ref[...], sgate_ref[b, :]).astype(BF)
    u_full = (xg * rs + m[3:4]).astype(BF)
    gkt_ref[0] = _dot_nt(wgkt_ref[...], u_full).astype(BF)


GATE_PIECE_ROWS = 256
GATE_OFFSET = IN_GATES - IN_GD


def _gate_weights_body(win_hbm, shift_ref, o_ref, stage_ref, tail_ref, sem_ref, tail_sem, *, layer):
    n_blk = 2 * D_MODEL // 128
    rows = [pl.ds(r, GATE_PIECE_ROWS) for r in range(0, D_MODEL, GATE_PIECE_ROWS)]
    tail_copy = pltpu.make_async_copy(
        win_hbm.at[layer, :, pl.ds(IN_GD + 2 * D_MODEL, GATE_OFFSET)], tail_ref, tail_sem)
    tail_copy.start()
    tail_copy.wait()

    def store(i, v):
        for j in range(n_blk - 1):
            pair = v[:, j * 128:(j + 2) * 128]
            o_ref[rows[i], j * 128:(j + 1) * 128] = _dot(pair, shift_ref[...]).astype(BF)
        last = (_dot(v[:, (n_blk - 1) * 128:], shift_ref[:128, :])
                + _dot(tail_ref[rows[i], :].astype(BF), shift_ref[128:128 + GATE_OFFSET, :]))
        o_ref[rows[i], (n_blk - 1) * 128:] = last.astype(BF)

    _stream_cast([win_hbm.at[layer, r, pl.ds(IN_GD, 2 * D_MODEL)] for r in rows],
                 stage_ref, sem_ref, store)


def _gate_weights_call(w_in, layer):
    k = np.arange(256)[:, None]
    c = np.arange(128)[None, :]
    shift = jnp.asarray((k == c + GATE_OFFSET).astype(np.float32), BF)
    return pl.pallas_call(
        functools.partial(_gate_weights_body, layer=layer),
        grid=(1,),
        in_specs=[pl.BlockSpec(memory_space=pl.ANY), _const_spec(shift.shape)],
        out_specs=pl.BlockSpec((D_MODEL, 2 * D_MODEL), lambda i: (0, 0)),
        out_shape=jax.ShapeDtypeStruct((D_MODEL, 2 * D_MODEL), BF),
        scratch_shapes=[pltpu.VMEM((2, GATE_PIECE_ROWS, 2 * D_MODEL), F32),
                        pltpu.VMEM((D_MODEL, GATE_OFFSET), F32),
                        pltpu.SemaphoreType.DMA((2,)), pltpu.SemaphoreType.DMA(())],
        compiler_params=_params("arbitrary"),
        name="gate_weights",
    )(w_in, shift)


def _proj_call(h, mod, g, w_in, wgate, hsum, hexp, qkg, layer):
    def row(w):
        return pl.BlockSpec((1, ROW_TILE, w), lambda b, i: (b, i, 0))

    def out(w):
        return jax.ShapeDtypeStruct((BATCH, SEQ, w), BF)

    kt_spec = pl.BlockSpec((1, GLA_KEY_DIM, ROW_TILE), lambda b, i: (b, 0, i))
    kt_shape = jax.ShapeDtypeStruct((BATCH, GLA_KEY_DIM, SEQ), BF)
    widths = (GLA_VAL_DIM, GLA_VAL_DIM, GD_PAD, 2 * D_MODEL)
    consts = (mod, g, wgate, hsum, hexp, qkg)
    dils = [d for _, d in ATT_GROUPS]
    return pl.pallas_call(
        functools.partial(_proj_body, layer=layer),
        grid=(BATCH, SEQ // ROW_TILE),
        in_specs=[row(D_MODEL), pl.BlockSpec((1, N_MOD, D_MODEL), lambda b, i: (b, 0, 0))]
                 + [_const_spec(a.shape) for a in consts]
                 + [pl.BlockSpec(memory_space=pl.ANY)],
        out_specs=[_dilated_spec(d, GRP_W) for d in dils] + [row(GLA_KEY_DIM), kt_spec]
                  + [row(w) for w in widths],
        out_shape=[jax.ShapeDtypeStruct((BATCH, d, SEQ // d, GRP_W), BF) for d in dils]
                  + [out(GLA_KEY_DIM), kt_shape] + [out(w) for w in widths],
        scratch_shapes=[pltpu.VMEM((GRP_W // 128, ROW_TILE, 128), F32)]
                       + PROJ_WEIGHT_SCRATCH + PROJ_SHIFT_SCRATCH + PROJ_DMA_SCRATCH,
        compiler_params=_params("arbitrary", "arbitrary"),
        name="proj",
    )(h, mod, *consts, w_in)


ATT_GROUP_BLOCKS = 4


def _attn_blocks(a_ref, bias_ref, o_ref, l_ref, blocks):
    W, NH = ATT_OUT_WIDTH, ATT_HEADS_PER_GROUP
    lane = lax.broadcasted_iota(jnp.int32, (ATT_BLK, 128), 1)
    low_half = lane < ATT_HEAD_DIM
    zero_col = jnp.zeros((ATT_BLK, 128), BF)
    qs, ks, vs, biases, q0s = [], [], [], [], []
    for r, n, first in blocks:
        if first:
            q0, k0, nk = 0, 0, ATT_BLK
            biases.append(bias_ref[:, ATT_BLK:])
        else:
            q0, k0, nk = n * ATT_BLK, (n - 1) * ATT_BLK, 2 * ATT_BLK
            if not isinstance(n, int):
                q0, k0 = pl.multiple_of(q0, ATT_BLK), pl.multiple_of(k0, ATT_BLK)
            biases.append(bias_ref[...])
        q0s.append(q0)
        stacked = []
        for j in range(NH):
            col = a_ref[0, r, pl.ds(q0, ATT_BLK), (j // 2) * 128:(j // 2 + 1) * 128]
            col = jnp.where(low_half == (j % 2 == 0), col, zero_col)
            stacked.append(jnp.concatenate([col, zero_col] if j < 2 else [zero_col, col], axis=1))
        qs.append(jnp.concatenate(stacked, axis=0))
        ks.append(a_ref[0, r, pl.ds(k0, nk), W:2 * W])
        vs.append(a_ref[0, r, pl.ds(k0, nk), 2 * W:])
    ss = [_dot_nt(q4, k) + b for q4, k, b in zip(qs, ks, biases)]
    mxs = [jnp.max(s, axis=-1, keepdims=True) for s in ss]
    es = [jnp.exp2(s - mx) for s, mx in zip(ss, mxs)]
    dens = [jnp.sum(e, axis=-1, keepdims=True) for e in es]
    pvs = [_dot(e.astype(BF), v) for e, v in zip(es, vs)]
    for (r, _, _), q0, pv, mx, den in zip(blocks, q0s, pvs, mxs, dens):
        rden = 1.0 / den
        lse = mx * LN2 + jnp.log(den)
        l = jnp.zeros((ATT_BLK, 128), F32)
        heads = []
        for j in range(NH):
            rows = slice(j * ATT_BLK, (j + 1) * ATT_BLK)
            heads.append(pv[rows, (j // 2) * 128:(j // 2 + 1) * 128] * rden[rows])
            l = jnp.where(lane == j, lse[rows], l)
        o = jnp.concatenate([jnp.where(low_half, heads[0], heads[1]),
                             jnp.where(low_half, heads[2], heads[3])], axis=1)
        o_ref[0, r, pl.ds(q0, ATT_BLK), :] = o.astype(o_ref.dtype)
        l_ref[0, r, pl.ds(q0, ATT_BLK), :] = l


def _attn_body(a_ref, bias_ref, o_ref, l_ref, *, subseqs, n_blocks):
    G = ATT_GROUP_BLOCKS
    if n_blocks == 1:
        for r0 in range(0, subseqs, G):
            _attn_blocks(a_ref, bias_ref, o_ref, l_ref,
                         [(r, 0, True) for r in range(r0, min(r0 + G, subseqs))])
        return
    for r in range(subseqs):
        for n0 in range(0, n_blocks, G):
            _attn_blocks(a_ref, bias_ref, o_ref, l_ref,
                         [(r, n, n == 0) for n in range(n0, n0 + G)])


def _attn_bias(group):
    _, dilation = ATT_GROUPS[group]
    heads = np.arange(group * ATT_HEADS_PER_GROUP, (group + 1) * ATT_HEADS_PER_GROUP)
    slopes = (2.0 ** (-ALIBI_MAX * (heads + 1).astype(np.float32) / ATT_HEADS)).astype(np.float32)
    qi = np.arange(ATT_BLK)[:, None]
    kj = np.arange(2 * ATT_BLK)[None, :]
    steps = qi + ATT_BLK - kj
    valid = (steps >= 0) & (steps <= ATT_BLK)
    bias = -(slopes * np.float32(dilation))[:, None, None] * steps.astype(np.float32)[None]
    bias = np.where(valid[None], bias * np.float32(LOG2E), np.float32(NEG_BIG)).astype(np.float32)
    return bias.reshape(ATT_HEADS_PER_GROUP * ATT_BLK, 2 * ATT_BLK)


def _attn_call(a, group):
    _, d = ATT_GROUPS[group]
    L = SEQ // d
    n_blocks = L // ATT_BLK
    subseqs = d if n_blocks == 1 else min(d, 4)

    def spec(w):
        return pl.BlockSpec((1, subseqs, L, w), lambda b, r: (b, r, 0, 0))

    return pl.pallas_call(
        functools.partial(_attn_body, subseqs=subseqs, n_blocks=n_blocks),
        grid=(BATCH, d // subseqs),
        in_specs=[spec(GRP_W), _const_spec((ATT_HEADS_PER_GROUP * ATT_BLK, 2 * ATT_BLK))],
        out_specs=[spec(ATT_OUT_WIDTH), spec(128)],
        out_shape=[jax.ShapeDtypeStruct((BATCH, d, L, ATT_OUT_WIDTH), BF),
                   jax.ShapeDtypeStruct((BATCH, d, L, 128), F32)],
        compiler_params=_params("parallel", "parallel"),
        name="attn%d" % d,
    )(a, jnp.asarray(_attn_bias(group)))


GLA_PAIR = 2 * GLA_CHUNK


def _gla_cum_decay(gd_ref, up_ref, gb_ref, tril2_ref):
    x = _dot(gd_ref[0], up_ref[...]) + gb_ref[...]
    log_a = (jnp.minimum(x, 0.0) - jnp.log(1.0 + jnp.exp(-jnp.abs(x)))) * (1.0 / GLA_TAU)
    la_hi, la_lo = _split_bf16(log_a)
    pieces = []
    for p in range(GLA_BLOCK // GLA_PAIR):
        rows = slice(p * GLA_PAIR, (p + 1) * GLA_PAIR)
        pieces.append(_dot(tril2_ref[...], jnp.concatenate([la_hi[rows], la_lo[rows]], axis=0)))
    return pieces


def _gla_pair(p, b, states, gq_ref, gkt_ref, gv_ref, o_ref):
    C, P = GLA_CHUNK, GLA_PAIR
    rows = slice(p * P, (p + 1) * P)
    q_dec =(gq_ref[0, rows, :].astype(F32) * (GLA_DK ** -0.5) * jnp.exp(b)).astype(BF)
    b_t = b.T
    k_dec_t_bf = (gkt_ref[0, :, rows].astype(F32) * jnp.exp(-b_t)).astype(BF)
    decay_t = jnp.exp(b_t)
    lane = lax.broadcasted_iota(jnp.int32, (GLA_KEY_DIM, P), 1)
    zero = jnp.zeros_like(k_dec_t_bf)
    k_dec_t_chunk = [jnp.where(lane < C, k_dec_t_bf, zero), jnp.where(lane >= C, k_dec_t_bf, zero)]
    ri = lax.broadcasted_iota(jnp.int32, (P, P), 0)
    ci = lax.broadcasted_iota(jnp.int32, (P, P), 1)
    causal = (ri >= ci) & ((ri >= C) == (ci >= C))
    new_states = []
    for h in range(GLA_HEADS):
        kc = slice(h * GLA_DK, (h + 1) * GLA_DK)
        vc = slice(h * GLA_DV, (h + 1) * GLA_DV)
        v = gv_ref[0, rows, vc]
        attn = jnp.where(causal, _dot(q_dec[:, kc], k_dec_t_bf[kc, :]), 0.0).astype(BF)
        stacked = _dot(jnp.concatenate([k_dec_t_chunk[0][kc, :], k_dec_t_chunk[1][kc, :], attn],
                                       axis=0), v)
        st = states[h]
        for cc in range(2):
            crow = slice(cc * C, (cc + 1) * C)
            o = stacked[2 * GLA_DK + cc * C:2 * GLA_DK + (cc + 1) * C] + _dot(q_dec[crow, kc],
                                                                              st.astype(BF))
            o_ref[0, p * P + cc * C:p * P + (cc + 1) * C, vc] = o.astype(o_ref.dtype)
            last = (cc + 1) * C - 1
            st = decay_t[kc, last:last + 1] * (st + stacked[cc * GLA_DK:(cc + 1) * GLA_DK])
        new_states.append(st)
    return new_states


def _gla_body(gq_ref, gkt_ref, gv_ref, gd_ref, up_ref, gb_ref, tril2_ref, o_ref, state_ref):
    @pl.when(pl.program_id(1) == 0)
    def _():
        state_ref[...] = jnp.zeros_like(state_ref)

    bs = _gla_cum_decay(gd_ref, up_ref, gb_ref, tril2_ref)
    states = [state_ref[h] for h in range(GLA_HEADS)]
    for p in range(GLA_BLOCK // GLA_PAIR):
        states = _gla_pair(p, bs[p], states, gq_ref, gkt_ref, gv_ref, o_ref)
    for h in range(GLA_HEADS):
        state_ref[h] = states[h]


def _pair_tril2():
    i = np.arange(GLA_PAIR)
    t = ((i[:, None] // GLA_CHUNK) == (i[None, :] // GLA_CHUNK)) & (i[:, None] >= i[None, :])
    return np.concatenate([t, t], axis=1).astype(np.float32)


def _gla_call(gq, gkt, gv, gd, up, gbias):
    def row(w):
        return pl.BlockSpec((1, GLA_BLOCK, w), lambda b, i: (b, i, 0))

    consts = (up, gbias, jnp.asarray(_pair_tril2(), BF))
    return pl.pallas_call(
        _gla_body,
        grid=(BATCH, SEQ // GLA_BLOCK),
        in_specs=[row(GLA_KEY_DIM),
                  pl.BlockSpec((1, GLA_KEY_DIM, GLA_BLOCK), lambda b, i: (b, 0, i)),
                  row(GLA_VAL_DIM), row(GD_PAD)]
                 + [_const_spec(a.shape) for a in consts],
        out_specs=row(GLA_VAL_DIM),
        out_shape=jax.ShapeDtypeStruct((BATCH, SEQ, GLA_VAL_DIM), BF),
        scratch_shapes=[pltpu.VMEM((GLA_HEADS, GLA_DK, GLA_DV), F32)],
        compiler_params=_params("parallel", "arbitrary"),
        name="gla",
    )(gq, gkt, gv, gd, *consts)


def _undilate(src_ref, scr_ref, d):
    if d == 1:
        return src_ref[0, 0].astype(F32)
    planes = scr_ref.shape[0]
    for r in range(d):
        x = src_ref[0, r].astype(F32)
        for j in range(planes):
            scr_ref[j, pl.ds(r, ROW_TILE // d, stride=d), :] = x[:, j * 128:(j + 1) * 128]
    return jnp.concatenate([scr_ref[j] for j in range(planes)], axis=-1)


def _merge_body(h_ref, mod_ref, o0_ref, o1_ref, o2_ref, l0_ref, l1_ref, l2_ref,
                ogla_ref, gr_ref, gate_ref, modall_ref, hexp_ref, gain_ref, wba_ref, wbg_ref,
                wout_ref, g_ref, w1_hbm, w3_hbm, w2_hbm, out_ref,
                os1_ref, os2_ref, ls1_ref, ls2_ref,
                w1_ref, w3_ref, w2_ref, s1_ref, s3_ref, *dma_scratch, layer):
    @pl.when(_first_step())
    def _():
        _load_ffn_weights(layer, 6, modall_ref, w1_hbm, w3_hbm, w2_hbm,
                          w1_ref, w3_ref, w2_ref, s1_ref, s3_ref, *dma_scratch)

    dils = [d for _, d in ATT_GROUPS]
    lses = [_undilate(r, s, d) for r, s, d in
            zip((l0_ref, l1_ref, l2_ref), (None, ls1_ref, ls2_ref), dils)]
    outs = [_undilate(r, s, d) for r, s, d in
            zip((o0_ref, o1_ref, o2_ref), (None, os1_ref, os2_ref), dils)]
    mx = jnp.maximum(jnp.maximum(lses[0], lses[1]), lses[2])
    es = [jnp.exp(l - mx) for l in lses]
    den = es[0] + es[1] + es[2]
    o_att = None
    for e, o in zip(es, outs):
        t = _dot_split(e / den, hexp_ref) * o
        o_att = t if o_att is None else o_att + t
    att = _dot(o_att.astype(BF), wba_ref[...])
    gated = []
    for h in range(GLA_HEADS):
        vc = slice(h * GLA_DV, (h + 1) * GLA_DV)
        o = ogla_ref[0, :, vc].astype(F32)
        ms = jnp.mean(o * o, axis=-1, keepdims=True)
        y = o * lax.rsqrt(ms + EPS) * gain_ref[:, vc]
        gated.append(y.astype(BF) * _silu(gr_ref[0, :, vc]))
    gla = _dot(jnp.concatenate(gated, axis=1), wbg_ref[...])
    merged = (_sigmoid(gate_ref[0, :, :D_MODEL]) * att.astype(BF)
              + _sigmoid(gate_ref[0, :, D_MODEL:]) * gla.astype(BF))
    m = mod_ref[0]
    h = h_ref[0] + (1.0 + m[5:6]) * _dot(merged, wout_ref[...])
    b = pl.program_id(0)
    out_ref[0] = _ffn_tiles([h], m, 6, g_ref, s1_ref[pl.ds(b, 1), :], s3_ref[pl.ds(b, 1), :],
                            w1_ref, w3_ref, w2_ref)[0]


def _merge_ffn_call(h, mod, os_, ls_, ogla, gr, gates, hexp, gain, wba, wbg, wout, g, w1, w3, w2,
                    layer):
    def row(w):
        return pl.BlockSpec((1, ROW_TILE, w), lambda b, i: (b, i, 0))

    consts = (mod, hexp, gain, wba, wbg, wout, g)
    hbm = pl.BlockSpec(memory_space=pl.ANY)
    dils = [d for _, d in ATT_GROUPS]
    return pl.pallas_call(
        functools.partial(_merge_body, layer=layer),
        grid=(BATCH, SEQ // ROW_TILE),
        in_specs=[row(D_MODEL), pl.BlockSpec((1, N_MOD, D_MODEL), lambda b, i: (b, 0, 0))]
                 + [_dilated_spec(d, ATT_OUT_WIDTH) for d in dils]
                 + [_dilated_spec(d, 128) for d in dils]
                 + [row(GLA_VAL_DIM), row(GLA_VAL_DIM), row(2 * D_MODEL)]
                 + [_const_spec(a.shape) for a in consts] + [hbm, hbm, hbm],
        out_specs=row(D_MODEL),
        out_shape=jax.ShapeDtypeStruct((BATCH, SEQ, D_MODEL), F32),
        scratch_shapes=[pltpu.VMEM((ATT_OUT_WIDTH // 128, ROW_TILE, 128), F32)] * 2
                       + [pltpu.VMEM((1, ROW_TILE, 128), F32)] * 2 + FFN_WEIGHT_SCRATCH,
        compiler_params=_params("arbitrary", "arbitrary"),
        name="merge_ffn",
    )(h, mod, *os_, *ls_, ogla, gr, gates, *consts, w1, w3, w2)


def _head_matrix(n_lanes_in, width):
    j = np.arange(n_lanes_in)[:, None]
    c = np.arange(width)[None, :]
    return (c // ATT_HEAD_DIM == j).astype(np.float32)


def _layer(l, h, mod, g_ffn1, f1w1, f1w3, f1w2, g_mix, w_in, q_norm_g, k_norm_g, gate_up, gate_bias,
           out_norm_g, w_branch_att, w_branch_gla, w_out, g_ffn2, f2w1, f2w3, f2w2):
    bf = lambda w: w.astype(BF)
    row = lambda v: v.reshape(1, -1)

    h = _ffn_call(h, mod, row(g_ffn1), f1w1, f1w3, f1w2, 0, l)

    wgate = _gate_weights_call(w_in, l)
    head_qk = _head_matrix(128, QK_W)
    hsum_qk = jnp.asarray(head_qk.T, BF)
    hexp_qk = jnp.asarray(np.concatenate([head_qk, head_qk]), BF)
    qkg = jnp.concatenate([jnp.tile(q_norm_g * (ATT_HEAD_DIM ** -0.5 * LOG2E), ATT_HEADS_PER_GROUP),
                           jnp.tile(k_norm_g, ATT_HEADS_PER_GROUP)])
    a0, a1, a2, pgq, pgkt, pgv, pgr, pgd, pgate = _proj_call(
        h, mod, row(g_mix), w_in, wgate, hsum_qk, hexp_qk, row(qkg), l)

    outs = [_attn_call(a, gi) for gi, a in enumerate((a0, a1, a2))]
    up = jnp.pad(bf(gate_up), ((0, GD_PAD - GLA_GATE_RANK), (0, 0)))
    ogla = _gla_call(pgq, pgkt, pgv, pgd, up, row(gate_bias))

    head_o = _head_matrix(128, ATT_OUT_WIDTH)
    hexp_o = jnp.asarray(np.concatenate([head_o, head_o]), BF)
    return _merge_ffn_call(h, mod, [o for o, _ in outs], [l for _, l in outs], ogla, pgr, pgate,
                           hexp_o, row(jnp.tile(out_norm_g, GLA_HEADS)), bf(w_branch_att),
                           bf(w_branch_gla), bf(w_out), row(g_ffn2), f2w1, f2w3, f2w2, l)


def kernel(x, c, w_mod, b_mod, g_ffn1, ffn1_w1, ffn1_w3, ffn1_w2, g_mix, w_in, q_norm_g, k_norm_g,
           gla_gate_up, gla_gate_bias, gla_out_norm_g, w_branch_att, w_branch_gla, w_out,
           g_ffn2, ffn2_w1, ffn2_w3, ffn2_w2):
    h = x
    for l in range(w_mod.shape[0]):
        mod = _mod_call(c, w_mod, b_mod[l], l).reshape(BATCH, N_MOD, D_MODEL)
        h = _layer(l, h, mod, g_ffn1[l], ffn1_w1, ffn1_w3, ffn1_w2, g_mix[l], w_in,
                   q_norm_g[l], k_norm_g[l], gla_gate_up[l], gla_gate_bias[l], gla_out_norm_g[l],
                   w_branch_att[l], w_branch_gla[l], w_out[l], g_ffn2[l], ffn2_w1, ffn2_w3,
                   ffn2_w2)
    return h
```

```python
import functools

import numpy as np
import jax
import jax.numpy as jnp
from jax import lax
from jax.experimental import pallas as pl
from jax.experimental.pallas import tpu as pltpu

D_MODEL = 1024
BATCH = 16
SEQ = 2048
N_MOD = 9
D_FF = 2816
EPS = 1e-6
ATT_GROUPS = ((128, 1), (512, 4), (2048, 16))
ATT_HEADS_PER_GROUP = 4
ATT_HEADS = 12
ATT_HEAD_DIM = 64
ATT_WIDTH = 768
ATT_OUT_WIDTH = 256
ALIBI_MAX = 8.0
GLA_HEADS = 4
GLA_KEY_DIM = 512
GLA_VAL_DIM = 1024
GLA_DK = 128
GLA_DV = 256
GLA_GATE_RANK = 16
GLA_TAU = 16.0
GLA_CHUNK = 64
ATT_BLK = 128
NEG_BIG = -1e30
LOG2E = float(np.log2(np.e))
LN2 = float(np.log(2.0))

BF = jnp.bfloat16
F32 = jnp.float32

VMEM_LIMIT_BYTES = 56 * 1024 * 1024

ROW_TILE = 512
FFN_TILE = 1024
FFN_SUBTILES = 2
MXU_TILE = 256
FF_CHUNK_TILES = 4
GLA_BLOCK = 512


def _dot(a, b):
    return jnp.dot(a, b, preferred_element_type=F32)


def _dot_nt(a, b):
    return lax.dot_general(a, b, (((1,), (1,)), ((), ())), preferred_element_type=F32)


def _split_bf16(x):
    hi = x.astype(BF)
    lo = (x - hi.astype(F32)).astype(BF)
    return hi, lo


def _dot_split(x, stacked_ref):
    hi, lo = _split_bf16(x)
    return _dot(jnp.concatenate([hi, lo], axis=1), stacked_ref[...])


def _sigmoid(x):
    return 0.5 * jnp.tanh(0.5 * x) + 0.5


def _silu(x):
    return x * _sigmoid(x)


def _const_spec(shape):
    nd = len(shape)
    return pl.BlockSpec(shape, lambda *_: (0,) * nd, pipeline_mode=pl.Buffered(1))


def _dilated_spec(d, width):
    return pl.BlockSpec((1, d, ROW_TILE // d, width), lambda b, i: (b, 0, i, 0))


def _params(*sem):
    return pltpu.CompilerParams(dimension_semantics=sem, vmem_limit_bytes=VMEM_LIMIT_BYTES)


def _mod_body(c_ref, w_ref, b_ref, o_ref):
    c = c_ref[...]
    o_ref[...] = _dot(_silu(c).astype(BF), w_ref[...].astype(BF)) + b_ref[...]


def _mod_call(c, w_mod, b_mod, layer):
    n = w_mod.shape[2]
    bn = D_MODEL
    return pl.pallas_call(
        _mod_body,
        grid=(n // bn,),
        in_specs=[pl.BlockSpec((BATCH, D_MODEL), lambda j: (0, 0)),
                  pl.BlockSpec((None, D_MODEL, bn), lambda j: (layer, 0, j)),
                  pl.BlockSpec((1, bn), lambda j: (0, j))],
        out_specs=pl.BlockSpec((BATCH, bn), lambda j: (0, j)),
        out_shape=jax.ShapeDtypeStruct((BATCH, n), F32),
        compiler_params=_params("arbitrary"),
        name="mod",
    )(c, w_mod, b_mod.reshape(1, n))


def _row_scale(x):
    return lax.rsqrt(jnp.mean(x * x, axis=-1, keepdims=True) + EPS)


UP_PIECE_ROWS = 128
DOWN_PIECE_ROWS = 256

FFN_WEIGHT_SCRATCH = [
    pltpu.VMEM((D_MODEL, D_FF), BF), pltpu.VMEM((D_MODEL, D_FF), BF), pltpu.VMEM((D_FF, D_MODEL), BF),
    pltpu.VMEM((BATCH, D_FF), F32), pltpu.VMEM((BATCH, D_FF), F32),
    pltpu.VMEM((2, UP_PIECE_ROWS, D_FF), F32), pltpu.VMEM((2, DOWN_PIECE_ROWS, D_MODEL), F32),
    pltpu.SemaphoreType.DMA((2,)), pltpu.SemaphoreType.DMA((2,))]


def _stream_cast(pieces, stage_ref, sem_ref, store, cast=True):
    def copy(i):
        return pltpu.make_async_copy(pieces[i], stage_ref.at[i % 2], sem_ref.at[i % 2])

    copy(0).start()
    for i in range(len(pieces)):
        if i + 1 < len(pieces):
            copy(i + 1).start()
        copy(i).wait()
        v = stage_ref[i % 2]
        store(i, v.astype(BF) if cast else v)


def _load_ffn_weights(layer, mod_row, modall_ref, w1_hbm, w3_hbm, w2_hbm,
                      w1_ref, w3_ref, w2_ref, s1_ref, s3_ref, stage_up, stage_dn, sem_up, sem_dn):
    up_rows = [pl.ds(r, UP_PIECE_ROWS) for r in range(0, D_MODEL, UP_PIECE_ROWS)]
    dn_rows = [pl.ds(r, DOWN_PIECE_ROWS) for r in range(0, D_FF, DOWN_PIECE_ROWS)]

    def store_up(i, v):
        dst = w1_ref if i < len(up_rows) else w3_ref
        dst[up_rows[i % len(up_rows)], :] = v

    def store_dn(i, v):
        w2_ref[dn_rows[i], :] = v

    _stream_cast([w.at[layer, r, :] for w in (w1_hbm, w3_hbm) for r in up_rows],
                 stage_up, sem_up, store_up)
    _stream_cast([w2_hbm.at[layer, r, :] for r in dn_rows], stage_dn, sem_dn, store_dn)
    shift = modall_ref[:, mod_row, :].astype(BF)
    s1_ref[...] = _dot(shift, w1_ref[...])
    s3_ref[...] = _dot(shift, w3_ref[...])


def _first_step():
    return jnp.logical_and(pl.program_id(0) == 0, pl.program_id(1) == 0)


def _ffn_tiles(xs, m, mod_row, g_ref, s1, s3, w1_ref, w3_ref, w2_ref):
    scale, gate = m[mod_row + 1:mod_row + 2], m[mod_row + 2:mod_row + 3]
    fc = FF_CHUNK_TILES * MXU_TILE
    gs = g_ref[...] * (1.0 + scale)
    us = [(x * gs).astype(BF) for x in xs]
    rs = [_row_scale(x) for x in xs]
    accs = [None] * len(xs)
    for lo in range(0, D_FF, fc):
        cols = slice(lo, min(lo + fc, D_FF))
        for s, u in enumerate(us):
            h1 = rs[s] * _dot(u, w1_ref[:, cols]) + s1[:, cols]
            h3 = rs[s] * _dot(u, w3_ref[:, cols]) + s3[:, cols]
            p = _dot((_silu(h1) * h3).astype(BF), w2_ref[cols, :])
            accs[s] = p if accs[s] is None else accs[s] + p
    return [x + (0.5 * (1.0 + gate)) * acc for x, acc in zip(xs, accs)]


def _ffn_body(x_ref, mod_ref, modall_ref, g_ref, w1_hbm, w3_hbm, w2_hbm, o_ref,
              w1_ref, w3_ref, w2_ref, s1_ref, s3_ref, *dma_scratch, mod_row, layer):
    @pl.when(_first_step())
    def _():
        _load_ffn_weights(layer, mod_row, modall_ref, w1_hbm, w3_hbm, w2_hbm,
                          w1_ref, w3_ref, w2_ref, s1_ref, s3_ref, *dma_scratch)

    b = pl.program_id(0)
    sub = FFN_TILE // FFN_SUBTILES
    rows = [slice(s * sub, (s + 1) * sub) for s in range(FFN_SUBTILES)]
    outs = _ffn_tiles([x_ref[0, r, :] for r in rows], mod_ref[0], mod_row, g_ref,
                      s1_ref[pl.ds(b, 1), :], s3_ref[pl.ds(b, 1), :], w1_ref, w3_ref, w2_ref)
    for r, o in zip(rows, outs):
        o_ref[0, r, :] = o


def _ffn_call(x, mod, g, w1, w3, w2, mod_row, layer):
    row = pl.BlockSpec((1, FFN_TILE, D_MODEL), lambda b, i: (b, i, 0))
    hbm = pl.BlockSpec(memory_space=pl.ANY)
    return pl.pallas_call(
        functools.partial(_ffn_body, mod_row=mod_row, layer=layer),
        grid=(BATCH, SEQ // FFN_TILE),
        in_specs=[row,
                  pl.BlockSpec((1, N_MOD, D_MODEL), lambda b, i: (b, 0, 0)),
                  _const_spec((BATCH, N_MOD, D_MODEL)),
                  _const_spec((1, D_MODEL)),
                  hbm, hbm, hbm],
        out_specs=row,
        out_shape=jax.ShapeDtypeStruct((BATCH, SEQ, D_MODEL), F32),
        scratch_shapes=FFN_WEIGHT_SCRATCH,
        compiler_params=_params("arbitrary", "arbitrary"),
        name="ffn%d" % mod_row,
    )(x, mod, mod, g, w1, w3, w2)


QK_W = 2 * ATT_OUT_WIDTH
GRP_W = 3 * ATT_OUT_WIDTH
GD_PAD = 128


IN_GQ = 3 * ATT_WIDTH
IN_GK = IN_GQ + GLA_KEY_DIM
IN_GV = IN_GK + GLA_KEY_DIM
IN_GR = IN_GV + GLA_VAL_DIM
IN_GD = IN_GR + GLA_VAL_DIM
IN_GATES = IN_GD + GLA_GATE_RANK
IN_PIECE_ROWS = 256

PROJ_WEIGHT_SCRATCH = [pltpu.VMEM((n, D_MODEL), BF) for n in
                       (3 * GRP_W, GLA_KEY_DIM, GLA_KEY_DIM, GLA_VAL_DIM, GLA_VAL_DIM, GD_PAD,
                        2 * D_MODEL)]
PROJ_SHIFT_SCRATCH = [pltpu.VMEM((BATCH, n), F32) for n in
                      (3 * GRP_W, GLA_KEY_DIM, GLA_VAL_DIM, GLA_VAL_DIM, GD_PAD, 2 * D_MODEL)]
PROJ_DMA_SCRATCH = [
    pltpu.VMEM((2, IN_PIECE_ROWS, D_MODEL), F32), pltpu.VMEM((GLA_GATE_RANK, D_MODEL), F32),
    pltpu.SemaphoreType.DMA((2,)), pltpu.SemaphoreType.DMA(())]


def _load_proj_weights(layer, modall_ref, wint_hbm, weights, shifts,
                       stage_ref, gd_stage_ref, sem_ref, gd_sem):
    watt_ref, wgq_ref, wgk_ref, wgv_ref, wgr_ref, wgd_ref, wgate_ref = weights
    P, W = IN_PIECE_ROWS, ATT_OUT_WIDTH
    moves = [(t * ATT_WIDTH + gi * W, watt_ref, gi * GRP_W + t * W)
             for gi in range(len(ATT_GROUPS)) for t in range(3)]
    for src, dst in ((IN_GQ, wgq_ref), (IN_GK, wgk_ref), (IN_GV, wgv_ref), (IN_GR, wgr_ref),
                     (IN_GATES, wgate_ref)):
        moves += [(src + r, dst, r) for r in range(0, dst.shape[0], P)]

    def store(i, v):
        _, dst, r = moves[i]
        dst[pl.ds(r, P), :] = v

    _stream_cast([wint_hbm.at[layer, pl.ds(src, P), :] for src, _, _ in moves],
                 stage_ref, sem_ref, store)

    gd_copy = pltpu.make_async_copy(wint_hbm.at[layer, pl.ds(IN_GD, GLA_GATE_RANK), :],
                                    gd_stage_ref, gd_sem)
    gd_copy.start()
    gd_copy.wait()
    wgd_ref[...] = jnp.zeros_like(wgd_ref)
    wgd_ref[:GLA_GATE_RANK, :] = gd_stage_ref[...].astype(BF)

    shift = modall_ref[:, 3, :].astype(BF)
    for s_ref, w_ref in zip(shifts, (watt_ref, wgq_ref, wgv_ref, wgr_ref, wgd_ref, wgate_ref)):
        s_ref[...] = _dot_nt(shift, w_ref[...])


def _proj_body(h_ref, mod_ref, modall_ref, g_ref, hsum_ref, hexp_ref, qkg_ref, wint_hbm,
               a0_ref, a1_ref, a2_ref, gq_ref, gkt_ref, gv_ref, gr_ref, gd_ref, gate_ref, perm_ref,
               watt_ref, wgq_ref, wgk_ref, wgv_ref, wgr_ref, wgd_ref, wgate_ref,
               satt_ref, sgq_ref, sgv_ref, sgr_ref, sgd_ref, sgate_ref, *dma_scratch, layer):
    @pl.when(_first_step())
    def _():
        _load_proj_weights(layer, modall_ref, wint_hbm,
                           (watt_ref, wgq_ref, wgk_ref, wgv_ref, wgr_ref, wgd_ref, wgate_ref),
                           (satt_ref, sgq_ref, sgv_ref, sgr_ref, sgd_ref, sgate_ref), *dma_scratch)

    b = pl.ds(pl.program_id(0), 1)
    m = mod_ref[0]
    x = h_ref[0]
    xg = x * (g_ref[...] * (1.0 + m[4:5]))
    u = xg.astype(BF)
    rs = _row_scale(x)

    def proj(wt, s):
        return rs * _dot_nt(u, wt) + s

    for gi, a_ref in enumerate((a0_ref, a1_ref, a2_ref)):
        d = ATT_GROUPS[gi][1]
        grp = slice(gi * GRP_W, (gi + 1) * GRP_W)
        y = proj(watt_ref[grp, :], satt_ref[b, grp])
        qk = y[:, :QK_W]
        ss = _dot((qk * qk).astype(BF), hsum_ref[...])
        rx = _dot_split(lax.rsqrt(ss * (1.0 / ATT_HEAD_DIM) + EPS), hexp_ref)
        qkn = qk * rx * qkg_ref[...]
        if d == 1:
            a_ref[0, 0, :, :QK_W] = qkn.astype(BF)
            a_ref[0, 0, :, QK_W:] = y[:, QK_W:].astype(BF)
        else:
            for j in range(GRP_W // 128):
                cols = slice(j * 128, (j + 1) * 128)
                perm_ref[j] = qkn[:, cols] if j < QK_W // 128 else y[:, cols]
            for r in range(d):
                for j in range(GRP_W // 128):
                    a_ref[0, r, :, j * 128:(j + 1) * 128] = (
                        perm_ref[j, pl.ds(r, ROW_TILE // d, stride=d), :].astype(BF))
    gq_ref[0] = proj(wgq_ref[...], sgq_ref[b, :]).astype(BF)
    gv_ref[0] = proj(wgv_ref[...], sgv_ref[b, :]).astype(BF)
    gr_ref[0] = proj(wgr_ref[...], sgr_ref[b, :]).astype(BF)
    gd_ref[0] = proj(wgd_ref[...], sgd_ref[b, :]).astype(BF)
    gate_ref[0] = proj(wgate_ref[...], sgate_ref[b, :]).astype(BF)
    u_full = (xg * rs + m[3:4]).astype(BF)
    gkt_ref[0] = _dot_nt(wgk_ref[...], u_full).astype(BF)


def _proj_call(h, mod, g, w_in_t, hsum, hexp, qkg, layer):
    def row(w):
        return pl.BlockSpec((1, ROW_TILE, w), lambda b, i: (b, i, 0))

    def out(w):
        return jax.ShapeDtypeStruct((BATCH, SEQ, w), BF)

    kt_spec = pl.BlockSpec((1, GLA_KEY_DIM, ROW_TILE), lambda b, i: (b, 0, i))
    kt_shape = jax.ShapeDtypeStruct((BATCH, GLA_KEY_DIM, SEQ), BF)
    widths = (GLA_VAL_DIM, GLA_VAL_DIM, GD_PAD, 2 * D_MODEL)
    consts = (mod, g, hsum, hexp, qkg)
    dils = [d for _, d in ATT_GROUPS]
    return pl.pallas_call(
        functools.partial(_proj_body, layer=layer),
        grid=(BATCH, SEQ // ROW_TILE),
        in_specs=[row(D_MODEL), pl.BlockSpec((1, N_MOD, D_MODEL), lambda b, i: (b, 0, 0))]
                 + [_const_spec(a.shape) for a in consts]
                 + [pl.BlockSpec(memory_space=pl.ANY)],
        out_specs=[_dilated_spec(d, GRP_W) for d in dils] + [row(GLA_KEY_DIM), kt_spec]
                  + [row(w) for w in widths],
        out_shape=[jax.ShapeDtypeStruct((BATCH, d, SEQ // d, GRP_W), BF) for d in dils]
                  + [out(GLA_KEY_DIM), kt_shape] + [out(w) for w in widths],
        scratch_shapes=[pltpu.VMEM((GRP_W // 128, ROW_TILE, 128), F32)]
                       + PROJ_WEIGHT_SCRATCH + PROJ_SHIFT_SCRATCH + PROJ_DMA_SCRATCH,
        compiler_params=_params("arbitrary", "arbitrary"),
        name="proj",
    )(h, mod, *consts, w_in_t)


ATT_GROUP_BLOCKS = 4


def _attn_blocks(a_ref, bias_ref, o_ref, l_ref, blocks):
    W, NH = ATT_OUT_WIDTH, ATT_HEADS_PER_GROUP
    lane = lax.broadcasted_iota(jnp.int32, (ATT_BLK, 128), 1)
    low_half = lane < ATT_HEAD_DIM
    zero_col = jnp.zeros((ATT_BLK, 128), BF)
    qs, ks, vs, biases, q0s = [], [], [], [], []
    for r, n, first in blocks:
        if first:
            q0, k0, nk = 0, 0, ATT_BLK
            biases.append(bias_ref[:, ATT_BLK:])
        else:
            q0, k0, nk = n * ATT_BLK, (n - 1) * ATT_BLK, 2 * ATT_BLK
            if not isinstance(n, int):
                q0, k0 = pl.multiple_of(q0, ATT_BLK), pl.multiple_of(k0, ATT_BLK)
            biases.append(bias_ref[...])
        q0s.append(q0)
        stacked = []
        for j in range(NH):
            col = a_ref[0, r, pl.ds(q0, ATT_BLK), (j // 2) * 128:(j // 2 + 1) * 128]
            col = jnp.where(low_half == (j % 2 == 0), col, zero_col)
            stacked.append(jnp.concatenate([col, zero_col] if j < 2 else [zero_col, col], axis=1))
        qs.append(jnp.concatenate(stacked, axis=0))
        ks.append(a_ref[0, r, pl.ds(k0, nk), W:2 * W])
        vs.append(a_ref[0, r, pl.ds(k0, nk), 2 * W:])
    ss = [_dot_nt(q4, k) + b for q4, k, b in zip(qs, ks, biases)]
    mxs = [jnp.max(s, axis=-1, keepdims=True) for s in ss]
    es = [jnp.exp2(s - mx) for s, mx in zip(ss, mxs)]
    dens = [jnp.sum(e, axis=-1, keepdims=True) for e in es]
    pvs = [_dot(e.astype(BF), v) for e, v in zip(es, vs)]
    for (r, _, _), q0, pv, mx, den in zip(blocks, q0s, pvs, mxs, dens):
        rden = 1.0 / den
        lse = mx * LN2 + jnp.log(den)
        l = jnp.zeros((ATT_BLK, 128), F32)
        heads = []
        for j in range(NH):
            rows = slice(j * ATT_BLK, (j + 1) * ATT_BLK)
            heads.append(pv[rows, (j // 2) * 128:(j // 2 + 1) * 128] * rden[rows])
            l = jnp.where(lane == j, lse[rows], l)
        o = jnp.concatenate([jnp.where(low_half, heads[0], heads[1]),
                             jnp.where(low_half, heads[2], heads[3])], axis=1)
        o_ref[0, r, pl.ds(q0, ATT_BLK), :] = o.astype(o_ref.dtype)
        l_ref[0, r, pl.ds(q0, ATT_BLK), :] = l


def _attn_body(a_ref, bias_ref, o_ref, l_ref, *, subseqs, n_blocks):
    G = ATT_GROUP_BLOCKS
    if n_blocks == 1:
        for r0 in range(0, subseqs, G):
            _attn_blocks(a_ref, bias_ref, o_ref, l_ref,
                         [(r, 0, True) for r in range(r0, min(r0 + G, subseqs))])
        return
    for r in range(subseqs):
        for n0 in range(0, n_blocks, G):
            _attn_blocks(a_ref, bias_ref, o_ref, l_ref,
                         [(r, n, n == 0) for n in range(n0, n0 + G)])


def _attn_bias(group):
    _, dilation = ATT_GROUPS[group]
    heads = np.arange(group * ATT_HEADS_PER_GROUP, (group + 1) * ATT_HEADS_PER_GROUP)
    slopes = (2.0 ** (-ALIBI_MAX * (heads + 1).astype(np.float32) / ATT_HEADS)).astype(np.float32)
    qi = np.arange(ATT_BLK)[:, None]
    kj = np.arange(2 * ATT_BLK)[None, :]
    steps = qi + ATT_BLK - kj
    valid = (steps >= 0) & (steps <= ATT_BLK)
    bias = -(slopes * np.float32(dilation))[:, None, None] * steps.astype(np.float32)[None]
    bias = np.where(valid[None], bias * np.float32(LOG2E), np.float32(NEG_BIG)).astype(np.float32)
    return bias.reshape(ATT_HEADS_PER_GROUP * ATT_BLK, 2 * ATT_BLK)


def _attn_call(a, group):
    _, d = ATT_GROUPS[group]
    L = SEQ // d
    n_blocks = L // ATT_BLK
    subseqs = d if n_blocks == 1 else min(d, 4)

    def spec(w):
        return pl.BlockSpec((1, subseqs, L, w), lambda b, r: (b, r, 0, 0))

    return pl.pallas_call(
        functools.partial(_attn_body, subseqs=subseqs, n_blocks=n_blocks),
        grid=(BATCH, d // subseqs),
        in_specs=[spec(GRP_W), _const_spec((ATT_HEADS_PER_GROUP * ATT_BLK, 2 * ATT_BLK))],
        out_specs=[spec(ATT_OUT_WIDTH), spec(128)],
        out_shape=[jax.ShapeDtypeStruct((BATCH, d, L, ATT_OUT_WIDTH), BF),
                   jax.ShapeDtypeStruct((BATCH, d, L, 128), F32)],
        compiler_params=_params("parallel", "parallel"),
        name="attn%d" % d,
    )(a, jnp.asarray(_attn_bias(group)))


GLA_PAIR = 2 * GLA_CHUNK


def _gla_cum_decay(gd_ref, up_ref, gb_ref, tril2_ref):
    x = _dot(gd_ref[0], up_ref[...]) + gb_ref[...]
    log_a = (jnp.minimum(x, 0.0) - jnp.log(1.0 + jnp.exp(-jnp.abs(x)))) * (1.0 / GLA_TAU)
    la_hi, la_lo = _split_bf16(log_a)
    pieces = []
    for p in range(GLA_BLOCK // GLA_PAIR):
        rows = slice(p * GLA_PAIR, (p + 1) * GLA_PAIR)
        pieces.append(_dot(tril2_ref[...], jnp.concatenate([la_hi[rows], la_lo[rows]], axis=0)))
    return pieces


def _gla_pair(p, b, states, gq_ref, gkt_ref, gv_ref, o_ref):
    C, P = GLA_CHUNK, GLA_PAIR
    rows = slice(p * P, (p + 1) * P)
    q_dec =(gq_ref[0, rows, :].astype(F32) * (GLA_DK ** -0.5) * jnp.exp(b)).astype(BF)
    b_t = b.T
    k_dec_t_bf = (gkt_ref[0, :, rows].astype(F32) * jnp.exp(-b_t)).astype(BF)
    decay_t = jnp.exp(b_t)
    lane = lax.broadcasted_iota(jnp.int32, (GLA_KEY_DIM, P), 1)
    zero = jnp.zeros_like(k_dec_t_bf)
    k_dec_t_chunk = [jnp.where(lane < C, k_dec_t_bf, zero), jnp.where(lane >= C, k_dec_t_bf, zero)]
    ri = lax.broadcasted_iota(jnp.int32, (P, P), 0)
    ci = lax.broadcasted_iota(jnp.int32, (P, P), 1)
    causal = (ri >= ci) & ((ri >= C) == (ci >= C))
    new_states = []
    for h in range(GLA_HEADS):
        kc = slice(h * GLA_DK, (h + 1) * GLA_DK)
        vc = slice(h * GLA_DV, (h + 1) * GLA_DV)
        v = gv_ref[0, rows, vc]
        attn = jnp.where(causal, _dot(q_dec[:, kc], k_dec_t_bf[kc, :]), 0.0).astype(BF)
        stacked = _dot(jnp.concatenate([k_dec_t_chunk[0][kc, :], k_dec_t_chunk[1][kc, :], attn],
                                       axis=0), v)
        st = states[h]
        for cc in range(2):
            crow = slice(cc * C, (cc + 1) * C)
            o = stacked[2 * GLA_DK + cc * C:2 * GLA_DK + (cc + 1) * C] + _dot(q_dec[crow, kc],
                                                                              st.astype(BF))
            o_ref[0, p * P + cc * C:p * P + (cc + 1) * C, vc] = o.astype(o_ref.dtype)
            last = (cc + 1) * C - 1
            st = decay_t[kc, last:last + 1] * (st + stacked[cc * GLA_DK:(cc + 1) * GLA_DK])
        new_states.append(st)
    return new_states


def _gla_body(gq_ref, gkt_ref, gv_ref, gd_ref, up_ref, gb_ref, tril2_ref, o_ref, state_ref):
    @pl.when(pl.program_id(1) == 0)
    def _():
        state_ref[...] = jnp.zeros_like(state_ref)

    bs = _gla_cum_decay(gd_ref, up_ref, gb_ref, tril2_ref)
    states = [state_ref[h] for h in range(GLA_HEADS)]
    for p in range(GLA_BLOCK // GLA_PAIR):
        states = _gla_pair(p, bs[p], states, gq_ref, gkt_ref, gv_ref, o_ref)
    for h in range(GLA_HEADS):
        state_ref[h] = states[h]


def _pair_tril2():
    i = np.arange(GLA_PAIR)
    t = ((i[:, None] // GLA_CHUNK) == (i[None, :] // GLA_CHUNK)) & (i[:, None] >= i[None, :])
    return np.concatenate([t, t], axis=1).astype(np.float32)


def _gla_call(gq, gkt, gv, gd, up, gbias):
    def row(w):
        return pl.BlockSpec((1, GLA_BLOCK, w), lambda b, i: (b, i, 0))

    consts = (up, gbias, jnp.asarray(_pair_tril2(), BF))
    return pl.pallas_call(
        _gla_body,
        grid=(BATCH, SEQ // GLA_BLOCK),
        in_specs=[row(GLA_KEY_DIM),
                  pl.BlockSpec((1, GLA_KEY_DIM, GLA_BLOCK), lambda b, i: (b, 0, i)),
                  row(GLA_VAL_DIM), row(GD_PAD)]
                 + [_const_spec(a.shape) for a in consts],
        out_specs=row(GLA_VAL_DIM),
        out_shape=jax.ShapeDtypeStruct((BATCH, SEQ, GLA_VAL_DIM), BF),
        scratch_shapes=[pltpu.VMEM((GLA_HEADS, GLA_DK, GLA_DV), F32)],
        compiler_params=_params("parallel", "arbitrary"),
        name="gla",
    )(gq, gkt, gv, gd, *consts)


def _undilate(src_ref, scr_ref, d):
    if d == 1:
        return src_ref[0, 0].astype(F32)
    planes = scr_ref.shape[0]
    for r in range(d):
        x = src_ref[0, r].astype(F32)
        for j in range(planes):
            scr_ref[j, pl.ds(r, ROW_TILE // d, stride=d), :] = x[:, j * 128:(j + 1) * 128]
    return jnp.concatenate([scr_ref[j] for j in range(planes)], axis=-1)


def _merge_body(h_ref, mod_ref, o0_ref, o1_ref, o2_ref, l0_ref, l1_ref, l2_ref,
                ogla_ref, gr_ref, gate_ref, modall_ref, hexp_ref, gain_ref, wba_ref, wbg_ref,
                wout_ref, g_ref, w1_hbm, w3_hbm, w2_hbm, out_ref,
                os1_ref, os2_ref, ls1_ref, ls2_ref,
                w1_ref, w3_ref, w2_ref, s1_ref, s3_ref, *dma_scratch, layer):
    @pl.when(_first_step())
    def _():
        _load_ffn_weights(layer, 6, modall_ref, w1_hbm, w3_hbm, w2_hbm,
                          w1_ref, w3_ref, w2_ref, s1_ref, s3_ref, *dma_scratch)

    dils = [d for _, d in ATT_GROUPS]
    lses = [_undilate(r, s, d) for r, s, d in
            zip((l0_ref, l1_ref, l2_ref), (None, ls1_ref, ls2_ref), dils)]
    outs = [_undilate(r, s, d) for r, s, d in
            zip((o0_ref, o1_ref, o2_ref), (None, os1_ref, os2_ref), dils)]
    mx = jnp.maximum(jnp.maximum(lses[0], lses[1]), lses[2])
    es = [jnp.exp(l - mx) for l in lses]
    den = es[0] + es[1] + es[2]
    o_att = None
    for e, o in zip(es, outs):
        t = _dot_split(e / den, hexp_ref) * o
        o_att = t if o_att is None else o_att + t
    att = _dot(o_att.astype(BF), wba_ref[...])
    gated = []
    for h in range(GLA_HEADS):
        vc = slice(h * GLA_DV, (h + 1) * GLA_DV)
        o = ogla_ref[0, :, vc].astype(F32)
        ms = jnp.mean(o * o, axis=-1, keepdims=True)
        y = o * lax.rsqrt(ms + EPS) * gain_ref[:, vc]
        gated.append(y.astype(BF) * _silu(gr_ref[0, :, vc]))
    gla = _dot(jnp.concatenate(gated, axis=1), wbg_ref[...])
    merged = (_sigmoid(gate_ref[0, :, :D_MODEL]) * att.astype(BF)
              + _sigmoid(gate_ref[0, :, D_MODEL:]) * gla.astype(BF))
    m = mod_ref[0]
    h = h_ref[0] + (1.0 + m[5:6]) * _dot(merged, wout_ref[...])
    b = pl.program_id(0)
    out_ref[0] = _ffn_tiles([h], m, 6, g_ref, s1_ref[pl.ds(b, 1), :], s3_ref[pl.ds(b, 1), :],
                            w1_ref, w3_ref, w2_ref)[0]


def _merge_ffn_call(h, mod, os_, ls_, ogla, gr, gates, hexp, gain, wba, wbg, wout, g, w1, w3, w2,
                    layer):
    def row(w):
        return pl.BlockSpec((1, ROW_TILE, w), lambda b, i: (b, i, 0))

    consts = (mod, hexp, gain, wba, wbg, wout, g)
    hbm = pl.BlockSpec(memory_space=pl.ANY)
    dils = [d for _, d in ATT_GROUPS]
    return pl.pallas_call(
        functools.partial(_merge_body, layer=layer),
        grid=(BATCH, SEQ // ROW_TILE),
        in_specs=[row(D_MODEL), pl.BlockSpec((1, N_MOD, D_MODEL), lambda b, i: (b, 0, 0))]
                 + [_dilated_spec(d, ATT_OUT_WIDTH) for d in dils]
                 + [_dilated_spec(d, 128) for d in dils]
                 + [row(GLA_VAL_DIM), row(GLA_VAL_DIM), row(2 * D_MODEL)]
                 + [_const_spec(a.shape) for a in consts] + [hbm, hbm, hbm],
        out_specs=row(D_MODEL),
        out_shape=jax.ShapeDtypeStruct((BATCH, SEQ, D_MODEL), F32),
        scratch_shapes=[pltpu.VMEM((ATT_OUT_WIDTH // 128, ROW_TILE, 128), F32)] * 2
                       + [pltpu.VMEM((1, ROW_TILE, 128), F32)] * 2 + FFN_WEIGHT_SCRATCH,
        compiler_params=_params("arbitrary", "arbitrary"),
        name="merge_ffn",
    )(h, mod, *os_, *ls_, ogla, gr, gates, *consts, w1, w3, w2)


def _head_matrix(n_lanes_in, width):
    j = np.arange(n_lanes_in)[:, None]
    c = np.arange(width)[None, :]
    return (c // ATT_HEAD_DIM == j).astype(np.float32)


def _layer(l, h, mod, g_ffn1, f1w1, f1w3, f1w2, g_mix, w_in, q_norm_g, k_norm_g, gate_up, gate_bias,
           out_norm_g, w_branch_att, w_branch_gla, w_out, g_ffn2, f2w1, f2w3, f2w2):
    bf = lambda w: w.astype(BF)
    row = lambda v: v.reshape(1, -1)

    h = _ffn_call(h, mod, row(g_ffn1), f1w1, f1w3, f1w2, 0, l)

    w_in_t = jnp.swapaxes(w_in, 1, 2)
    head_qk = _head_matrix(128, QK_W)
    hsum_qk = jnp.asarray(head_qk.T, BF)
    hexp_qk = jnp.asarray(np.concatenate([head_qk, head_qk]), BF)
    qkg = jnp.concatenate([jnp.tile(q_norm_g * (ATT_HEAD_DIM ** -0.5 * LOG2E), ATT_HEADS_PER_GROUP),
                           jnp.tile(k_norm_g, ATT_HEADS_PER_GROUP)])
    a0, a1, a2, pgq, pgkt, pgv, pgr, pgd, pgate = _proj_call(
        h, mod, row(g_mix), w_in_t, hsum_qk, hexp_qk, row(qkg), l)

    outs = [_attn_call(a, gi) for gi, a in enumerate((a0, a1, a2))]
    up = jnp.pad(bf(gate_up), ((0, GD_PAD - GLA_GATE_RANK), (0, 0)))
    ogla = _gla_call(pgq, pgkt, pgv, pgd, up, row(gate_bias))

    head_o = _head_matrix(128, ATT_OUT_WIDTH)
    hexp_o = jnp.asarray(np.concatenate([head_o, head_o]), BF)
    return _merge_ffn_call(h, mod, [o for o, _ in outs], [l for _, l in outs], ogla, pgr, pgate,
                           hexp_o, row(jnp.tile(out_norm_g, GLA_HEADS)), bf(w_branch_att),
                           bf(w_branch_gla), bf(w_out), row(g_ffn2), f2w1, f2w3, f2w2, l)


def kernel(x, c, w_mod, b_mod, g_ffn1, ffn1_w1, ffn1_w3, ffn1_w2, g_mix, w_in, q_norm_g, k_norm_g,
           gla_gate_up, gla_gate_bias, gla_out_norm_g, w_branch_att, w_branch_gla, w_out,
           g_ffn2, ffn2_w1, ffn2_w3, ffn2_w2):
    h = x
    for l in range(w_mod.shape[0]):
        mod = _mod_call(c, w_mod, b_mod[l], l).reshape(BATCH, N_MOD, D_MODEL)
        h = _layer(l, h, mod, g_ffn1[l], ffn1_w1, ffn1_w3, ffn1_w2, g_mix[l], w_in,
                   q_norm_g[l], k_norm_g[l], gla_gate_up[l], gla_gate_bias[l], gla_out_norm_g[l],
                   w_branch_att[l], w_branch_gla[l], w_out[l], g_ffn2[l], ffn2_w1, ffn2_w3,
                   ffn2_w2)
    return h
```

```python
import functools

import numpy as np
import jax
import jax.numpy as jnp
from jax import lax
from jax.experimental import pallas as pl
from jax.experimental.pallas import tpu as pltpu

D_MODEL = 1024
BATCH = 16
SEQ = 2048
N_MOD = 9
D_FF = 2816
EPS = 1e-6
ATT_GROUPS = ((128, 1), (512, 4), (2048, 16))
ATT_HEADS_PER_GROUP = 4
ATT_HEADS = 12
ATT_HEAD_DIM = 64
ATT_WIDTH = 768
ATT_OUT_WIDTH = 256
ALIBI_MAX = 8.0
GLA_HEADS = 4
GLA_KEY_DIM = 512
GLA_VAL_DIM = 1024
GLA_DK = 128
GLA_DV = 256
GLA_GATE_RANK = 16
GLA_TAU = 16.0
GLA_CHUNK = 64
ATT_BLK = 128
NEG_BIG = -1e30
LOG2E = float(np.log2(np.e))
LN2 = float(np.log(2.0))

BF = jnp.bfloat16
F32 = jnp.float32

VMEM_LIMIT_BYTES = 56 * 1024 * 1024

ROW_TILE = 512
FFN_TILE = 1024
FFN_SUBTILES = 2
MXU_TILE = 256
FF_CHUNK_TILES = 4
GLA_BLOCK = 512
MOD_BLOCK_COLS = 2304


def _dot(a, b):
    return jnp.dot(a, b, preferred_element_type=F32)


def _dot_nt(a, b):
    return lax.dot_general(a, b, (((1,), (1,)), ((), ())), preferred_element_type=F32)


def _split_bf16(x):
    hi = x.astype(BF)
    lo = (x - hi.astype(F32)).astype(BF)
    return hi, lo


def _dot_split(x, stacked_ref):
    hi, lo = _split_bf16(x)
    return _dot(jnp.concatenate([hi, lo], axis=1), stacked_ref[...])


def _sigmoid(x):
    return 0.5 * jnp.tanh(0.5 * x) + 0.5


def _silu(x):
    return x * _sigmoid(x)


def _const_spec(shape):
    nd = len(shape)
    return pl.BlockSpec(shape, lambda *_: (0,) * nd, pipeline_mode=pl.Buffered(1))


def _dilated_spec(d, width):
    return pl.BlockSpec((1, d, ROW_TILE // d, width), lambda b, i: (b, 0, i, 0))


def _params(*sem):
    return pltpu.CompilerParams(dimension_semantics=sem, vmem_limit_bytes=VMEM_LIMIT_BYTES)


def _mod_body(c_ref, w_ref, b_ref, o_ref):
    c = c_ref[...]
    o_ref[...] = _dot(_silu(c).astype(BF), w_ref[...].astype(BF)) + b_ref[...]


def _mod_call(c, w_mod, b_mod, layer):
    n = w_mod.shape[2]
    bn = MOD_BLOCK_COLS
    return pl.pallas_call(
        _mod_body,
        grid=(n // bn,),
        in_specs=[pl.BlockSpec((BATCH, D_MODEL), lambda j: (0, 0)),
                  pl.BlockSpec((None, D_MODEL, bn), lambda j: (layer, 0, j)),
                  pl.BlockSpec((1, bn), lambda j: (0, j))],
        out_specs=pl.BlockSpec((BATCH, bn), lambda j: (0, j)),
        out_shape=jax.ShapeDtypeStruct((BATCH, n), F32),
        compiler_params=_params("arbitrary"),
        name="mod",
    )(c, w_mod, b_mod.reshape(1, n))


def _row_scale(x):
    return lax.rsqrt(jnp.mean(x * x, axis=-1, keepdims=True) + EPS)


STAGE_SLOTS = 3
UP_PIECE_ROWS = 128
DOWN_PIECE_ROWS = 256

def _ffn_weight_scratch(slots):
    return [
        pltpu.VMEM((D_MODEL, D_FF), BF), pltpu.VMEM((D_MODEL, D_FF), BF),
        pltpu.VMEM((D_FF, D_MODEL), BF),
        pltpu.VMEM((BATCH, D_FF), F32), pltpu.VMEM((BATCH, D_FF), F32),
        pltpu.VMEM((slots, UP_PIECE_ROWS, D_FF), F32),
        pltpu.VMEM((slots, DOWN_PIECE_ROWS, D_MODEL), F32),
        pltpu.SemaphoreType.DMA((slots,)), pltpu.SemaphoreType.DMA((slots,))]


def _stream_cast(pieces, stage_ref, sem_ref, store, cast=True):
    slots = stage_ref.shape[0]

    def copy(i):
        return pltpu.make_async_copy(pieces[i], stage_ref.at[i % slots], sem_ref.at[i % slots])

    for i in range(min(slots - 1, len(pieces))):
        copy(i).start()
    for i in range(len(pieces)):
        if i + slots - 1 < len(pieces):
            copy(i + slots - 1).start()
        copy(i).wait()
        v = stage_ref[i % slots]
        store(i, v.astype(BF) if cast else v)


def _load_ffn_weights(layer, mod_row, modall_ref, w1_hbm, w3_hbm, w2_hbm,
                      w1_ref, w3_ref, w2_ref, s1_ref, s3_ref, stage_up, stage_dn, sem_up, sem_dn):
    up_rows = [pl.ds(r, UP_PIECE_ROWS) for r in range(0, D_MODEL, UP_PIECE_ROWS)]
    dn_rows = [pl.ds(r, DOWN_PIECE_ROWS) for r in range(0, D_FF, DOWN_PIECE_ROWS)]

    def store_up(i, v):
        dst = w1_ref if i < len(up_rows) else w3_ref
        dst[up_rows[i % len(up_rows)], :] = v

    def store_dn(i, v):
        w2_ref[dn_rows[i], :] = v

    _stream_cast([w.at[layer, r, :] for w in (w1_hbm, w3_hbm) for r in up_rows],
                 stage_up, sem_up, store_up)
    _stream_cast([w2_hbm.at[layer, r, :] for r in dn_rows], stage_dn, sem_dn, store_dn)
    shift = modall_ref[:, mod_row, :].astype(BF)
    s1_ref[...] = _dot(shift, w1_ref[...])
    s3_ref[...] = _dot(shift, w3_ref[...])


def _first_step():
    return jnp.logical_and(pl.program_id(0) == 0, pl.program_id(1) == 0)


def _ffn_tiles(xs, m, mod_row, g_ref, s1, s3, w1_ref, w3_ref, w2_ref):
    scale, gate = m[mod_row + 1:mod_row + 2], m[mod_row + 2:mod_row + 3]
    fc = FF_CHUNK_TILES * MXU_TILE
    gs = g_ref[...] * (1.0 + scale)
    us = [(x * gs).astype(BF) for x in xs]
    rs = [_row_scale(x) for x in xs]
    accs = [None] * len(xs)
    for lo in range(0, D_FF, fc):
        cols = slice(lo, min(lo + fc, D_FF))
        for s, u in enumerate(us):
            h1 = rs[s] * _dot(u, w1_ref[:, cols]) + s1[:, cols]
            h3 = rs[s] * _dot(u, w3_ref[:, cols]) + s3[:, cols]
            p = _dot((_silu(h1) * h3).astype(BF), w2_ref[cols, :])
            accs[s] = p if accs[s] is None else accs[s] + p
    return [x + (0.5 * (1.0 + gate)) * acc for x, acc in zip(xs, accs)]


def _ffn_body(x_ref, mod_ref, modall_ref, g_ref, w1_hbm, w3_hbm, w2_hbm, o_ref,
              w1_ref, w3_ref, w2_ref, s1_ref, s3_ref, *dma_scratch, mod_row, layer):
    @pl.when(_first_step())
    def _():
        _load_ffn_weights(layer, mod_row, modall_ref, w1_hbm, w3_hbm, w2_hbm,
                          w1_ref, w3_ref, w2_ref, s1_ref, s3_ref, *dma_scratch)

    b = pl.program_id(0)
    sub = FFN_TILE // FFN_SUBTILES
    rows = [slice(s * sub, (s + 1) * sub) for s in range(FFN_SUBTILES)]
    outs = _ffn_tiles([x_ref[0, r, :] for r in rows], mod_ref[0], mod_row, g_ref,
                      s1_ref[pl.ds(b, 1), :], s3_ref[pl.ds(b, 1), :], w1_ref, w3_ref, w2_ref)
    for r, o in zip(rows, outs):
        o_ref[0, r, :] = o


def _ffn_call(x, mod, g, w1, w3, w2, mod_row, layer):
    row = pl.BlockSpec((1, FFN_TILE, D_MODEL), lambda b, i: (b, i, 0))
    hbm = pl.BlockSpec(memory_space=pl.ANY)
    return pl.pallas_call(
        functools.partial(_ffn_body, mod_row=mod_row, layer=layer),
        grid=(BATCH, SEQ // FFN_TILE),
        in_specs=[row,
                  pl.BlockSpec((1, N_MOD, D_MODEL), lambda b, i: (b, 0, 0)),
                  _const_spec((BATCH, N_MOD, D_MODEL)),
                  _const_spec((1, D_MODEL)),
                  hbm, hbm, hbm],
        out_specs=row,
        out_shape=jax.ShapeDtypeStruct((BATCH, SEQ, D_MODEL), F32),
        scratch_shapes=_ffn_weight_scratch(STAGE_SLOTS),
        compiler_params=_params("arbitrary", "arbitrary"),
        name="ffn%d" % mod_row,
    )(x, mod, mod, g, w1, w3, w2)


QK_W = 2 * ATT_OUT_WIDTH
GRP_W = 3 * ATT_OUT_WIDTH
GD_PAD = 128


IN_GQ = 3 * ATT_WIDTH
IN_GK = IN_GQ + GLA_KEY_DIM
IN_GV = IN_GK + GLA_KEY_DIM
IN_GR = IN_GV + GLA_VAL_DIM
IN_GD = IN_GR + GLA_VAL_DIM
IN_GATES = IN_GD + GLA_GATE_RANK
IN_PIECE_ROWS = 256

PROJ_WEIGHT_SCRATCH = [pltpu.VMEM((n, D_MODEL), BF) for n in
                       (3 * GRP_W, GLA_KEY_DIM, GLA_KEY_DIM, GLA_VAL_DIM, GLA_VAL_DIM, GD_PAD,
                        2 * D_MODEL)]
PROJ_SHIFT_SCRATCH = [pltpu.VMEM((BATCH, n), F32) for n in
                      (3 * GRP_W, GLA_KEY_DIM, GLA_VAL_DIM, GLA_VAL_DIM, GD_PAD, 2 * D_MODEL)]
PROJ_DMA_SCRATCH = [
    pltpu.VMEM((STAGE_SLOTS, IN_PIECE_ROWS, D_MODEL), F32),
    pltpu.VMEM((GLA_GATE_RANK, D_MODEL), F32),
    pltpu.SemaphoreType.DMA((STAGE_SLOTS,)), pltpu.SemaphoreType.DMA(())]


def _load_proj_weights(layer, modall_ref, wint_hbm, weights, shifts,
                       stage_ref, gd_stage_ref, sem_ref, gd_sem):
    watt_ref, wgq_ref, wgk_ref, wgv_ref, wgr_ref, wgd_ref, wgate_ref = weights
    P, W = IN_PIECE_ROWS, ATT_OUT_WIDTH
    moves = [(t * ATT_WIDTH + gi * W, watt_ref, gi * GRP_W + t * W)
             for gi in range(len(ATT_GROUPS)) for t in range(3)]
    for src, dst in ((IN_GQ, wgq_ref), (IN_GK, wgk_ref), (IN_GV, wgv_ref), (IN_GR, wgr_ref),
                     (IN_GATES, wgate_ref)):
        moves += [(src + r, dst, r) for r in range(0, dst.shape[0], P)]

    def store(i, v):
        _, dst, r = moves[i]
        dst[pl.ds(r, P), :] = v

    _stream_cast([wint_hbm.at[layer, pl.ds(src, P), :] for src, _, _ in moves],
                 stage_ref, sem_ref, store)

    gd_copy = pltpu.make_async_copy(wint_hbm.at[layer, pl.ds(IN_GD, GLA_GATE_RANK), :],
                                    gd_stage_ref, gd_sem)
    gd_copy.start()
    gd_copy.wait()
    wgd_ref[...] = jnp.zeros_like(wgd_ref)
    wgd_ref[:GLA_GATE_RANK, :] = gd_stage_ref[...].astype(BF)

    shift = modall_ref[:, 3, :].astype(BF)
    for s_ref, w_ref in zip(shifts, (watt_ref, wgq_ref, wgv_ref, wgr_ref, wgd_ref, wgate_ref)):
        s_ref[...] = _dot_nt(shift, w_ref[...])


def _proj_body(h_ref, mod_ref, modall_ref, g_ref, hsum_ref, hexp_ref, qkg_ref, wint_hbm,
               a0_ref, a1_ref, a2_ref, gq_ref, gkt_ref, gv_ref, gr_ref, gd_ref, gate_ref, perm_ref,
               watt_ref, wgq_ref, wgk_ref, wgv_ref, wgr_ref, wgd_ref, wgate_ref,
               satt_ref, sgq_ref, sgv_ref, sgr_ref, sgd_ref, sgate_ref, *dma_scratch, layer):
    @pl.when(_first_step())
    def _():
        _load_proj_weights(layer, modall_ref, wint_hbm,
                           (watt_ref, wgq_ref, wgk_ref, wgv_ref, wgr_ref, wgd_ref, wgate_ref),
                           (satt_ref, sgq_ref, sgv_ref, sgr_ref, sgd_ref, sgate_ref), *dma_scratch)

    b = pl.ds(pl.program_id(0), 1)
    m = mod_ref[0]
    x = h_ref[0]
    xg = x * (g_ref[...] * (1.0 + m[4:5]))
    u = xg.astype(BF)
    rs = _row_scale(x)

    def proj(wt, s):
        return rs * _dot_nt(u, wt) + s

    for gi, a_ref in enumerate((a0_ref, a1_ref, a2_ref)):
        d = ATT_GROUPS[gi][1]
        grp = slice(gi * GRP_W, (gi + 1) * GRP_W)
        y = proj(watt_ref[grp, :], satt_ref[b, grp])
        qk = y[:, :QK_W]
        ss = _dot((qk * qk).astype(BF), hsum_ref[...])
        rx = _dot_split(lax.rsqrt(ss * (1.0 / ATT_HEAD_DIM) + EPS), hexp_ref)
        qkn = qk * rx * qkg_ref[...]
        if d == 1:
            a_ref[0, 0, :, :QK_W] = qkn.astype(BF)
            a_ref[0, 0, :, QK_W:] = y[:, QK_W:].astype(BF)
        else:
            for j in range(GRP_W // 128):
                cols = slice(j * 128, (j + 1) * 128)
                perm_ref[j] = qkn[:, cols] if j < QK_W // 128 else y[:, cols]
            for r in range(d):
                for j in range(GRP_W // 128):
                    a_ref[0, r, :, j * 128:(j + 1) * 128] = (
                        perm_ref[j, pl.ds(r, ROW_TILE // d, stride=d), :].astype(BF))
    gq_ref[0] = proj(wgq_ref[...], sgq_ref[b, :]).astype(BF)
    gv_ref[0] = proj(wgv_ref[...], sgv_ref[b, :]).astype(BF)
    gr_ref[0] = proj(wgr_ref[...], sgr_ref[b, :]).astype(BF)
    gd_ref[0] = proj(wgd_ref[...], sgd_ref[b, :]).astype(BF)
    gate_ref[0] = proj(wgate_ref[...], sgate_ref[b, :]).astype(BF)
    u_full = (xg * rs + m[3:4]).astype(BF)
    gkt_ref[0] = _dot_nt(wgk_ref[...], u_full).astype(BF)


def _proj_call(h, mod, g, w_in_t, hsum, hexp, qkg, layer):
    def row(w):
        return pl.BlockSpec((1, ROW_TILE, w), lambda b, i: (b, i, 0))

    def out(w):
        return jax.ShapeDtypeStruct((BATCH, SEQ, w), BF)

    kt_spec = pl.BlockSpec((1, GLA_KEY_DIM, ROW_TILE), lambda b, i: (b, 0, i))
    kt_shape = jax.ShapeDtypeStruct((BATCH, GLA_KEY_DIM, SEQ), BF)
    widths = (GLA_VAL_DIM, GLA_VAL_DIM, GD_PAD, 2 * D_MODEL)
    consts = (mod, g, hsum, hexp, qkg)
    dils = [d for _, d in ATT_GROUPS]
    return pl.pallas_call(
        functools.partial(_proj_body, layer=layer),
        grid=(BATCH, SEQ // ROW_TILE),
        in_specs=[row(D_MODEL), pl.BlockSpec((1, N_MOD, D_MODEL), lambda b, i: (b, 0, 0))]
                 + [_const_spec(a.shape) for a in consts]
                 + [pl.BlockSpec(memory_space=pl.ANY)],
        out_specs=[_dilated_spec(d, GRP_W) for d in dils] + [row(GLA_KEY_DIM), kt_spec]
                  + [row(w) for w in widths],
        out_shape=[jax.ShapeDtypeStruct((BATCH, d, SEQ // d, GRP_W), BF) for d in dils]
                  + [out(GLA_KEY_DIM), kt_shape] + [out(w) for w in widths],
        scratch_shapes=[pltpu.VMEM((GRP_W // 128, ROW_TILE, 128), F32)]
                       + PROJ_WEIGHT_SCRATCH + PROJ_SHIFT_SCRATCH + PROJ_DMA_SCRATCH,
        compiler_params=_params("arbitrary", "arbitrary"),
        name="proj",
    )(h, mod, *consts, w_in_t)


ATT_GROUP_BLOCKS = 4


def _attn_blocks(a_ref, bias_ref, o_ref, l_ref, blocks):
    W, NH = ATT_OUT_WIDTH, ATT_HEADS_PER_GROUP
    lane = lax.broadcasted_iota(jnp.int32, (ATT_BLK, 128), 1)
    low_half = lane < ATT_HEAD_DIM
    zero_col = jnp.zeros((ATT_BLK, 128), BF)
    qs, ks, vs, biases, q0s = [], [], [], [], []
    for r, n, first in blocks:
        if first:
            q0, k0, nk = 0, 0, ATT_BLK
            biases.append(bias_ref[:, ATT_BLK:])
        else:
            q0, k0, nk = n * ATT_BLK, (n - 1) * ATT_BLK, 2 * ATT_BLK
            if not isinstance(n, int):
                q0, k0 = pl.multiple_of(q0, ATT_BLK), pl.multiple_of(k0, ATT_BLK)
            biases.append(bias_ref[...])
        q0s.append(q0)
        stacked = []
        for j in range(NH):
            col = a_ref[0, r, pl.ds(q0, ATT_BLK), (j // 2) * 128:(j // 2 + 1) * 128]
            col = jnp.where(low_half == (j % 2 == 0), col, zero_col)
            stacked.append(jnp.concatenate([col, zero_col] if j < 2 else [zero_col, col], axis=1))
        qs.append(jnp.concatenate(stacked, axis=0))
        ks.append(a_ref[0, r, pl.ds(k0, nk), W:2 * W])
        vs.append(a_ref[0, r, pl.ds(k0, nk), 2 * W:])
    ss = [_dot_nt(q4, k) + b for q4, k, b in zip(qs, ks, biases)]
    mxs = [jnp.max(s, axis=-1, keepdims=True) for s in ss]
    es = [jnp.exp2(s - mx) for s, mx in zip(ss, mxs)]
    dens = [jnp.sum(e, axis=-1, keepdims=True) for e in es]
    pvs = [_dot(e.astype(BF), v) for e, v in zip(es, vs)]
    for (r, _, _), q0, pv, mx, den in zip(blocks, q0s, pvs, mxs, dens):
        rden = 1.0 / den
        lse = mx * LN2 + jnp.log(den)
        l = jnp.zeros((ATT_BLK, 128), F32)
        heads = []
        for j in range(NH):
            rows = slice(j * ATT_BLK, (j + 1) * ATT_BLK)
            heads.append(pv[rows, (j // 2) * 128:(j // 2 + 1) * 128] * rden[rows])
            l = jnp.where(lane == j, lse[rows], l)
        o = jnp.concatenate([jnp.where(low_half, heads[0], heads[1]),
                             jnp.where(low_half, heads[2], heads[3])], axis=1)
        o_ref[0, r, pl.ds(q0, ATT_BLK), :] = o.astype(o_ref.dtype)
        l_ref[0, r, pl.ds(q0, ATT_BLK), :] = l


def _attn_body(a_ref, bias_ref, o_ref, l_ref, *, subseqs, n_blocks):
    G = ATT_GROUP_BLOCKS
    if n_blocks == 1:
        for r0 in range(0, subseqs, G):
            _attn_blocks(a_ref, bias_ref, o_ref, l_ref,
                         [(r, 0, True) for r in range(r0, min(r0 + G, subseqs))])
        return
    for r in range(subseqs):
        for n0 in range(0, n_blocks, G):
            _attn_blocks(a_ref, bias_ref, o_ref, l_ref,
                         [(r, n, n == 0) for n in range(n0, n0 + G)])


def _attn_bias(group):
    _, dilation = ATT_GROUPS[group]
    heads = np.arange(group * ATT_HEADS_PER_GROUP, (group + 1) * ATT_HEADS_PER_GROUP)
    slopes = (2.0 ** (-ALIBI_MAX * (heads + 1).astype(np.float32) / ATT_HEADS)).astype(np.float32)
    qi = np.arange(ATT_BLK)[:, None]
    kj = np.arange(2 * ATT_BLK)[None, :]
    steps = qi + ATT_BLK - kj
    valid = (steps >= 0) & (steps <= ATT_BLK)
    bias = -(slopes * np.float32(dilation))[:, None, None] * steps.astype(np.float32)[None]
    bias = np.where(valid[None], bias * np.float32(LOG2E), np.float32(NEG_BIG)).astype(np.float32)
    return bias.reshape(ATT_HEADS_PER_GROUP * ATT_BLK, 2 * ATT_BLK)


def _attn_call(a, group):
    _, d = ATT_GROUPS[group]
    L = SEQ // d
    n_blocks = L // ATT_BLK
    subseqs = d if n_blocks == 1 else min(d, 4)

    def spec(w):
        return pl.BlockSpec((1, subseqs, L, w), lambda b, r: (b, r, 0, 0))

    return pl.pallas_call(
        functools.partial(_attn_body, subseqs=subseqs, n_blocks=n_blocks),
        grid=(BATCH, d // subseqs),
        in_specs=[spec(GRP_W), _const_spec((ATT_HEADS_PER_GROUP * ATT_BLK, 2 * ATT_BLK))],
        out_specs=[spec(ATT_OUT_WIDTH), spec(128)],
        out_shape=[jax.ShapeDtypeStruct((BATCH, d, L, ATT_OUT_WIDTH), BF),
                   jax.ShapeDtypeStruct((BATCH, d, L, 128), F32)],
        compiler_params=_params("parallel", "parallel"),
        name="attn%d" % d,
    )(a, jnp.asarray(_attn_bias(group)))


GLA_PAIR = 2 * GLA_CHUNK


def _gla_cum_decay(gd_ref, up_ref, gb_ref, tril2_ref):
    x = _dot(gd_ref[0], up_ref[...]) + gb_ref[...]
    log_a = (jnp.minimum(x, 0.0) - jnp.log(1.0 + jnp.exp(-jnp.abs(x)))) * (1.0 / GLA_TAU)
    la_hi, la_lo = _split_bf16(log_a)
    pieces = []
    for p in range(GLA_BLOCK // GLA_PAIR):
        rows = slice(p * GLA_PAIR, (p + 1) * GLA_PAIR)
        pieces.append(_dot(tril2_ref[...], jnp.concatenate([la_hi[rows], la_lo[rows]], axis=0)))
    return pieces


def _gla_pair(p, b, states, gq_ref, gkt_ref, gv_ref, o_ref):
    C, P = GLA_CHUNK, GLA_PAIR
    rows = slice(p * P, (p + 1) * P)
    q_dec =(gq_ref[0, rows, :].astype(F32) * (GLA_DK ** -0.5) * jnp.exp(b)).astype(BF)
    b_t = b.T
    k_dec_t_bf = (gkt_ref[0, :, rows].astype(F32) * jnp.exp(-b_t)).astype(BF)
    decay_t = jnp.exp(b_t)
    lane = lax.broadcasted_iota(jnp.int32, (GLA_KEY_DIM, P), 1)
    zero = jnp.zeros_like(k_dec_t_bf)
    k_dec_t_chunk = [jnp.where(lane < C, k_dec_t_bf, zero), jnp.where(lane >= C, k_dec_t_bf, zero)]
    ri = lax.broadcasted_iota(jnp.int32, (P, P), 0)
    ci = lax.broadcasted_iota(jnp.int32, (P, P), 1)
    causal = (ri >= ci) & ((ri >= C) == (ci >= C))
    new_states = []
    for h in range(GLA_HEADS):
        kc = slice(h * GLA_DK, (h + 1) * GLA_DK)
        vc = slice(h * GLA_DV, (h + 1) * GLA_DV)
        v = gv_ref[0, rows, vc]
        attn = jnp.where(causal, _dot(q_dec[:, kc], k_dec_t_bf[kc, :]), 0.0).astype(BF)
        stacked = _dot(jnp.concatenate([k_dec_t_chunk[0][kc, :], k_dec_t_chunk[1][kc, :], attn],
                                       axis=0), v)
        st = states[h]
        for cc in range(2):
            crow = slice(cc * C, (cc + 1) * C)
            o = stacked[2 * GLA_DK + cc * C:2 * GLA_DK + (cc + 1) * C] + _dot(q_dec[crow, kc],
                                                                              st.astype(BF))
            o_ref[0, p * P + cc * C:p * P + (cc + 1) * C, vc] = o.astype(o_ref.dtype)
            last = (cc + 1) * C - 1
            st = decay_t[kc, last:last + 1] * (st + stacked[cc * GLA_DK:(cc + 1) * GLA_DK])
        new_states.append(st)
    return new_states


def _gla_body(gq_ref, gkt_ref, gv_ref, gd_ref, up_ref, gb_ref, tril2_ref, o_ref, state_ref):
    @pl.when(pl.program_id(1) == 0)
    def _():
        state_ref[...] = jnp.zeros_like(state_ref)

    bs = _gla_cum_decay(gd_ref, up_ref, gb_ref, tril2_ref)
    states = [state_ref[h] for h in range(GLA_HEADS)]
    for p in range(GLA_BLOCK // GLA_PAIR):
        states = _gla_pair(p, bs[p], states, gq_ref, gkt_ref, gv_ref, o_ref)
    for h in range(GLA_HEADS):
        state_ref[h] = states[h]


def _pair_tril2():
    i = np.arange(GLA_PAIR)
    t = ((i[:, None] // GLA_CHUNK) == (i[None, :] // GLA_CHUNK)) & (i[:, None] >= i[None, :])
    return np.concatenate([t, t], axis=1).astype(np.float32)


def _gla_call(gq, gkt, gv, gd, up, gbias):
    def row(w):
        return pl.BlockSpec((1, GLA_BLOCK, w), lambda b, i: (b, i, 0))

    consts = (up, gbias, jnp.asarray(_pair_tril2(), BF))
    return pl.pallas_call(
        _gla_body,
        grid=(BATCH, SEQ // GLA_BLOCK),
        in_specs=[row(GLA_KEY_DIM),
                  pl.BlockSpec((1, GLA_KEY_DIM, GLA_BLOCK), lambda b, i: (b, 0, i)),
                  row(GLA_VAL_DIM), row(GD_PAD)]
                 + [_const_spec(a.shape) for a in consts],
        out_specs=row(GLA_VAL_DIM),
        out_shape=jax.ShapeDtypeStruct((BATCH, SEQ, GLA_VAL_DIM), BF),
        scratch_shapes=[pltpu.VMEM((GLA_HEADS, GLA_DK, GLA_DV), F32)],
        compiler_params=_params("parallel", "arbitrary"),
        name="gla",
    )(gq, gkt, gv, gd, *consts)


def _undilate(src_ref, scr_ref, d):
    if d == 1:
        return src_ref[0, 0].astype(F32)
    planes = scr_ref.shape[0]
    for r in range(d):
        x = src_ref[0, r].astype(F32)
        for j in range(planes):
            scr_ref[j, pl.ds(r, ROW_TILE // d, stride=d), :] = x[:, j * 128:(j + 1) * 128]
    return jnp.concatenate([scr_ref[j] for j in range(planes)], axis=-1)


def _merge_body(h_ref, mod_ref, o0_ref, o1_ref, o2_ref, l0_ref, l1_ref, l2_ref,
                ogla_ref, gr_ref, gate_ref, modall_ref, hexp_ref, gain_ref, wba_ref, wbg_ref,
                wout_ref, g_ref, w1_hbm, w3_hbm, w2_hbm, out_ref,
                os1_ref, os2_ref, ls1_ref, ls2_ref,
                w1_ref, w3_ref, w2_ref, s1_ref, s3_ref, *dma_scratch, layer):
    @pl.when(_first_step())
    def _():
        _load_ffn_weights(layer, 6, modall_ref, w1_hbm, w3_hbm, w2_hbm,
                          w1_ref, w3_ref, w2_ref, s1_ref, s3_ref, *dma_scratch)

    dils = [d for _, d in ATT_GROUPS]
    lses = [_undilate(r, s, d) for r, s, d in
            zip((l0_ref, l1_ref, l2_ref), (None, ls1_ref, ls2_ref), dils)]
    outs = [_undilate(r, s, d) for r, s, d in
            zip((o0_ref, o1_ref, o2_ref), (None, os1_ref, os2_ref), dils)]
    mx = jnp.maximum(jnp.maximum(lses[0], lses[1]), lses[2])
    es = [jnp.exp(l - mx) for l in lses]
    den = es[0] + es[1] + es[2]
    o_att = None
    for e, o in zip(es, outs):
        t = _dot_split(e / den, hexp_ref) * o
        o_att = t if o_att is None else o_att + t
    att = _dot(o_att.astype(BF), wba_ref[...])
    gated = []
    for h in range(GLA_HEADS):
        vc = slice(h * GLA_DV, (h + 1) * GLA_DV)
        o = ogla_ref[0, :, vc].astype(F32)
        ms = jnp.mean(o * o, axis=-1, keepdims=True)
        y = o * lax.rsqrt(ms + EPS) * gain_ref[:, vc]
        gated.append(y.astype(BF) * _silu(gr_ref[0, :, vc]))
    gla = _dot(jnp.concatenate(gated, axis=1), wbg_ref[...])
    merged = (_sigmoid(gate_ref[0, :, :D_MODEL]) * att.astype(BF)
              + _sigmoid(gate_ref[0, :, D_MODEL:]) * gla.astype(BF))
    m = mod_ref[0]
    h = h_ref[0] + (1.0 + m[5:6]) * _dot(merged, wout_ref[...])
    b = pl.program_id(0)
    out_ref[0] = _ffn_tiles([h], m, 6, g_ref, s1_ref[pl.ds(b, 1), :], s3_ref[pl.ds(b, 1), :],
                            w1_ref, w3_ref, w2_ref)[0]


def _merge_ffn_call(h, mod, os_, ls_, ogla, gr, gates, hexp, gain, wba, wbg, wout, g, w1, w3, w2,
                    layer):
    def row(w):
        return pl.BlockSpec((1, ROW_TILE, w), lambda b, i: (b, i, 0))

    consts = (mod, hexp, gain, wba, wbg, wout, g)
    hbm = pl.BlockSpec(memory_space=pl.ANY)
    dils = [d for _, d in ATT_GROUPS]
    return pl.pallas_call(
        functools.partial(_merge_body, layer=layer),
        grid=(BATCH, SEQ // ROW_TILE),
        in_specs=[row(D_MODEL), pl.BlockSpec((1, N_MOD, D_MODEL), lambda b, i: (b, 0, 0))]
                 + [_dilated_spec(d, ATT_OUT_WIDTH) for d in dils]
                 + [_dilated_spec(d, 128) for d in dils]
                 + [row(GLA_VAL_DIM), row(GLA_VAL_DIM), row(2 * D_MODEL)]
                 + [_const_spec(a.shape) for a in consts] + [hbm, hbm, hbm],
        out_specs=row(D_MODEL),
        out_shape=jax.ShapeDtypeStruct((BATCH, SEQ, D_MODEL), F32),
        scratch_shapes=[pltpu.VMEM((ATT_OUT_WIDTH // 128, ROW_TILE, 128), F32)] * 2
                       + [pltpu.VMEM((1, ROW_TILE, 128), F32)] * 2
                       + _ffn_weight_scratch(2),
        compiler_params=_params("arbitrary", "arbitrary"),
        name="merge_ffn",
    )(h, mod, *os_, *ls_, ogla, gr, gates, *consts, w1, w3, w2)


def _head_matrix(n_lanes_in, width):
    j = np.arange(n_lanes_in)[:, None]
    c = np.arange(width)[None, :]
    return (c // ATT_HEAD_DIM == j).astype(np.float32)


def _layer(l, h, mod, g_ffn1, f1w1, f1w3, f1w2, g_mix, w_in, q_norm_g, k_norm_g, gate_up, gate_bias,
           out_norm_g, w_branch_att, w_branch_gla, w_out, g_ffn2, f2w1, f2w3, f2w2):
    bf = lambda w: w.astype(BF)
    row = lambda v: v.reshape(1, -1)

    h = _ffn_call(h, mod, row(g_ffn1), f1w1, f1w3, f1w2, 0, l)

    w_in_t = jnp.swapaxes(w_in, 1, 2)
    head_qk = _head_matrix(128, QK_W)
    hsum_qk = jnp.asarray(head_qk.T, BF)
    hexp_qk = jnp.asarray(np.concatenate([head_qk, head_qk]), BF)
    qkg = jnp.concatenate([jnp.tile(q_norm_g * (ATT_HEAD_DIM ** -0.5 * LOG2E), ATT_HEADS_PER_GROUP),
                           jnp.tile(k_norm_g, ATT_HEADS_PER_GROUP)])
    a0, a1, a2, pgq, pgkt, pgv, pgr, pgd, pgate = _proj_call(
        h, mod, row(g_mix), w_in_t, hsum_qk, hexp_qk, row(qkg), l)

    outs = [_attn_call(a, gi) for gi, a in enumerate((a0, a1, a2))]
    up = jnp.pad(bf(gate_up), ((0, GD_PAD - GLA_GATE_RANK), (0, 0)))
    ogla = _gla_call(pgq, pgkt, pgv, pgd, up, row(gate_bias))

    head_o = _head_matrix(128, ATT_OUT_WIDTH)
    hexp_o = jnp.asarray(np.concatenate([head_o, head_o]), BF)
    return _merge_ffn_call(h, mod, [o for o, _ in outs], [l for _, l in outs], ogla, pgr, pgate,
                           hexp_o, row(jnp.tile(out_norm_g, GLA_HEADS)), bf(w_branch_att),
                           bf(w_branch_gla), bf(w_out), row(g_ffn2), f2w1, f2w3, f2w2, l)


def kernel(x, c, w_mod, b_mod, g_ffn1, ffn1_w1, ffn1_w3, ffn1_w2, g_mix, w_in, q_norm_g, k_norm_g,
           gla_gate_up, gla_gate_bias, gla_out_norm_g, w_branch_att, w_branch_gla, w_out,
           g_ffn2, ffn2_w1, ffn2_w3, ffn2_w2):
    h = x
    for l in range(w_mod.shape[0]):
        mod = _mod_call(c, w_mod, b_mod[l], l).reshape(BATCH, N_MOD, D_MODEL)
        h = _layer(l, h, mod, g_ffn1[l], ffn1_w1, ffn1_w3, ffn1_w2, g_mix[l], w_in,
                   q_norm_g[l], k_norm_g[l], gla_gate_up[l], gla_gate_bias[l], gla_out_norm_g[l],
                   w_branch_att[l], w_branch_gla[l], w_out[l], g_ffn2[l], ffn2_w1, ffn2_w3,
                   ffn2_w2)
    return h
```

```python
import functools

import numpy as np
import jax
import jax.numpy as jnp
from jax import lax
from jax.experimental import pallas as pl
from jax.experimental.pallas import tpu as pltpu

D_MODEL = 1024
BATCH = 16
SEQ = 2048
N_MOD = 9
D_FF = 2816
EPS = 1e-6
ATT_GROUPS = ((128, 1), (512, 4), (2048, 16))
ATT_HEADS_PER_GROUP = 4
ATT_HEADS = 12
ATT_HEAD_DIM = 64
ATT_WIDTH = 768
ATT_OUT_WIDTH = 256
ALIBI_MAX = 8.0
GLA_HEADS = 4
GLA_KEY_DIM = 512
GLA_VAL_DIM = 1024
GLA_DK = 128
GLA_DV = 256
GLA_GATE_RANK = 16
GLA_TAU = 16.0
GLA_CHUNK = 64
ATT_BLK = 128
NEG_BIG = -1e30
LOG2E = float(np.log2(np.e))
LN2 = float(np.log(2.0))

BF = jnp.bfloat16
F32 = jnp.float32

VMEM_LIMIT_BYTES = 56 * 1024 * 1024

ROW_TILE = 512
FFN_TILE = 1024
FFN_SUBTILES = 2
MXU_TILE = 256
FF_CHUNK_TILES = 4
GLA_BLOCK = 512
MOD_BLOCK_COLS = 2304


def _dot(a, b):
    return jnp.dot(a, b, preferred_element_type=F32)


def _dot_nt(a, b):
    return lax.dot_general(a, b, (((1,), (1,)), ((), ())), preferred_element_type=F32)


def _split_bf16(x):
    hi = x.astype(BF)
    lo = (x - hi.astype(F32)).astype(BF)
    return hi, lo


def _dot_split(x, stacked_ref):
    hi, lo = _split_bf16(x)
    return _dot(jnp.concatenate([hi, lo], axis=1), stacked_ref[...])


def _sigmoid(x):
    return 0.5 * jnp.tanh(0.5 * x) + 0.5


def _silu(x):
    return x * _sigmoid(x)


def _const_spec(shape):
    nd = len(shape)
    return pl.BlockSpec(shape, lambda *_: (0,) * nd, pipeline_mode=pl.Buffered(1))


def _dilated_spec(d, width):
    return pl.BlockSpec((1, d, ROW_TILE // d, width), lambda b, i: (b, 0, i, 0))


def _params(*sem):
    return pltpu.CompilerParams(dimension_semantics=sem, vmem_limit_bytes=VMEM_LIMIT_BYTES)


def _mod_body(c_ref, w_ref, b_ref, o_ref):
    c = c_ref[...]
    o_ref[...] = _dot(_silu(c).astype(BF), w_ref[...].astype(BF)) + b_ref[...]


def _mod_call(c, w_mod, b_mod, layer):
    n = w_mod.shape[2]
    bn = MOD_BLOCK_COLS
    return pl.pallas_call(
        _mod_body,
        grid=(n // bn,),
        in_specs=[pl.BlockSpec((BATCH, D_MODEL), lambda j: (0, 0)),
                  pl.BlockSpec((None, D_MODEL, bn), lambda j: (layer, 0, j)),
                  pl.BlockSpec((1, bn), lambda j: (0, j))],
        out_specs=pl.BlockSpec((BATCH, bn), lambda j: (0, j)),
        out_shape=jax.ShapeDtypeStruct((BATCH, n), F32),
        compiler_params=_params("arbitrary"),
        name="mod",
    )(c, w_mod, b_mod.reshape(1, n))


def _row_scale(x):
    return lax.rsqrt(jnp.mean(x * x, axis=-1, keepdims=True) + EPS)


STAGE_SLOTS = 3
UP_PIECE_ROWS = 128
DOWN_PIECE_ROWS = 256

def _ffn_weight_scratch(piece_scale=1):
    return [
        pltpu.VMEM((D_MODEL, D_FF), BF), pltpu.VMEM((D_MODEL, D_FF), BF),
        pltpu.VMEM((D_FF, D_MODEL), BF),
        pltpu.VMEM((BATCH, D_FF), F32), pltpu.VMEM((BATCH, D_FF), F32),
        pltpu.VMEM((STAGE_SLOTS, UP_PIECE_ROWS // piece_scale, D_FF), F32),
        pltpu.VMEM((STAGE_SLOTS, DOWN_PIECE_ROWS // piece_scale, D_MODEL), F32),
        pltpu.SemaphoreType.DMA((STAGE_SLOTS,)), pltpu.SemaphoreType.DMA((STAGE_SLOTS,))]


def _stream_cast(pieces, stage_ref, sem_ref, store, cast=True):
    slots = stage_ref.shape[0]

    def copy(i):
        return pltpu.make_async_copy(pieces[i], stage_ref.at[i % slots], sem_ref.at[i % slots])

    for i in range(min(slots - 1, len(pieces))):
        copy(i).start()
    for i in range(len(pieces)):
        if i + slots - 1 < len(pieces):
            copy(i + slots - 1).start()
        copy(i).wait()
        v = stage_ref[i % slots]
        store(i, v.astype(BF) if cast else v)


def _load_ffn_weights(layer, mod_row, modall_ref, w1_hbm, w3_hbm, w2_hbm,
                      w1_ref, w3_ref, w2_ref, s1_ref, s3_ref, stage_up, stage_dn, sem_up, sem_dn):
    up, dn = stage_up.shape[1], stage_dn.shape[1]
    up_rows = [pl.ds(r, up) for r in range(0, D_MODEL, up)]
    dn_rows = [pl.ds(r, dn) for r in range(0, D_FF, dn)]

    def store_up(i, v):
        dst = w1_ref if i < len(up_rows) else w3_ref
        dst[up_rows[i % len(up_rows)], :] = v

    def store_dn(i, v):
        w2_ref[dn_rows[i], :] = v

    _stream_cast([w.at[layer, r, :] for w in (w1_hbm, w3_hbm) for r in up_rows],
                 stage_up, sem_up, store_up)
    _stream_cast([w2_hbm.at[layer, r, :] for r in dn_rows], stage_dn, sem_dn, store_dn)
    shift = modall_ref[:, mod_row, :].astype(BF)
    s1_ref[...] = _dot(shift, w1_ref[...])
    s3_ref[...] = _dot(shift, w3_ref[...])


def _first_step():
    return jnp.logical_and(pl.program_id(0) == 0, pl.program_id(1) == 0)


def _ffn_tiles(xs, m, mod_row, g_ref, s1, s3, w1_ref, w3_ref, w2_ref):
    scale, gate = m[mod_row + 1:mod_row + 2], m[mod_row + 2:mod_row + 3]
    fc = FF_CHUNK_TILES * MXU_TILE
    gs = g_ref[...] * (1.0 + scale)
    us = [(x * gs).astype(BF) for x in xs]
    rs = [_row_scale(x) for x in xs]
    accs = [None] * len(xs)
    for lo in range(0, D_FF, fc):
        cols = slice(lo, min(lo + fc, D_FF))
        for s, u in enumerate(us):
            h1 = rs[s] * _dot(u, w1_ref[:, cols]) + s1[:, cols]
            h3 = rs[s] * _dot(u, w3_ref[:, cols]) + s3[:, cols]
            p = _dot((_silu(h1) * h3).astype(BF), w2_ref[cols, :])
            accs[s] = p if accs[s] is None else accs[s] + p
    return [x + (0.5 * (1.0 + gate)) * acc for x, acc in zip(xs, accs)]


def _ffn_body(x_ref, mod_ref, modall_ref, g_ref, w1_hbm, w3_hbm, w2_hbm, o_ref,
              w1_ref, w3_ref, w2_ref, s1_ref, s3_ref, *dma_scratch, mod_row, layer):
    @pl.when(_first_step())
    def _():
        _load_ffn_weights(layer, mod_row, modall_ref, w1_hbm, w3_hbm, w2_hbm,
                          w1_ref, w3_ref, w2_ref, s1_ref, s3_ref, *dma_scratch)

    b = pl.program_id(0)
    sub = FFN_TILE // FFN_SUBTILES
    rows = [slice(s * sub, (s + 1) * sub) for s in range(FFN_SUBTILES)]
    outs = _ffn_tiles([x_ref[0, r, :] for r in rows], mod_ref[0], mod_row, g_ref,
                      s1_ref[pl.ds(b, 1), :], s3_ref[pl.ds(b, 1), :], w1_ref, w3_ref, w2_ref)
    for r, o in zip(rows, outs):
        o_ref[0, r, :] = o


def _ffn_call(x, mod, g, w1, w3, w2, mod_row, layer):
    row = pl.BlockSpec((1, FFN_TILE, D_MODEL), lambda b, i: (b, i, 0))
    hbm = pl.BlockSpec(memory_space=pl.ANY)
    return pl.pallas_call(
        functools.partial(_ffn_body, mod_row=mod_row, layer=layer),
        grid=(BATCH, SEQ // FFN_TILE),
        in_specs=[row,
                  pl.BlockSpec((1, N_MOD, D_MODEL), lambda b, i: (b, 0, 0)),
                  _const_spec((BATCH, N_MOD, D_MODEL)),
                  _const_spec((1, D_MODEL)),
                  hbm, hbm, hbm],
        out_specs=row,
        out_shape=jax.ShapeDtypeStruct((BATCH, SEQ, D_MODEL), F32),
        scratch_shapes=_ffn_weight_scratch(),
        compiler_params=_params("arbitrary", "arbitrary"),
        name="ffn%d" % mod_row,
    )(x, mod, mod, g, w1, w3, w2)


QK_W = 2 * ATT_OUT_WIDTH
GRP_W = 3 * ATT_OUT_WIDTH
GD_PAD = 128


IN_GQ = 3 * ATT_WIDTH
IN_GK = IN_GQ + GLA_KEY_DIM
IN_GV = IN_GK + GLA_KEY_DIM
IN_GR = IN_GV + GLA_VAL_DIM
IN_GD = IN_GR + GLA_VAL_DIM
IN_GATES = IN_GD + GLA_GATE_RANK
IN_PIECE_ROWS = 256

PROJ_WEIGHT_SCRATCH = [pltpu.VMEM((n, D_MODEL), BF) for n in
                       (3 * GRP_W, GLA_KEY_DIM, GLA_KEY_DIM, GLA_VAL_DIM, GLA_VAL_DIM, GD_PAD,
                        2 * D_MODEL)]
PROJ_SHIFT_SCRATCH = [pltpu.VMEM((BATCH, n), F32) for n in
                      (3 * GRP_W, GLA_KEY_DIM, GLA_VAL_DIM, GLA_VAL_DIM, GD_PAD, 2 * D_MODEL)]
PROJ_DMA_SCRATCH = [
    pltpu.VMEM((STAGE_SLOTS, IN_PIECE_ROWS, D_MODEL), F32),
    pltpu.VMEM((GLA_GATE_RANK, D_MODEL), F32),
    pltpu.SemaphoreType.DMA((STAGE_SLOTS,)), pltpu.SemaphoreType.DMA(())]


def _load_proj_weights(layer, modall_ref, wint_hbm, weights, shifts,
                       stage_ref, gd_stage_ref, sem_ref, gd_sem):
    watt_ref, wgq_ref, wgk_ref, wgv_ref, wgr_ref, wgd_ref, wgate_ref = weights
    P, W = IN_PIECE_ROWS, ATT_OUT_WIDTH
    moves = [(t * ATT_WIDTH + gi * W, watt_ref, gi * GRP_W + t * W)
             for gi in range(len(ATT_GROUPS)) for t in range(3)]
    for src, dst in ((IN_GQ, wgq_ref), (IN_GK, wgk_ref), (IN_GV, wgv_ref), (IN_GR, wgr_ref),
                     (IN_GATES, wgate_ref)):
        moves += [(src + r, dst, r) for r in range(0, dst.shape[0], P)]

    def store(i, v):
        _, dst, r = moves[i]
        dst[pl.ds(r, P), :] = v

    _stream_cast([wint_hbm.at[layer, pl.ds(src, P), :] for src, _, _ in moves],
                 stage_ref, sem_ref, store)

    gd_copy = pltpu.make_async_copy(wint_hbm.at[layer, pl.ds(IN_GD, GLA_GATE_RANK), :],
                                    gd_stage_ref, gd_sem)
    gd_copy.start()
    gd_copy.wait()
    wgd_ref[...] = jnp.zeros_like(wgd_ref)
    wgd_ref[:GLA_GATE_RANK, :] = gd_stage_ref[...].astype(BF)

    shift = modall_ref[:, 3, :].astype(BF)
    for s_ref, w_ref in zip(shifts, (watt_ref, wgq_ref, wgv_ref, wgr_ref, wgd_ref, wgate_ref)):
        s_ref[...] = _dot_nt(shift, w_ref[...])


def _proj_body(h_ref, mod_ref, modall_ref, g_ref, hsum_ref, hexp_ref, qkg_ref, wint_hbm,
               a0_ref, a1_ref, a2_ref, gq_ref, gkt_ref, gv_ref, gr_ref, gd_ref, gate_ref, perm_ref,
               watt_ref, wgq_ref, wgk_ref, wgv_ref, wgr_ref, wgd_ref, wgate_ref,
               satt_ref, sgq_ref, sgv_ref, sgr_ref, sgd_ref, sgate_ref, *dma_scratch, layer):
    @pl.when(_first_step())
    def _():
        _load_proj_weights(layer, modall_ref, wint_hbm,
                           (watt_ref, wgq_ref, wgk_ref, wgv_ref, wgr_ref, wgd_ref, wgate_ref),
                           (satt_ref, sgq_ref, sgv_ref, sgr_ref, sgd_ref, sgate_ref), *dma_scratch)

    b = pl.ds(pl.program_id(0), 1)
    m = mod_ref[0]
    x = h_ref[0]
    xg = x * (g_ref[...] * (1.0 + m[4:5]))
    u = xg.astype(BF)
    rs = _row_scale(x)

    def proj(wt, s):
        return rs * _dot_nt(u, wt) + s

    for gi, a_ref in enumerate((a0_ref, a1_ref, a2_ref)):
        d = ATT_GROUPS[gi][1]
        grp = slice(gi * GRP_W, (gi + 1) * GRP_W)
        y = proj(watt_ref[grp, :], satt_ref[b, grp])
        qk = y[:, :QK_W]
        ss = _dot((qk * qk).astype(BF), hsum_ref[...])
        rx = _dot_split(lax.rsqrt(ss * (1.0 / ATT_HEAD_DIM) + EPS), hexp_ref)
        qkn = qk * rx * qkg_ref[...]
        if d == 1:
            a_ref[0, 0, :, :QK_W] = qkn.astype(BF)
            a_ref[0, 0, :, QK_W:] = y[:, QK_W:].astype(BF)
        else:
            for j in range(GRP_W // 128):
                cols = slice(j * 128, (j + 1) * 128)
                perm_ref[j] = qkn[:, cols] if j < QK_W // 128 else y[:, cols]
            for r in range(d):
                for j in range(GRP_W // 128):
                    a_ref[0, r, :, j * 128:(j + 1) * 128] = (
                        perm_ref[j, pl.ds(r, ROW_TILE // d, stride=d), :].astype(BF))
    gq_ref[0] = proj(wgq_ref[...], sgq_ref[b, :]).astype(BF)
    gv_ref[0] = proj(wgv_ref[...], sgv_ref[b, :]).astype(BF)
    gr_ref[0] = proj(wgr_ref[...], sgr_ref[b, :]).astype(BF)
    gd_ref[0] = proj(wgd_ref[...], sgd_ref[b, :]).astype(BF)
    gate_ref[0] = proj(wgate_ref[...], sgate_ref[b, :]).astype(BF)
    u_full = (xg * rs + m[3:4]).astype(BF)
    gkt_ref[0] = _dot_nt(wgk_ref[...], u_full).astype(BF)


def _proj_call(h, mod, g, w_in_t, hsum, hexp, qkg, layer):
    def row(w):
        return pl.BlockSpec((1, ROW_TILE, w), lambda b, i: (b, i, 0))

    def out(w):
        return jax.ShapeDtypeStruct((BATCH, SEQ, w), BF)

    kt_spec = pl.BlockSpec((1, GLA_KEY_DIM, ROW_TILE), lambda b, i: (b, 0, i))
    kt_shape = jax.ShapeDtypeStruct((BATCH, GLA_KEY_DIM, SEQ), BF)
    widths = (GLA_VAL_DIM, GLA_VAL_DIM, GD_PAD, 2 * D_MODEL)
    consts = (mod, g, hsum, hexp, qkg)
    dils = [d for _, d in ATT_GROUPS]
    return pl.pallas_call(
        functools.partial(_proj_body, layer=layer),
        grid=(BATCH, SEQ // ROW_TILE),
        in_specs=[row(D_MODEL), pl.BlockSpec((1, N_MOD, D_MODEL), lambda b, i: (b, 0, 0))]
                 + [_const_spec(a.shape) for a in consts]
                 + [pl.BlockSpec(memory_space=pl.ANY)],
        out_specs=[_dilated_spec(d, GRP_W) for d in dils] + [row(GLA_KEY_DIM), kt_spec]
                  + [row(w) for w in widths],
        out_shape=[jax.ShapeDtypeStruct((BATCH, d, SEQ // d, GRP_W), BF) for d in dils]
                  + [out(GLA_KEY_DIM), kt_shape] + [out(w) for w in widths],
        scratch_shapes=[pltpu.VMEM((GRP_W // 128, ROW_TILE, 128), F32)]
                       + PROJ_WEIGHT_SCRATCH + PROJ_SHIFT_SCRATCH + PROJ_DMA_SCRATCH,
        compiler_params=_params("arbitrary", "arbitrary"),
        name="proj",
    )(h, mod, *consts, w_in_t)


ATT_GROUP_BLOCKS = 4


def _attn_blocks(a_ref, bias_ref, o_ref, l_ref, blocks):
    W, NH = ATT_OUT_WIDTH, ATT_HEADS_PER_GROUP
    lane = lax.broadcasted_iota(jnp.int32, (ATT_BLK, 128), 1)
    low_half = lane < ATT_HEAD_DIM
    zero_col = jnp.zeros((ATT_BLK, 128), BF)
    qs, ks, vs, biases, q0s = [], [], [], [], []
    for r, n, first in blocks:
        if first:
            q0, k0, nk = 0, 0, ATT_BLK
            biases.append(bias_ref[:, ATT_BLK:])
        else:
            q0, k0, nk = n * ATT_BLK, (n - 1) * ATT_BLK, 2 * ATT_BLK
            if not isinstance(n, int):
                q0, k0 = pl.multiple_of(q0, ATT_BLK), pl.multiple_of(k0, ATT_BLK)
            biases.append(bias_ref[...])
        q0s.append(q0)
        stacked = []
        for j in range(NH):
            col = a_ref[0, r, pl.ds(q0, ATT_BLK), (j // 2) * 128:(j // 2 + 1) * 128]
            col = jnp.where(low_half == (j % 2 == 0), col, zero_col)
            stacked.append(jnp.concatenate([col, zero_col] if j < 2 else [zero_col, col], axis=1))
        qs.append(jnp.concatenate(stacked, axis=0))
        ks.append(a_ref[0, r, pl.ds(k0, nk), W:2 * W])
        vs.append(a_ref[0, r, pl.ds(k0, nk), 2 * W:])
    ss = [_dot_nt(q4, k) + b for q4, k, b in zip(qs, ks, biases)]
    mxs = [jnp.max(s, axis=-1, keepdims=True) for s in ss]
    es = [jnp.exp2(s - mx) for s, mx in zip(ss, mxs)]
    dens = [jnp.sum(e, axis=-1, keepdims=True) for e in es]
    pvs = [_dot(e.astype(BF), v) for e, v in zip(es, vs)]
    for (r, _, _), q0, pv, mx, den in zip(blocks, q0s, pvs, mxs, dens):
        rden = 1.0 / den
        lse = mx * LN2 + jnp.log(den)
        l = jnp.zeros((ATT_BLK, 128), F32)
        heads = []
        for j in range(NH):
            rows = slice(j * ATT_BLK, (j + 1) * ATT_BLK)
            heads.append(pv[rows, (j // 2) * 128:(j // 2 + 1) * 128] * rden[rows])
            l = jnp.where(lane == j, lse[rows], l)
        o = jnp.concatenate([jnp.where(low_half, heads[0], heads[1]),
                             jnp.where(low_half, heads[2], heads[3])], axis=1)
        o_ref[0, r, pl.ds(q0, ATT_BLK), :] = o.astype(o_ref.dtype)
        l_ref[0, r, pl.ds(q0, ATT_BLK), :] = l


def _attn_body(a_ref, bias_ref, o_ref, l_ref, *, subseqs, n_blocks):
    G = ATT_GROUP_BLOCKS
    if n_blocks == 1:
        for r0 in range(0, subseqs, G):
            _attn_blocks(a_ref, bias_ref, o_ref, l_ref,
                         [(r, 0, True) for r in range(r0, min(r0 + G, subseqs))])
        return
    for r in range(subseqs):
        for n0 in range(0, n_blocks, G):
            _attn_blocks(a_ref, bias_ref, o_ref, l_ref,
                         [(r, n, n == 0) for n in range(n0, n0 + G)])


def _attn_bias(group):
    _, dilation = ATT_GROUPS[group]
    heads = np.arange(group * ATT_HEADS_PER_GROUP, (group + 1) * ATT_HEADS_PER_GROUP)
    slopes = (2.0 ** (-ALIBI_MAX * (heads + 1).astype(np.float32) / ATT_HEADS)).astype(np.float32)
    qi = np.arange(ATT_BLK)[:, None]
    kj = np.arange(2 * ATT_BLK)[None, :]
    steps = qi + ATT_BLK - kj
    valid = (steps >= 0) & (steps <= ATT_BLK)
    bias = -(slopes * np.float32(dilation))[:, None, None] * steps.astype(np.float32)[None]
    bias = np.where(valid[None], bias * np.float32(LOG2E), np.float32(NEG_BIG)).astype(np.float32)
    return bias.reshape(ATT_HEADS_PER_GROUP * ATT_BLK, 2 * ATT_BLK)


def _attn_call(a, group):
    _, d = ATT_GROUPS[group]
    L = SEQ // d
    n_blocks = L // ATT_BLK
    subseqs = d if n_blocks == 1 else min(d, 4)

    def spec(w):
        return pl.BlockSpec((1, subseqs, L, w), lambda b, r: (b, r, 0, 0))

    return pl.pallas_call(
        functools.partial(_attn_body, subseqs=subseqs, n_blocks=n_blocks),
        grid=(BATCH, d // subseqs),
        in_specs=[spec(GRP_W), _const_spec((ATT_HEADS_PER_GROUP * ATT_BLK, 2 * ATT_BLK))],
        out_specs=[spec(ATT_OUT_WIDTH), spec(128)],
        out_shape=[jax.ShapeDtypeStruct((BATCH, d, L, ATT_OUT_WIDTH), BF),
                   jax.ShapeDtypeStruct((BATCH, d, L, 128), F32)],
        compiler_params=_params("parallel", "parallel"),
        name="attn%d" % d,
    )(a, jnp.asarray(_attn_bias(group)))


GLA_PAIR = 2 * GLA_CHUNK


def _gla_cum_decay(gd_ref, up_ref, gb_ref, tril2_ref):
    x = _dot(gd_ref[0], up_ref[...]) + gb_ref[...]
    log_a = (jnp.minimum(x, 0.0) - jnp.log(1.0 + jnp.exp(-jnp.abs(x)))) * (1.0 / GLA_TAU)
    la_hi, la_lo = _split_bf16(log_a)
    pieces = []
    for p in range(GLA_BLOCK // GLA_PAIR):
        rows = slice(p * GLA_PAIR, (p + 1) * GLA_PAIR)
        pieces.append(_dot(tril2_ref[...], jnp.concatenate([la_hi[rows], la_lo[rows]], axis=0)))
    return pieces


def _gla_pair(p, b, states, gq_ref, gkt_ref, gv_ref, o_ref):
    C, P = GLA_CHUNK, GLA_PAIR
    rows = slice(p * P, (p + 1) * P)
    q_dec =(gq_ref[0, rows, :].astype(F32) * (GLA_DK ** -0.5) * jnp.exp(b)).astype(BF)
    b_t = b.T
    k_dec_t_bf = (gkt_ref[0, :, rows].astype(F32) * jnp.exp(-b_t)).astype(BF)
    decay_t = jnp.exp(b_t)
    lane = lax.broadcasted_iota(jnp.int32, (GLA_KEY_DIM, P), 1)
    zero = jnp.zeros_like(k_dec_t_bf)
    k_dec_t_chunk = [jnp.where(lane < C, k_dec_t_bf, zero), jnp.where(lane >= C, k_dec_t_bf, zero)]
    ri = lax.broadcasted_iota(jnp.int32, (P, P), 0)
    ci = lax.broadcasted_iota(jnp.int32, (P, P), 1)
    causal = (ri >= ci) & ((ri >= C) == (ci >= C))
    new_states = []
    for h in range(GLA_HEADS):
        kc = slice(h * GLA_DK, (h + 1) * GLA_DK)
        vc = slice(h * GLA_DV, (h + 1) * GLA_DV)
        v = gv_ref[0, rows, vc]
        attn = jnp.where(causal, _dot(q_dec[:, kc], k_dec_t_bf[kc, :]), 0.0).astype(BF)
        stacked = _dot(jnp.concatenate([k_dec_t_chunk[0][kc, :], k_dec_t_chunk[1][kc, :], attn],
                                       axis=0), v)
        st = states[h]
        for cc in range(2):
            crow = slice(cc * C, (cc + 1) * C)
            o = stacked[2 * GLA_DK + cc * C:2 * GLA_DK + (cc + 1) * C] + _dot(q_dec[crow, kc],
                                                                              st.astype(BF))
            o_ref[0, p * P + cc * C:p * P + (cc + 1) * C, vc] = o.astype(o_ref.dtype)
            last = (cc + 1) * C - 1
            st = decay_t[kc, last:last + 1] * (st + stacked[cc * GLA_DK:(cc + 1) * GLA_DK])
        new_states.append(st)
    return new_states


def _gla_body(gq_ref, gkt_ref, gv_ref, gd_ref, up_ref, gb_ref, tril2_ref, o_ref, state_ref):
    @pl.when(pl.program_id(1) == 0)
    def _():
        state_ref[...] = jnp.zeros_like(state_ref)

    bs = _gla_cum_decay(gd_ref, up_ref, gb_ref, tril2_ref)
    states = [state_ref[h] for h in range(GLA_HEADS)]
    for p in range(GLA_BLOCK // GLA_PAIR):
        states = _gla_pair(p, bs[p], states, gq_ref, gkt_ref, gv_ref, o_ref)
    for h in range(GLA_HEADS):
        state_ref[h] = states[h]


def _pair_tril2():
    i = np.arange(GLA_PAIR)
    t = ((i[:, None] // GLA_CHUNK) == (i[None, :] // GLA_CHUNK)) & (i[:, None] >= i[None, :])
    return np.concatenate([t, t], axis=1).astype(np.float32)


def _gla_call(gq, gkt, gv, gd, up, gbias):
    def row(w):
        return pl.BlockSpec((1, GLA_BLOCK, w), lambda b, i: (b, i, 0))

    consts = (up, gbias, jnp.asarray(_pair_tril2(), BF))
    return pl.pallas_call(
        _gla_body,
        grid=(BATCH, SEQ // GLA_BLOCK),
        in_specs=[row(GLA_KEY_DIM),
                  pl.BlockSpec((1, GLA_KEY_DIM, GLA_BLOCK), lambda b, i: (b, 0, i)),
                  row(GLA_VAL_DIM), row(GD_PAD)]
                 + [_const_spec(a.shape) for a in consts],
        out_specs=row(GLA_VAL_DIM),
        out_shape=jax.ShapeDtypeStruct((BATCH, SEQ, GLA_VAL_DIM), BF),
        scratch_shapes=[pltpu.VMEM((GLA_HEADS, GLA_DK, GLA_DV), F32)],
        compiler_params=_params("parallel", "arbitrary"),
        name="gla",
    )(gq, gkt, gv, gd, *consts)


def _undilate(src_ref, scr_ref, d):
    if d == 1:
        return src_ref[0, 0].astype(F32)
    planes = scr_ref.shape[0]
    for r in range(d):
        x = src_ref[0, r].astype(F32)
        for j in range(planes):
            scr_ref[j, pl.ds(r, ROW_TILE // d, stride=d), :] = x[:, j * 128:(j + 1) * 128]
    return jnp.concatenate([scr_ref[j] for j in range(planes)], axis=-1)


def _merge_body(h_ref, mod_ref, o0_ref, o1_ref, o2_ref, l0_ref, l1_ref, l2_ref,
                ogla_ref, gr_ref, gate_ref, modall_ref, hexp_ref, gain_ref, wba_ref, wbg_ref,
                wout_ref, g_ref, w1_hbm, w3_hbm, w2_hbm, out_ref,
                os1_ref, os2_ref, ls1_ref, ls2_ref,
                w1_ref, w3_ref, w2_ref, s1_ref, s3_ref, *dma_scratch, layer):
    @pl.when(_first_step())
    def _():
        _load_ffn_weights(layer, 6, modall_ref, w1_hbm, w3_hbm, w2_hbm,
                          w1_ref, w3_ref, w2_ref, s1_ref, s3_ref, *dma_scratch)

    dils = [d for _, d in ATT_GROUPS]
    lses = [_undilate(r, s, d) for r, s, d in
            zip((l0_ref, l1_ref, l2_ref), (None, ls1_ref, ls2_ref), dils)]
    outs = [_undilate(r, s, d) for r, s, d in
            zip((o0_ref, o1_ref, o2_ref), (None, os1_ref, os2_ref), dils)]
    mx = jnp.maximum(jnp.maximum(lses[0], lses[1]), lses[2])
    es = [jnp.exp(l - mx) for l in lses]
    den = es[0] + es[1] + es[2]
    o_att = None
    for e, o in zip(es, outs):
        t = _dot_split(e / den, hexp_ref) * o
        o_att = t if o_att is None else o_att + t
    att = _dot(o_att.astype(BF), wba_ref[...])
    gated = []
    for h in range(GLA_HEADS):
        vc = slice(h * GLA_DV, (h + 1) * GLA_DV)
        o = ogla_ref[0, :, vc].astype(F32)
        ms = jnp.mean(o * o, axis=-1, keepdims=True)
        y = o * lax.rsqrt(ms + EPS) * gain_ref[:, vc]
        gated.append(y.astype(BF) * _silu(gr_ref[0, :, vc]))
    gla = _dot(jnp.concatenate(gated, axis=1), wbg_ref[...])
    merged = (_sigmoid(gate_ref[0, :, :D_MODEL]) * att.astype(BF)
              + _sigmoid(gate_ref[0, :, D_MODEL:]) * gla.astype(BF))
    m = mod_ref[0]
    h = h_ref[0] + (1.0 + m[5:6]) * _dot(merged, wout_ref[...])
    b = pl.program_id(0)
    out_ref[0] = _ffn_tiles([h], m, 6, g_ref, s1_ref[pl.ds(b, 1), :], s3_ref[pl.ds(b, 1), :],
                            w1_ref, w3_ref, w2_ref)[0]


def _merge_ffn_call(h, mod, os_, ls_, ogla, gr, gates, hexp, gain, wba, wbg, wout, g, w1, w3, w2,
                    layer):
    def row(w):
        return pl.BlockSpec((1, ROW_TILE, w), lambda b, i: (b, i, 0))

    consts = (mod, hexp, gain, wba, wbg, wout, g)
    hbm = pl.BlockSpec(memory_space=pl.ANY)
    dils = [d for _, d in ATT_GROUPS]
    return pl.pallas_call(
        functools.partial(_merge_body, layer=layer),
        grid=(BATCH, SEQ // ROW_TILE),
        in_specs=[row(D_MODEL), pl.BlockSpec((1, N_MOD, D_MODEL), lambda b, i: (b, 0, 0))]
                 + [_dilated_spec(d, ATT_OUT_WIDTH) for d in dils]
                 + [_dilated_spec(d, 128) for d in dils]
                 + [row(GLA_VAL_DIM), row(GLA_VAL_DIM), row(2 * D_MODEL)]
                 + [_const_spec(a.shape) for a in consts] + [hbm, hbm, hbm],
        out_specs=row(D_MODEL),
        out_shape=jax.ShapeDtypeStruct((BATCH, SEQ, D_MODEL), F32),
        scratch_shapes=[pltpu.VMEM((ATT_OUT_WIDTH // 128, ROW_TILE, 128), F32)] * 2
                       + [pltpu.VMEM((1, ROW_TILE, 128), F32)] * 2
                       + _ffn_weight_scratch(piece_scale=2),
        compiler_params=_params("arbitrary", "arbitrary"),
        name="merge_ffn",
    )(h, mod, *os_, *ls_, ogla, gr, gates, *consts, w1, w3, w2)


def _head_matrix(n_lanes_in, width):
    j = np.arange(n_lanes_in)[:, None]
    c = np.arange(width)[None, :]
    return (c // ATT_HEAD_DIM == j).astype(np.float32)


def _layer(l, h, mod, g_ffn1, f1w1, f1w3, f1w2, g_mix, w_in, q_norm_g, k_norm_g, gate_up, gate_bias,
           out_norm_g, w_branch_att, w_branch_gla, w_out, g_ffn2, f2w1, f2w3, f2w2):
    bf = lambda w: w.astype(BF)
    row = lambda v: v.reshape(1, -1)

    h = _ffn_call(h, mod, row(g_ffn1), f1w1, f1w3, f1w2, 0, l)

    w_in_t = jnp.swapaxes(w_in, 1, 2)
    head_qk = _head_matrix(128, QK_W)
    hsum_qk = jnp.asarray(head_qk.T, BF)
    hexp_qk = jnp.asarray(np.concatenate([head_qk, head_qk]), BF)
    qkg = jnp.concatenate([jnp.tile(q_norm_g * (ATT_HEAD_DIM ** -0.5 * LOG2E), ATT_HEADS_PER_GROUP),
                           jnp.tile(k_norm_g, ATT_HEADS_PER_GROUP)])
    a0, a1, a2, pgq, pgkt, pgv, pgr, pgd, pgate = _proj_call(
        h, mod, row(g_mix), w_in_t, hsum_qk, hexp_qk, row(qkg), l)

    outs = [_attn_call(a, gi) for gi, a in enumerate((a0, a1, a2))]
    up = jnp.pad(bf(gate_up), ((0, GD_PAD - GLA_GATE_RANK), (0, 0)))
    ogla = _gla_call(pgq, pgkt, pgv, pgd, up, row(gate_bias))

    head_o = _head_matrix(128, ATT_OUT_WIDTH)
    hexp_o = jnp.asarray(np.concatenate([head_o, head_o]), BF)
    return _merge_ffn_call(h, mod, [o for o, _ in outs], [l for _, l in outs], ogla, pgr, pgate,
                           hexp_o, row(jnp.tile(out_norm_g, GLA_HEADS)), bf(w_branch_att),
                           bf(w_branch_gla), bf(w_out), row(g_ffn2), f2w1, f2w3, f2w2, l)


def kernel(x, c, w_mod, b_mod, g_ffn1, ffn1_w1, ffn1_w3, ffn1_w2, g_mix, w_in, q_norm_g, k_norm_g,
           gla_gate_up, gla_gate_bias, gla_out_norm_g, w_branch_att, w_branch_gla, w_out,
           g_ffn2, ffn2_w1, ffn2_w3, ffn2_w2):
    h = x
    for l in range(w_mod.shape[0]):
        mod = _mod_call(c, w_mod, b_mod[l], l).reshape(BATCH, N_MOD, D_MODEL)
        h = _layer(l, h, mod, g_ffn1[l], ffn1_w1, ffn1_w3, ffn1_w2, g_mix[l], w_in,
                   q_norm_g[l], k_norm_g[l], gla_gate_up[l], gla_gate_bias[l], gla_out_norm_g[l],
                   w_branch_att[l], w_branch_gla[l], w_out[l], g_ffn2[l], ffn2_w1, ffn2_w3,
                   ffn2_w2)
    return h
```
